```python
import jax
import jax.numpy as jnp
from jax import lax
import numpy as np

D_MODEL = 1024
BATCH = 2
SEQ = 16384
DEPTH = 2

GRID_W = 64
HEAD_DIM = 64
EPS = 1e-6

NA_HEADS = 6
NA_WIN_H = 8
NA_WIN_W = 16
A_WIDTH = NA_HEADS * HEAD_DIM

MLA_HEADS = 6
MLA_Q_RANK = 256
MLA_KV_RANK = 128
MLA_NOPE = 64
MLA_ROPE = 32
MLA_QK = MLA_NOPE + MLA_ROPE
MLA_V = 64
B_WIDTH = MLA_HEADS * MLA_V
ROPE_THETA = 10000.0
Q_BLOCK = 128

CONV_GROUPS = 4
CONV_CH = CONV_GROUPS * HEAD_DIM
CONV_W = 3
C_WIDTH = CONV_CH

D_MIX = A_WIDTH + B_WIDTH + C_WIDTH
IN_SIZES = (A_WIDTH, A_WIDTH, A_WIDTH, MLA_Q_RANK, MLA_KV_RANK, MLA_ROPE, C_WIDTH, C_WIDTH, C_WIDTH)
IN_COLS = sum(IN_SIZES)

N_EXPERTS = 16
EXPERT_FF = 1024
CAPACITY_FACTOR = 2

kernel_name = 'hybrid_natten_mla_shortconv_ec_moe'


def rms_norm(x, g):
    xf = x.astype(jnp.float32)
    y = xf * lax.rsqrt(jnp.mean(xf * xf, axis=-1, keepdims=True) + EPS)
    return (y * g.astype(jnp.float32)).astype(x.dtype)


def rope(x, pos):
    half = x.shape[-1] // 2
    inv = ROPE_THETA ** (-jnp.arange(half, dtype=jnp.float32) / half)
    ang = pos.astype(jnp.float32)[:, None] * inv[None, :]
    cos = jnp.cos(ang)[None, :, None, :]
    sin = jnp.sin(ang)[None, :, None, :]
    xf = x.astype(jnp.float32)
    x1, x2 = xf[..., :half], xf[..., half:]
    return jnp.concatenate([x1 * cos - x2 * sin, x2 * cos + x1 * sin], axis=-1).astype(x.dtype)


def neighbourhood_attention(q, k, v, rpb):
    b, s, h, dh = q.shape
    rows = s // GRID_W
    wh = min(NA_WIN_H, rows)
    ww = NA_WIN_W
    cols = np.arange(GRID_W)
    col_idx = np.clip(cols - ww // 2, 0, GRID_W - ww)[:, None] + np.arange(ww)[None, :]
    col_off = col_idx - cols[:, None] + (NA_WIN_W - 1)
    qg = q.reshape(b, rows, GRID_W, h, dh).transpose(1, 0, 2, 3, 4)
    kg = k.reshape(b, rows, GRID_W, h, dh)
    vg = v.reshape(b, rows, GRID_W, h, dh)
    scale = dh ** -0.5

    def one_row(args):
        r, q_row = args
        r0 = jnp.clip(r - wh // 2, 0, rows - wh)
        k_win = lax.dynamic_slice_in_dim(kg, r0, wh, axis=1)[:, :, col_idx]
        v_win = lax.dynamic_slice_in_dim(vg, r0, wh, axis=1)[:, :, col_idx]
        row_off = r0 + jnp.arange(wh) - r + (NA_WIN_H - 1)
        bias = rpb[:, row_off[:, None, None], col_off[None, :, :]].transpose(0, 2, 1, 3)
        sc = jnp.einsum('bqhd,biqjhd->bhqij', q_row, k_win).astype(jnp.float32) * scale
        sc = sc + bias.astype(jnp.float32)[None]
        p = jax.nn.softmax(sc.reshape(b, h, GRID_W, wh * ww), axis=-1).reshape(sc.shape).astype(v.dtype)
        return jnp.einsum('bhqij,biqjhd->bqhd', p, v_win)

    o = lax.map(one_row, (jnp.arange(rows), qg))
    return o.transpose(1, 0, 2, 3, 4).reshape(b, s, h * dh)


def mla_attention(q, k, v):
    b, s, h, dq = q.shape
    dv = v.shape[-1]
    nblk = s // Q_BLOCK
    scale = dq ** -0.5
    qb = q.reshape(b, nblk, Q_BLOCK, h, dq).transpose(1, 0, 2, 3, 4)

    def one_block(q_blk):
        sc = jnp.einsum('bqhd,bkhd->bhqk', q_blk, k).astype(jnp.float32) * scale
        p = jax.nn.softmax(sc, axis=-1).astype(v.dtype)
        return jnp.einsum('bhqk,bkhd->bqhd', p, v)

    o = lax.map(one_block, qb)
    return o.transpose(1, 0, 2, 3, 4).reshape(b, s, h * dv)


def short_conv_mixer(hc, bc, cc, conv_w):
    u = cc * hc
    y = lax.conv_general_dilated(
        u, conv_w[:, None, :].astype(u.dtype), window_strides=(1,),
        padding=((CONV_W // 2, CONV_W // 2),), dimension_numbers=('NWC', 'WIO', 'NWC'),
        feature_group_count=u.shape[-1])
    return bc * y


def expert_choice_moe(x, w_router, b_router, w_gate, w_up, w_down):
    b, s, d = x.shape
    cap = CAPACITY_FACTOR * s // N_EXPERTS
    logits = jnp.einsum('bsd,de->bse', x, w_router).astype(jnp.float32) + b_router.astype(jnp.float32)
    aff = jax.nn.softmax(logits, axis=-1)
    gate, idx = lax.top_k(aff.transpose(0, 2, 1), cap)
    bidx = jnp.arange(b)[:, None, None]
    xs = x[bidx, idx]
    hid = jax.nn.silu(jnp.einsum('becd,edf->becf', xs, w_gate)) * jnp.einsum('becd,edf->becf', xs, w_up)
    ys = jnp.einsum('becf,efd->becd', hid, w_down) * gate[..., None].astype(x.dtype)
    return jnp.zeros_like(x).at[bidx, idx].add(ys)


def hybrid_layer(x, pos, norm_mix, w_in, q_norm_a, k_norm_a, rpb, cq_norm, w_uq, ckv_norm, w_ukv,
                 q_norm_b, k_norm_b, conv_w, out_norm, w_out, norm_ffn, w_router, b_router,
                 w_gate, w_up, w_down):
    b, s, _ = x.shape
    hn = rms_norm(x, norm_mix)
    proj = jnp.einsum('bsd,dc->bsc', hn, w_in)
    splits = np.cumsum(IN_SIZES)[:-1].tolist()
    qa, ka, va, cq, ckv, kr, hc, bc, cc = jnp.split(proj, splits, axis=-1)

    qa = rms_norm(qa.reshape(b, s, NA_HEADS, HEAD_DIM), q_norm_a)
    ka = rms_norm(ka.reshape(b, s, NA_HEADS, HEAD_DIM), k_norm_a)
    va = va.reshape(b, s, NA_HEADS, HEAD_DIM)
    out_a = neighbourhood_attention(qa, ka, va, rpb)

    qb = jnp.einsum('bsr,rc->bsc', rms_norm(cq, cq_norm), w_uq).reshape(b, s, MLA_HEADS, MLA_QK)
    kv = jnp.einsum('bsr,rc->bsc', rms_norm(ckv, ckv_norm), w_ukv).reshape(b, s, MLA_HEADS, MLA_NOPE + MLA_V)
    kb = jnp.concatenate(
        [kv[..., :MLA_NOPE], jnp.broadcast_to(kr[:, :, None, :], (b, s, MLA_HEADS, MLA_ROPE))], axis=-1)
    vb = kv[..., MLA_NOPE:]
    qb = rms_norm(qb, q_norm_b)
    kb = rms_norm(kb, k_norm_b)
    qb = jnp.concatenate([qb[..., :MLA_NOPE], rope(qb[..., MLA_NOPE:], pos)], axis=-1)
    kb = jnp.concatenate([kb[..., :MLA_NOPE], rope(kb[..., MLA_NOPE:], pos)], axis=-1)
    out_b = mla_attention(qb, kb, vb)

    out_c = short_conv_mixer(hc, bc, cc, conv_w)

    mixed = jnp.concatenate([
        rms_norm(out_a, out_norm[:A_WIDTH]),
        rms_norm(out_b, out_norm[A_WIDTH:A_WIDTH + B_WIDTH]),
        rms_norm(out_c, out_norm[A_WIDTH + B_WIDTH:]),
    ], axis=-1)
    x = x + jnp.einsum('bsc,cd->bsd', mixed, w_out)
    x = x + expert_choice_moe(rms_norm(x, norm_ffn), w_router, b_router, w_gate, w_up, w_down)
    return x


def setup_inputs(seed: int = 0) -> dict:
    key = jax.random.key(seed)
    ks = jax.random.split(key, 22)
    f32 = jnp.float32

    def nrm(k, shape, scale):
        return jax.random.normal(k, shape, f32) * scale

    def gain(k, shape):
        return 1.0 + 0.05 * jax.random.normal(k, shape, f32)

    res_scale = (2 * DEPTH) ** -0.5
    return {
        'x': nrm(ks[0], (BATCH, SEQ, D_MODEL), 1.0),
        'norm_mix': gain(ks[1], (DEPTH, D_MODEL)),
        'w_in': nrm(ks[2], (DEPTH, D_MODEL, IN_COLS), D_MODEL ** -0.5),
        'q_norm_a': gain(ks[3], (DEPTH, HEAD_DIM)),
        'k_norm_a': gain(ks[4], (DEPTH, HEAD_DIM)),
        'rpb': nrm(ks[5], (DEPTH, NA_HEADS, 2 * NA_WIN_H - 1, 2 * NA_WIN_W - 1), 0.1),
        'cq_norm': gain(ks[6], (DEPTH, MLA_Q_RANK)),
        'w_uq': nrm(ks[7], (DEPTH, MLA_Q_RANK, MLA_HEADS * MLA_QK), MLA_Q_RANK ** -0.5),
        'ckv_norm': gain(ks[8], (DEPTH, MLA_KV_RANK)),
        'w_ukv': nrm(ks[9], (DEPTH, MLA_KV_RANK, MLA_HEADS * (MLA_NOPE + MLA_V)), MLA_KV_RANK ** -0.5),
        'q_norm_b': gain(ks[10], (DEPTH, MLA_QK)),
        'k_norm_b': gain(ks[11], (DEPTH, MLA_QK)),
        'conv_w': nrm(ks[12], (DEPTH, CONV_W, CONV_CH), CONV_W ** -0.5),
        'out_norm': gain(ks[13], (DEPTH, D_MIX)),
        'w_out': nrm(ks[14], (DEPTH, D_MIX, D_MODEL), D_MIX ** -0.5 * res_scale),
        'norm_ffn': gain(ks[15], (DEPTH, D_MODEL)),
        'w_router': nrm(ks[16], (DEPTH, D_MODEL, N_EXPERTS), D_MODEL ** -0.5),
        'b_router': nrm(ks[17], (DEPTH, N_EXPERTS), 0.01),
        'w_gate': nrm(ks[18], (DEPTH, N_EXPERTS, D_MODEL, EXPERT_FF), D_MODEL ** -0.5),
        'w_up': nrm(ks[19], (DEPTH, N_EXPERTS, D_MODEL, EXPERT_FF), D_MODEL ** -0.5),
        'w_down': nrm(ks[20], (DEPTH, N_EXPERTS, EXPERT_FF, D_MODEL), EXPERT_FF ** -0.5),
    }


def reference(x, norm_mix, w_in, q_norm_a, k_norm_a, rpb, cq_norm, w_uq, ckv_norm, w_ukv,
              q_norm_b, k_norm_b, conv_w, out_norm, w_out, norm_ffn, w_router, b_router,
              w_gate, w_up, w_down):
    pos = jnp.arange(x.shape[1], dtype=jnp.int32)
    for l in range(DEPTH):
        x = hybrid_layer(x, pos, norm_mix[l], w_in[l], q_norm_a[l], k_norm_a[l], rpb[l], cq_norm[l],
                         w_uq[l], ckv_norm[l], w_ukv[l], q_norm_b[l], k_norm_b[l], conv_w[l],
                         out_norm[l], w_out[l], norm_ffn[l], w_router[l], b_router[l],
                         w_gate[l], w_up[l], w_down[l])
    return x
```

```python
import functools

import numpy as np
import jax
import jax.numpy as jnp
from jax import lax
from jax.experimental import pallas as pl
from jax.experimental.pallas import tpu as pltpu

D_MODEL = 1024
GRID_W = 64
HEAD_DIM = 64
EPS = 1e-6

NA_HEADS = 6
NA_WIN_H = 8
NA_WIN_W = 16
A_WIDTH = NA_HEADS * HEAD_DIM

MLA_HEADS = 6
MLA_Q_RANK = 256
MLA_KV_RANK = 128
MLA_NOPE = 64
MLA_ROPE = 32
MLA_QK = MLA_NOPE + MLA_ROPE
MLA_V = 64
B_WIDTH = MLA_HEADS * MLA_V
ROPE_THETA = 10000.0

CONV_CH = 256
C_WIDTH = CONV_CH

N_EXPERTS = 16
EXPERT_FF = 1024
CAPACITY_FACTOR = 2

LANES = 128
HEAD_PAD = 128
QB_WIDTH = MLA_HEADS * HEAD_PAD
IN_COLS_PAD = 2432
TOK_TILE = 512
NA_ROWS = 8
MOE_CHUNK = 256
MOE_COLS = 256
TOPK_ROWS = 128
NEG_BIG = -1e30
VMEM_LIMIT = 56 * 1024 * 1024

F32 = jnp.float32
BF16 = jnp.bfloat16


def _dot(a, b):
    return jnp.dot(a, b, preferred_element_type=F32)


def _dot_nt(a, b):
    return lax.dot_general(a, b, (((1,), (1,)), ((), ())), preferred_element_type=F32)


def _rms(x, g):
    return x * lax.rsqrt(jnp.mean(x * x, axis=-1, keepdims=True) + EPS) * g


def _group_sumsq(y, e_ref):
    sq = y * y
    hi = sq.astype(BF16)
    lo = (sq - hi.astype(F32)).astype(BF16)
    e = e_ref[...]
    return _dot(hi, e) + _dot(lo, e)


def _moe_tile_to_rows(moe_ref, n_chunks):
    rows = []
    for ch in range(n_chunks):
        cols = [moe_ref[0, cq, ch].T for cq in range(D_MODEL // MOE_COLS)]
        rows.append(jnp.concatenate(cols, axis=1))
    return jnp.concatenate(rows, axis=0)


def _proj_kernel(*refs, has_moe):
    if has_moe:
        x_ref, moe_ref = refs[0], refs[1]
        refs = refs[2:]
    else:
        x_ref = refs[0]
        refs = refs[1:]
    (nmix_ref, win_ref, gqa_ref, gka_ref, e384_ref, cqn_ref, wuq_ref, ckvn_ref, wkk_ref,
     wuv_ref, gqb_ref, gkb_ref, e768_ref, cos_ref, sin_ref) = refs[:15]
    outs = refs[15:]
    if has_moe:
        qa_o, ka_o, va_o, qb_o, kb_o, vbt_o, u_o, bc_o, xs_o = outs
    else:
        qa_o, ka_o, va_o, qb_o, kb_o, vbt_o, u_o, bc_o = outs

    x = x_ref[0]
    if has_moe:
        x = x + _moe_tile_to_rows(moe_ref, TOK_TILE // MOE_CHUNK)
        xs_o[0] = x
    xb = _rms(x, nmix_ref[...]).astype(BF16)

    pa = _dot(xb, win_ref[:, 0:3 * A_WIDTH])
    qa = pa[:, 0:A_WIDTH]
    ka = pa[:, A_WIDTH:2 * A_WIDTH]
    qa_o[0] = (qa * lax.rsqrt(_group_sumsq(qa, e384_ref) * (1.0 / HEAD_DIM) + EPS) * gqa_ref[...]).astype(BF16)
    ka_o[0] = (ka * lax.rsqrt(_group_sumsq(ka, e384_ref) * (1.0 / HEAD_DIM) + EPS) * gka_ref[...]).astype(BF16)
    va_o[0] = pa[:, 2 * A_WIDTH:3 * A_WIDTH].astype(BF16)

    c0 = 3 * A_WIDTH
    pb = _dot(xb, win_ref[:, c0:c0 + 512])
    cq = _rms(pb[:, 0:MLA_Q_RANK], cqn_ref[...]).astype(BF16)
    ckv = _rms(pb[:, MLA_Q_RANK:MLA_Q_RANK + MLA_KV_RANK], ckvn_ref[...]).astype(BF16)
    kr = pb[:, MLA_Q_RANK + MLA_KV_RANK:512].astype(BF16)
    qb_raw = _dot(cq, wuq_ref[...])
    kb_raw = _dot(jnp.concatenate([ckv, kr], axis=1), wkk_ref[...])
    vb = _dot(ckv, wuv_ref[...])

    cos = jnp.concatenate([cos_ref[...]] * MLA_HEADS, axis=1)
    sin = jnp.concatenate([sin_ref[...]] * MLA_HEADS, axis=1)
    lane = lax.broadcasted_iota(jnp.int32, (TOK_TILE, QB_WIDTH), 1) & (HEAD_PAD - 1)
    first_half = (lane >= MLA_NOPE) & (lane < MLA_NOPE + MLA_ROPE // 2)

    def norm_rope(y, g_ref):
        y = y * lax.rsqrt(_group_sumsq(y, e768_ref) * (1.0 / MLA_QK) + EPS) * g_ref[...]
        rot = jnp.where(first_half,
                        pltpu.roll(y, QB_WIDTH - MLA_ROPE // 2, 1),
                        pltpu.roll(y, MLA_ROPE // 2, 1))
        return (y * cos + rot * sin).astype(BF16)

    qb_o[0] = norm_rope(qb_raw, gqb_ref)
    kb_o[0] = norm_rope(kb_raw, gkb_ref)
    vbt = vb.T
    for h in range(MLA_HEADS):
        vbt_o[0, h, 0] = vbt[h * MLA_V:(h + 1) * MLA_V, :].astype(BF16)

    c1 = c0 + 512
    pc = _dot(xb, win_ref[:, c1:c1 + 3 * CONV_CH])
    u_o[0] = pc[:, 2 * CONV_CH:3 * CONV_CH] * pc[:, 0:CONV_CH]
    bc_o[0] = pc[:, CONV_CH:2 * CONV_CH]


def _proj_call(x, moe, lw, tabs):
    b, s, d = x.shape
    nt = s // TOK_TILE
    has_moe = moe is not None

    def const(shape):
        return pl.BlockSpec(shape, lambda bi, ti: (0,) * len(shape))

    in_specs = [pl.BlockSpec((1, TOK_TILE, d), lambda bi, ti: (bi, ti, 0))]
    args = [x]
    if has_moe:
        ncq = D_MODEL // MOE_COLS
        in_specs.append(pl.BlockSpec((1, ncq, TOK_TILE // MOE_CHUNK, MOE_COLS, MOE_CHUNK),
                                     lambda bi, ti: (bi, 0, ti, 0, 0)))
        args.append(moe)
    consts = [lw['nmix'], lw['w_in'], lw['gqa'], lw['gka'], tabs['e384'], lw['cqn'], lw['w_uq'],
              lw['ckvn'], lw['w_kk'], lw['w_uv'], lw['gqb'], lw['gkb'], tabs['e768']]
    for c in consts:
        in_specs.append(const(c.shape))
        args.append(c)
    in_specs += [pl.BlockSpec((TOK_TILE, HEAD_PAD), lambda bi, ti: (ti, 0))] * 2
    args += [tabs['cos'], tabs['sin']]

    def tok(width, dtype):
        return (jax.ShapeDtypeStruct((b, s, width), dtype),
                pl.BlockSpec((1, TOK_TILE, width), lambda bi, ti: (bi, ti, 0)))

    outs = [tok(A_WIDTH, BF16), tok(A_WIDTH, BF16), tok(A_WIDTH, BF16),
            tok(QB_WIDTH, BF16), tok(QB_WIDTH, BF16),
            (jax.ShapeDtypeStruct((b, MLA_HEADS, nt, MLA_V, TOK_TILE), BF16),
             pl.BlockSpec((1, MLA_HEADS, 1, MLA_V, TOK_TILE), lambda bi, ti: (bi, 0, ti, 0, 0))),
            tok(CONV_CH, F32), tok(CONV_CH, F32)]
    if has_moe:
        outs.append(tok(d, F32))
    return pl.pallas_call(
        functools.partial(_proj_kernel, has_moe=has_moe),
        grid=(b, nt),
        in_specs=in_specs,
        out_specs=[o[1] for o in outs],
        out_shape=[o[0] for o in outs],
        compiler_params=pltpu.CompilerParams(
            dimension_semantics=("arbitrary", "arbitrary"), vmem_limit_bytes=VMEM_LIMIT),
        name="proj_moe" if has_moe else "proj",
    )(*args)


def _bias_kernel(rpb_ref, o_ref):
    h = pl.program_id(0)
    delta = pl.program_id(1)
    nw = 2 * NA_WIN_W - 1
    shape = (GRID_W, NA_WIN_H * GRID_W)
    c = lax.broadcasted_iota(jnp.int32, shape, 0)
    col = lax.broadcasted_iota(jnp.int32, shape, 1)
    wrow = lax.shift_right_logical(col, 6)
    kc = col & (GRID_W - 1)
    c_lo = jnp.clip(c - NA_WIN_W // 2, 0, GRID_W - NA_WIN_W)
    inside = (kc >= c_lo) & (kc < c_lo + NA_WIN_W)
    d = kc - c + (NA_WIN_W - 1)
    acc = jnp.zeros(shape, F32)
    base = h * ((2 * NA_WIN_H - 1) * nw)
    for i in range(NA_WIN_H):
        ri = i - delta + (NA_WIN_H - 1)
        for dd in range(nw):
            val = rpb_ref[base + ri * nw + dd]
            acc = jnp.where((wrow == i) & (d == dd), val, acc)
    o_ref[0, 0] = jnp.where(inside, acc, NEG_BIG)


def _bias_call(rpb_l):
    flat = rpb_l.reshape(-1)
    return pl.pallas_call(
        _bias_kernel,
        grid=(NA_HEADS, NA_WIN_H),
        in_specs=[pl.BlockSpec(memory_space=pltpu.SMEM)],
        out_specs=pl.BlockSpec((1, 1, GRID_W, NA_WIN_H * GRID_W), lambda h, dl: (h, dl, 0, 0)),
        out_shape=jax.ShapeDtypeStruct((NA_HEADS, NA_WIN_H, GRID_W, NA_WIN_H * GRID_W), F32),
        name="na_bias",
    )(flat)


def _na_kernel(q_ref, kp_ref, kc_ref, kn_ref, vp_ref, vc_ref, vn_ref, bias_ref, o_ref,
               k_scr, v_scr, *, rows):
    j = pl.program_id(1)
    blk = NA_ROWS * GRID_W
    k_scr[0:blk] = kp_ref[0]
    k_scr[blk:2 * blk] = kc_ref[0]
    k_scr[2 * blk:3 * blk] = kn_ref[0]
    v_scr[0:blk] = vp_ref[0]
    v_scr[blk:2 * blk] = vc_ref[0]
    v_scr[2 * blk:3 * blk] = vn_ref[0]
    lane = lax.broadcasted_iota(jnp.int32, (GRID_W, LANES), 1)
    low = lane < HEAD_DIM
    for i in range(NA_ROWS):
        r = NA_ROWS * j + i
        r0 = jnp.clip(r - NA_WIN_H // 2, 0, rows - NA_WIN_H)
        delta = r - r0
        start = pl.multiple_of((r0 - NA_ROWS * j + NA_ROWS) * GRID_W, GRID_W)
        for p in range(NA_HEADS // 2):
            lsl = slice(p * LANES, (p + 1) * LANES)
            qp = q_ref[0, i * GRID_W:(i + 1) * GRID_W, lsl]
            kw = k_scr[pl.ds(start, NA_WIN_H * GRID_W), lsl]
            vw = v_scr[pl.ds(start, NA_WIN_H * GRID_W), lsl]
            res = []
            for a in range(2):
                qm = jnp.where(low if a == 0 else jnp.logical_not(low), qp, jnp.zeros_like(qp))
                sc = _dot_nt(qm, kw) + bias_ref[2 * p + a, delta]
                m = jnp.max(sc, axis=-1, keepdims=True)
                e = jnp.exp(sc - m)
                l = jnp.sum(e, axis=-1, keepdims=True)
                res.append(_dot(e.astype(BF16), vw) / l)
            o_ref[0, i * GRID_W:(i + 1) * GRID_W, lsl] = jnp.where(low, res[0], res[1])


def _na_call(qa, ka, va, bias_tab):
    b, s, w = qa.shape
    rows = s // GRID_W
    nb = rows // NA_ROWS
    blk = NA_ROWS * GRID_W
    cur = pl.BlockSpec((1, blk, w), lambda bi, j: (bi, j, 0))
    prev = pl.BlockSpec((1, blk, w), lambda bi, j: (bi, jnp.maximum(j - 1, 0), 0))
    nxt = pl.BlockSpec((1, blk, w), lambda bi, j: (bi, jnp.minimum(j + 1, nb - 1), 0))
    return pl.pallas_call(
        functools.partial(_na_kernel, rows=rows),
        grid=(b, nb),
        in_specs=[cur, prev, cur, nxt, prev, cur, nxt,
                  pl.BlockSpec(bias_tab.shape, lambda bi, j: (0, 0, 0, 0))],
        out_specs=pl.BlockSpec((1, blk, w), lambda bi, j: (bi, j, 0)),
        out_shape=jax.ShapeDtypeStruct((b, s, w), F32),
        scratch_shapes=[pltpu.VMEM((3 * blk, w), BF16), pltpu.VMEM((3 * blk, w), BF16)],
        compiler_params=pltpu.CompilerParams(
            dimension_semantics=("arbitrary", "arbitrary"), vmem_limit_bytes=VMEM_LIMIT),
        name="na_attn",
    )(qa, ka, ka, ka, va, va, va, bias_tab)


def _mla_kernel(q_ref, k_ref, vt_ref, o_ref, *, n_chunks):
    q = q_ref[0]
    tq = q.shape[0]

    def body(c, carry):
        m, l, acc = carry
        kb = k_ref[0, pl.ds(pl.multiple_of(c * TOK_TILE, TOK_TILE), TOK_TILE), :]
        st = _dot_nt(kb, q)
        m_new = jnp.maximum(m, jnp.max(st, axis=0, keepdims=True))
        alpha = jnp.exp(m - m_new)
        p = jnp.exp(st - m_new)
        l = alpha * l + jnp.sum(p, axis=0, keepdims=True)
        acc = alpha * acc + _dot(vt_ref[0, 0, c], p.astype(BF16))
        return m_new, l, acc

    init = (jnp.full((1, tq), NEG_BIG, F32), jnp.zeros((1, tq), F32), jnp.zeros((MLA_V, tq), F32))
    _, l, acc = lax.fori_loop(0, n_chunks, body, init)
    o_ref[0] = acc / l


def _mla_call(qb, kb, vbt):
    b, s, _ = qb.shape
    nt = s // TOK_TILE
    return pl.pallas_call(
        functools.partial(_mla_kernel, n_chunks=nt),
        grid=(b, MLA_HEADS, nt),
        in_specs=[pl.BlockSpec((1, TOK_TILE, HEAD_PAD), lambda bi, h, qi: (bi, qi, h)),
                  pl.BlockSpec((1, s, HEAD_PAD), lambda bi, h, qi: (bi, 0, h)),
                  pl.BlockSpec((1, 1, nt, MLA_V, TOK_TILE), lambda bi, h, qi: (bi, h, 0, 0, 0))],
        out_specs=pl.BlockSpec((1, MLA_V, TOK_TILE), lambda bi, h, qi: (bi, h, qi)),
        out_shape=jax.ShapeDtypeStruct((b, B_WIDTH, s), F32),
        compiler_params=pltpu.CompilerParams(
            dimension_semantics=("arbitrary", "arbitrary", "arbitrary"), vmem_limit_bytes=VMEM_LIMIT),
        name="mla_attn",
    )(qb, kb, vbt)


def _out_kernel(x_ref, oa_ref, obt_ref, u_ref, up_ref, un_ref, bc_ref, ga_ref, gb_ref, gc_ref,
                woa_ref, wob_ref, woc_ref, cw_ref, nffn_ref, wrh_ref, wrl_ref, br_ref,
                x1_o, xn_o, aff_o):
    ti = pl.program_id(1)
    nt = pl.num_programs(1)
    t = TOK_TILE

    na = _rms(oa_ref[0], ga_ref[...]).astype(BF16)
    acc = _dot(na, woa_ref[...])

    obt = obt_ref[0]
    nbt = obt * lax.rsqrt(jnp.mean(obt * obt, axis=0, keepdims=True) + EPS) * gb_ref[...]
    acc = acc + _dot(nbt.T.astype(BF16), wob_ref[...])

    u = u_ref[0]
    row = lax.broadcasted_iota(jnp.int32, (t, CONV_CH), 0)
    prev_row = jnp.where(ti > 0, up_ref[0, 7:8, :], 0.0)
    next_row = jnp.where(ti < nt - 1, un_ref[0, 0:1, :], 0.0)
    u_m1 = jnp.where(row == 0, prev_row, pltpu.roll(u, 1, 0))
    u_p1 = jnp.where(row == t - 1, next_row, pltpu.roll(u, t - 1, 0))
    y = cw_ref[0:1, :] * u_m1 + cw_ref[1:2, :] * u + cw_ref[2:3, :] * u_p1
    oc = _rms(bc_ref[0] * y, gc_ref[...]).astype(BF16)
    acc = acc + _dot(oc, woc_ref[...])

    x1 = x_ref[0] + acc
    x1_o[0] = x1
    xn = _rms(x1, nffn_ref[...])
    hi = xn.astype(BF16)
    lo = (xn - hi.astype(F32)).astype(BF16)
    xn_o[0] = hi
    logits = _dot(hi, wrh_ref[...]) + _dot(lo, wrh_ref[...]) + _dot(hi, wrl_ref[...]) + br_ref[...]
    m = jnp.max(logits, axis=-1, keepdims=True)
    e = jnp.exp(logits - m)
    aff = e / jnp.sum(e, axis=-1, keepdims=True)
    aff_o[0] = aff.T[0:N_EXPERTS, :]


def _out_call(x, out_a, out_bt, u, bc, lw):
    b, s, d = x.shape
    nt = s // TOK_TILE
    h8 = TOK_TILE // 8

    def const(a):
        return pl.BlockSpec(a.shape, lambda bi, ti: (0,) * a.ndim)

    def tok(width):
        return pl.BlockSpec((1, TOK_TILE, width), lambda bi, ti: (bi, ti, 0))

    consts = [lw['ga'], lw['gb_col'], lw['gc'], lw['wo_a'], lw['wo_b'], lw['wo_c'], lw['conv_w'],
              lw['nffn'], lw['wr_hi'], lw['wr_lo'], lw['b_r']]
    in_specs = [tok(d), tok(A_WIDTH),
                pl.BlockSpec((1, B_WIDTH, TOK_TILE), lambda bi, ti: (bi, 0, ti)),
                tok(CONV_CH),
                pl.BlockSpec((1, 8, CONV_CH), lambda bi, ti: (bi, jnp.maximum(ti * h8 - 1, 0), 0)),
                pl.BlockSpec((1, 8, CONV_CH), lambda bi, ti: (bi, jnp.minimum((ti + 1) * h8, s // 8 - 1), 0)),
                tok(CONV_CH)] + [const(c) for c in consts]
    return pl.pallas_call(
        _out_kernel,
        grid=(b, nt),
        in_specs=in_specs,
        out_specs=[tok(d), tok(d), pl.BlockSpec((1, N_EXPERTS, TOK_TILE), lambda bi, ti: (bi, 0, ti))],
        out_shape=[jax.ShapeDtypeStruct((b, s, d), F32), jax.ShapeDtypeStruct((b, s, d), BF16),
                   jax.ShapeDtypeStruct((b, N_EXPERTS, s), F32)],
        compiler_params=pltpu.CompilerParams(
            dimension_semantics=("arbitrary", "arbitrary"), vmem_limit_bytes=VMEM_LIMIT),
        name="out_proj",
    )(x, out_a, out_bt, u, u, u, bc, *consts)


def _topk_kernel(a_ref, tri_ref, o_slot, o_gate, o_off, *, cap):
    a = a_ref[0, 0]
    bits = pltpu.bitcast(a, jnp.int32)

    def total(mask):
        v = jnp.where(mask, 1.0, 0.0)
        return jnp.sum(jnp.sum(v, axis=0, keepdims=True), axis=1, keepdims=True)

    thr = jnp.zeros((1, 1), jnp.int32)
    for bit in range(30, -1, -1):
        cand = thr | (1 << bit)
        thr = jnp.where(total(bits >= cand) >= cap, cand, thr)
    gt = bits > thr
    eq = bits == thr
    need = cap - total(gt)

    upper = tri_ref[0]
    lower_strict = tri_ref[1]
    upper_strict = tri_ref[2]

    def excl_rank(mask):
        f = jnp.where(mask, 1.0, 0.0)
        fb = f.astype(BF16)
        in_row = _dot(fb, upper)
        before_rows = jnp.sum(_dot(lower_strict, fb), axis=1, keepdims=True)
        return before_rows + in_row - f, fb

    eq_rank, _ = excl_rank(eq)
    sel = gt | (eq & (eq_rank < need))
    pos, sel_b = excl_rank(sel)
    o_slot[0, 0] = jnp.where(sel, pos, -1.0).astype(jnp.int32)
    o_gate[0, 0] = jnp.where(sel, a, 0.0)
    row_tot = _dot_nt(jnp.ones((8, LANES), BF16), sel_b)
    o_off[0, 0] = _dot(row_tot.astype(BF16), upper_strict)


def _topk_call(aff_t, tri, cap):
    b, e, s = aff_t.shape
    pad = TOPK_ROWS * LANES - s
    a = aff_t if pad == 0 else jnp.pad(aff_t, ((0, 0), (0, 0), (0, pad)), constant_values=-1.0)
    a = a.reshape(b, e, TOPK_ROWS, LANES)
    blk = pl.BlockSpec((1, 1, TOPK_ROWS, LANES), lambda bi, ei: (bi, ei, 0, 0))
    return pl.pallas_call(
        functools.partial(_topk_kernel, cap=cap),
        grid=(b, e),
        in_specs=[blk, pl.BlockSpec(tri.shape, lambda bi, ei: (0, 0, 0))],
        out_specs=[blk, blk, pl.BlockSpec((1, 1, 8, LANES), lambda bi, ei: (bi, ei, 0, 0))],
        out_shape=[jax.ShapeDtypeStruct((b, e, TOPK_ROWS, LANES), jnp.int32),
                   jax.ShapeDtypeStruct((b, e, TOPK_ROWS, LANES), F32),
                   jax.ShapeDtypeStruct((b, e, 8, LANES), F32)],
        name="topk_select",
    )(a, tri)


def _chunk_tiles(coff_ref, base, k):
    off0 = coff_ref[base + k]
    off1 = coff_ref[base + k + 1]
    r0 = lax.shift_right_logical(off0, 8)
    r1 = lax.shift_right_logical(jnp.maximum(off1 - 1, 0), 8)
    return off1 > off0, r0, r1


def _onehot(slot_row, r):
    i = lax.broadcasted_iota(jnp.int32, (MOE_CHUNK, MOE_CHUNK), 0)
    return jnp.where(i + r * MOE_CHUNK == slot_row, 1.0, 0.0).astype(BF16)


def _gather_kernel(coff_ref, x_ref, slot_ref, o_ref, acc_ref, *, n_chunks):
    bi = pl.program_id(0)
    ei = pl.program_id(2)
    base = (bi * N_EXPERTS + ei) * (n_chunks + 1)
    acc_ref[...] = jnp.zeros_like(acc_ref)

    def body(k, carry):
        some, r0, r1 = _chunk_tiles(coff_ref, base, k)

        def add(r):
            g = _onehot(slot_ref[0, 0, k], r)
            xk = x_ref[0, pl.ds(pl.multiple_of(k * MOE_CHUNK, MOE_CHUNK), MOE_CHUNK), :]
            rows = pl.ds(pl.multiple_of(r * MOE_CHUNK, MOE_CHUNK), MOE_CHUNK)
            acc_ref[rows, :] += _dot(g, xk)

        @pl.when(some)
        def _():
            add(r0)

        @pl.when(some & (r1 > r0))
        def _():
            add(r0 + 1)

        return carry

    lax.fori_loop(0, n_chunks, body, 0)
    o_ref[0, 0] = acc_ref[...].astype(BF16)


def _gather_call(coff, xn, slot5, cap):
    b, s, d = xn.shape
    n_chunks = s // MOE_CHUNK
    ncq = d // MOE_COLS
    grid_spec = pltpu.PrefetchScalarGridSpec(
        num_scalar_prefetch=1,
        grid=(b, ncq, N_EXPERTS),
        in_specs=[pl.BlockSpec((1, s, MOE_COLS), lambda bi, cq, ei, co: (bi, 0, cq)),
                  pl.BlockSpec((1, 1, n_chunks, 1, MOE_CHUNK), lambda bi, cq, ei, co: (bi, ei, 0, 0, 0))],
        out_specs=pl.BlockSpec((1, 1, cap, MOE_COLS), lambda bi, cq, ei, co: (bi, ei, 0, cq)),
        scratch_shapes=[pltpu.VMEM((cap, MOE_COLS), F32)],
    )
    return pl.pallas_call(
        functools.partial(_gather_kernel, n_chunks=n_chunks),
        grid_spec=grid_spec,
        out_shape=jax.ShapeDtypeStruct((b, N_EXPERTS, cap, d), BF16),
        compiler_params=pltpu.CompilerParams(
            dimension_semantics=("arbitrary", "arbitrary", "arbitrary"), vmem_limit_bytes=VMEM_LIMIT),
        name="moe_gather",
    )(coff, xn, slot5)


def _ffn_kernel(xc_ref, wg_ref, wu_ref, wdt_ref, o_ref, *, tiles):
    xc = xc_ref[0, 0]
    g = _dot(xc, wg_ref[0])
    u = _dot(xc, wu_ref[0])
    h = (g / (1.0 + jnp.exp(-g)) * u).astype(BF16)
    yt = _dot_nt(wdt_ref[0], h)
    for r in range(tiles):
        o_ref[0, 0, r] = yt[:, r * MOE_CHUNK:(r + 1) * MOE_CHUNK].astype(BF16)


def _ffn_call(xc, wg, wu, wdt):
    b, e, cap, d = xc.shape
    ff = wg.shape[-1]
    rt = min(cap, 1024)
    tiles = rt // MOE_CHUNK
    return pl.pallas_call(
        functools.partial(_ffn_kernel, tiles=tiles),
        grid=(e, b, cap // rt),
        in_specs=[pl.BlockSpec((1, 1, rt, d), lambda ei, bi, ri: (bi, ei, ri, 0)),
                  pl.BlockSpec((1, d, ff), lambda ei, bi, ri: (ei, 0, 0)),
                  pl.BlockSpec((1, d, ff), lambda ei, bi, ri: (ei, 0, 0)),
                  pl.BlockSpec((1, d, ff), lambda ei, bi, ri: (ei, 0, 0))],
        out_specs=pl.BlockSpec((1, 1, tiles, d, MOE_CHUNK), lambda ei, bi, ri: (bi, ei, ri, 0, 0)),
        out_shape=jax.ShapeDtypeStruct((b, e, cap // MOE_CHUNK, d, MOE_CHUNK), BF16),
        compiler_params=pltpu.CompilerParams(
            dimension_semantics=("arbitrary", "arbitrary", "arbitrary"), vmem_limit_bytes=VMEM_LIMIT),
        name="moe_ffn",
    )(xc, wg, wu, wdt)


def _scatter_kernel(coff_ref, y_ref, slot_ref, gate_ref, o_ref, *, n_chunks):
    bi = pl.program_id(0)
    ei = pl.program_id(2)
    base = (bi * N_EXPERTS + ei) * (n_chunks + 1)

    @pl.when(ei == 0)
    def _():
        o_ref[...] = jnp.zeros_like(o_ref)

    def body(k, carry):
        some, r0, r1 = _chunk_tiles(coff_ref, base, k)

        def add(r):
            g = _onehot(slot_ref[0, 0, k], r)
            o_ref[0, 0, k] += _dot(y_ref[0, 0, r], g) * gate_ref[0, 0, k]

        @pl.when(some)
        def _():
            add(r0)

        @pl.when(some & (r1 > r0))
        def _():
            add(r0 + 1)

        return carry

    lax.fori_loop(0, n_chunks, body, 0)


def _scatter_call(coff, y5, slot5, gate5):
    b, e, tiles, d, _ = y5.shape
    n_chunks = slot5.shape[2]
    ncq = d // MOE_COLS
    small = pl.BlockSpec((1, 1, n_chunks, 1, MOE_CHUNK), lambda bi, cq, ei, co: (bi, ei, 0, 0, 0))
    grid_spec = pltpu.PrefetchScalarGridSpec(
        num_scalar_prefetch=1,
        grid=(b, ncq, N_EXPERTS),
        in_specs=[pl.BlockSpec((1, 1, tiles, MOE_COLS, MOE_CHUNK), lambda bi, cq, ei, co: (bi, ei, 0, cq, 0)),
                  small, small],
        out_specs=pl.BlockSpec((1, 1, n_chunks, MOE_COLS, MOE_CHUNK), lambda bi, cq, ei, co: (bi, cq, 0, 0, 0)),
    )
    return pl.pallas_call(
        functools.partial(_scatter_kernel, n_chunks=n_chunks),
        grid_spec=grid_spec,
        out_shape=jax.ShapeDtypeStruct((b, ncq, n_chunks, MOE_COLS, MOE_CHUNK), F32),
        compiler_params=pltpu.CompilerParams(
            dimension_semantics=("arbitrary", "arbitrary", "arbitrary"), vmem_limit_bytes=VMEM_LIMIT),
        name="moe_scatter",
    )(coff, y5, slot5, gate5)


def _final_kernel(x_ref, moe_ref, o_ref):
    o_ref[0] = x_ref[0] + _moe_tile_to_rows(moe_ref, TOK_TILE // MOE_CHUNK)


def _final_call(x1, moe):
    b, s, d = x1.shape
    ncq = d // MOE_COLS
    tok = pl.BlockSpec((1, TOK_TILE, d), lambda bi, ti: (bi, ti, 0))
    return pl.pallas_call(
        _final_kernel,
        grid=(b, s // TOK_TILE),
        in_specs=[tok, pl.BlockSpec((1, ncq, TOK_TILE // MOE_CHUNK, MOE_COLS, MOE_CHUNK),
                                    lambda bi, ti: (bi, 0, ti, 0, 0))],
        out_specs=tok,
        out_shape=jax.ShapeDtypeStruct((b, s, d), F32),
        compiler_params=pltpu.CompilerParams(
            dimension_semantics=("arbitrary", "arbitrary"), vmem_limit_bytes=VMEM_LIMIT),
        name="final_add",
    )(x1, moe)


def _block_ones(n, group):
    i = np.arange(n)
    return jnp.asarray((i[:, None] // group) == (i[None, :] // group), BF16)


def _tables(s):
    half = MLA_ROPE // 2
    inv = ROPE_THETA ** (-jnp.arange(half, dtype=F32) / half)
    ang = jnp.arange(s, dtype=jnp.int32).astype(F32)[:, None] * inv[None, :]
    cos, sin = jnp.cos(ang), jnp.sin(ang)
    ones = jnp.ones((s, MLA_NOPE), F32)
    tail = HEAD_PAD - MLA_QK
    cos_t = jnp.concatenate([ones, cos, cos, jnp.ones((s, tail), F32)], axis=1)
    sin_t = jnp.concatenate([0 * ones, -sin, sin, jnp.zeros((s, tail), F32)], axis=1)
    i = np.arange(LANES)
    tri = np.stack([i[:, None] <= i[None, :], i[None, :] < i[:, None], i[:, None] < i[None, :]])
    return {'cos': cos_t, 'sin': sin_t, 'e384': _block_ones(A_WIDTH, HEAD_DIM),
            'e768': _block_ones(QB_WIDTH, HEAD_PAD), 'tri': jnp.asarray(tri, BF16)}


def _pad_heads(v, scale=1.0):
    g = jnp.pad(v.astype(F32) * scale, (0, HEAD_PAD - MLA_QK))
    return jnp.tile(g, MLA_HEADS)[None, :]


def _layer_weights(l, p):
    w_in = p['w_in'][l]
    kr_end = 3 * A_WIDTH + MLA_Q_RANK + MLA_KV_RANK + MLA_ROPE
    w_in_r = jnp.concatenate(
        [w_in[:, :kr_end], jnp.zeros((D_MODEL, LANES - MLA_ROPE), F32), w_in[:, kr_end:]], axis=1)
    w_uq = p['w_uq'][l].reshape(MLA_Q_RANK, MLA_HEADS, MLA_QK)
    w_uq = jnp.pad(w_uq, ((0, 0), (0, 0), (0, HEAD_PAD - MLA_QK))).reshape(MLA_Q_RANK, QB_WIDTH)
    w_ukv = p['w_ukv'][l].reshape(MLA_KV_RANK, MLA_HEADS, MLA_NOPE + MLA_V)
    w_uk = jnp.pad(w_ukv[:, :, :MLA_NOPE], ((0, 0), (0, 0), (0, HEAD_PAD - MLA_NOPE)))
    w_uk = w_uk.reshape(MLA_KV_RANK, QB_WIDTH)
    w_uv = w_ukv[:, :, MLA_NOPE:].reshape(MLA_KV_RANK, B_WIDTH)
    place = np.zeros((LANES, QB_WIDTH), np.float32)
    for h in range(MLA_HEADS):
        place[np.arange(MLA_ROPE), h * HEAD_PAD + MLA_NOPE + np.arange(MLA_ROPE)] = 1.0
    w_kk = jnp.concatenate([w_uk, jnp.asarray(place)], axis=0)
    out_norm = p['out_norm'][l]
    w_out = p['w_out'][l]
    w_r = jnp.pad(p['w_router'][l], ((0, 0), (0, LANES - N_EXPERTS)))
    wr_hi = w_r.astype(BF16)
    b_r = jnp.concatenate([p['b_router'][l].astype(F32), jnp.full((LANES - N_EXPERTS,), NEG_BIG, F32)])
    return {
        'nmix': p['norm_mix'][l][None, :], 'w_in': w_in_r.astype(BF16),
        'gqa': jnp.tile(p['q_norm_a'][l] * HEAD_DIM ** -0.5, NA_HEADS)[None, :],
        'gka': jnp.tile(p['k_norm_a'][l], NA_HEADS)[None, :],
        'cqn': p['cq_norm'][l][None, :], 'w_uq': w_uq.astype(BF16),
        'ckvn': p['ckv_norm'][l][None, :], 'w_kk': w_kk.astype(BF16), 'w_uv': w_uv.astype(BF16),
        'gqb': _pad_heads(p['q_norm_b'][l], MLA_QK ** -0.5), 'gkb': _pad_heads(p['k_norm_b'][l]),
        'ga': out_norm[None, :A_WIDTH], 'gb_col': out_norm[A_WIDTH:A_WIDTH + B_WIDTH, None],
        'gc': out_norm[None, A_WIDTH + B_WIDTH:],
        'wo_a': w_out[:A_WIDTH].astype(BF16), 'wo_b': w_out[A_WIDTH:A_WIDTH + B_WIDTH].astype(BF16),
        'wo_c': w_out[A_WIDTH + B_WIDTH:].astype(BF16),
        'conv_w': p['conv_w'][l], 'nffn': p['norm_ffn'][l][None, :],
        'wr_hi': wr_hi, 'wr_lo': (w_r - wr_hi.astype(F32)).astype(BF16), 'b_r': b_r[None, :],
        'w_gate': p['w_gate'][l].astype(BF16), 'w_up': p['w_up'][l].astype(BF16),
        'w_down_t': jnp.swapaxes(p['w_down'][l], 1, 2).astype(BF16),
    }


def _layer(x, moe, lw, tabs, rpb_l):
    b, s, d = x.shape
    cap = CAPACITY_FACTOR * s // N_EXPERTS
    n_chunks = s // MOE_CHUNK

    outs = _proj_call(x, moe, lw, tabs)
    if moe is not None:
        qa, ka, va, qb, kb, vbt, u, bc, x = outs
    else:
        qa, ka, va, qb, kb, vbt, u, bc = outs
    out_a = _na_call(qa, ka, va, _bias_call(rpb_l))
    out_bt = _mla_call(qb, kb, vbt)
    x1, xn, aff_t = _out_call(x, out_a, out_bt, u, bc, lw)

    slot, gate, off = _topk_call(aff_t, tabs['tri'], cap)
    rows_per_chunk = MOE_CHUNK // LANES
    coff = off[:, :, 0, :s // LANES:rows_per_chunk].astype(jnp.int32)
    coff = jnp.concatenate([coff, jnp.full((b, N_EXPERTS, 1), cap, jnp.int32)], axis=-1).reshape(-1)
    slot5 = slot.reshape(b, N_EXPERTS, -1)[:, :, :s].reshape(b, N_EXPERTS, n_chunks, 1, MOE_CHUNK)
    gate5 = gate.reshape(b, N_EXPERTS, -1)[:, :, :s].reshape(b, N_EXPERTS, n_chunks, 1, MOE_CHUNK)

    xc = _gather_call(coff, xn, slot5, cap)
    y5 = _ffn_call(xc, lw['w_gate'], lw['w_up'], lw['w_down_t'])
    moe_out = _scatter_call(coff, y5, slot5, gate5)
    return x1, moe_out


def kernel(x, norm_mix, w_in, q_norm_a, k_norm_a, rpb, cq_norm, w_uq, ckv_norm, w_ukv, q_norm_b,
           k_norm_b, conv_w, out_norm, w_out, norm_ffn, w_router, b_router, w_gate, w_up, w_down):
    p = dict(norm_mix=norm_mix, w_in=w_in, q_norm_a=q_norm_a, k_norm_a=k_norm_a, cq_norm=cq_norm,
             w_uq=w_uq, ckv_norm=ckv_norm, w_ukv=w_ukv, q_norm_b=q_norm_b, k_norm_b=k_norm_b,
             conv_w=conv_w, out_norm=out_norm, w_out=w_out, norm_ffn=norm_ffn, w_router=w_router,
             b_router=b_router, w_gate=w_gate, w_up=w_up, w_down=w_down)
    b, s, d = x.shape
    assert d == D_MODEL and (CAPACITY_FACTOR * s // N_EXPERTS) % MOE_CHUNK == 0
    assert s % TOK_TILE == 0 and (s // GRID_W) % NA_ROWS == 0 and s <= TOPK_ROWS * LANES
    tabs = _tables(s)
    moe = None
    for l in range(norm_mix.shape[0]):
        x, moe = _layer(x, moe, _layer_weights(l, p), tabs, rpb[l])
    return _final_call(x, moe)
```

```python
import functools

import numpy as np
import jax
import jax.numpy as jnp
from jax import lax
from jax.experimental import pallas as pl
from jax.experimental.pallas import tpu as pltpu

D_MODEL = 1024
GRID_W = 64
HEAD_DIM = 64
EPS = 1e-6

NA_HEADS = 6
NA_WIN_H = 8
NA_WIN_W = 16
A_WIDTH = NA_HEADS * HEAD_DIM

MLA_HEADS = 6
MLA_Q_RANK = 256
MLA_KV_RANK = 128
MLA_NOPE = 64
MLA_ROPE = 32
MLA_QK = MLA_NOPE + MLA_ROPE
MLA_V = 64
B_WIDTH = MLA_HEADS * MLA_V
ROPE_THETA = 10000.0

CONV_CH = 256
C_WIDTH = CONV_CH

N_EXPERTS = 16
EXPERT_FF = 1024
CAPACITY_FACTOR = 2

LANES = 128
HEAD_PAD = 128
QB_WIDTH = MLA_HEADS * HEAD_PAD
IN_COLS_PAD = 2432
TOK_TILE = 512
NA_ROWS = 8
MOE_CHUNK = 256
MOE_COLS = 256
TOPK_ROWS = 128
MLA_VROWS = MLA_V + 16
LOG2_E = 1.4426950408889634
MLA_UNROLL = 4
NEG_BIG = -1e30
VMEM_LIMIT = 56 * 1024 * 1024

F32 = jnp.float32
BF16 = jnp.bfloat16


def _dot(a, b):
    return jnp.dot(a, b, preferred_element_type=F32)


def _dot_nt(a, b):
    return lax.dot_general(a, b, (((1,), (1,)), ((), ())), preferred_element_type=F32)


def _rms(x, g):
    return x * lax.rsqrt(jnp.mean(x * x, axis=-1, keepdims=True) + EPS) * g


def _group_sumsq(y, e_ref):
    sq = y * y
    hi = sq.astype(BF16)
    lo = (sq - hi.astype(F32)).astype(BF16)
    e = e_ref[...]
    return _dot(hi, e) + _dot(lo, e)


def _moe_tile_to_rows(moe_ref, n_chunks):
    rows = []
    for ch in range(n_chunks):
        cols = [moe_ref[0, cq, ch].T for cq in range(D_MODEL // MOE_COLS)]
        rows.append(jnp.concatenate(cols, axis=1))
    return jnp.concatenate(rows, axis=0)


def _proj_kernel(*refs, has_moe):
    if has_moe:
        x_ref, moe_ref = refs[0], refs[1]
        refs = refs[2:]
    else:
        x_ref = refs[0]
        refs = refs[1:]
    (nmix_ref, win_ref, gqa_ref, gka_ref, e384_ref, cqn_ref, wuq_ref, ckvn_ref, wkk_ref,
     wuv_ref, gqb_ref, gkb_ref, e768_ref, cos_ref, sin_ref) = refs[:15]
    outs = refs[15:]
    if has_moe:
        qa_o, ka_o, va_o, qb_o, kb_o, vbt_o, u_o, bc_o, xs_o = outs
    else:
        qa_o, ka_o, va_o, qb_o, kb_o, vbt_o, u_o, bc_o = outs

    x = x_ref[0]
    if has_moe:
        x = x + _moe_tile_to_rows(moe_ref, TOK_TILE // MOE_CHUNK)
        xs_o[0] = x
    xb = _rms(x, nmix_ref[...]).astype(BF16)

    pa = _dot(xb, win_ref[:, 0:3 * A_WIDTH])
    qa = pa[:, 0:A_WIDTH]
    ka = pa[:, A_WIDTH:2 * A_WIDTH]
    qa_o[0] = (qa * lax.rsqrt(_group_sumsq(qa, e384_ref) * (1.0 / HEAD_DIM) + EPS) * gqa_ref[...]).astype(BF16)
    ka_o[0] = (ka * lax.rsqrt(_group_sumsq(ka, e384_ref) * (1.0 / HEAD_DIM) + EPS) * gka_ref[...]).astype(BF16)
    va_o[0] = pa[:, 2 * A_WIDTH:3 * A_WIDTH].astype(BF16)

    c0 = 3 * A_WIDTH
    pb = _dot(xb, win_ref[:, c0:c0 + 512])
    cq = _rms(pb[:, 0:MLA_Q_RANK], cqn_ref[...]).astype(BF16)
    ckv = _rms(pb[:, MLA_Q_RANK:MLA_Q_RANK + MLA_KV_RANK], ckvn_ref[...]).astype(BF16)
    kr = pb[:, MLA_Q_RANK + MLA_KV_RANK:512].astype(BF16)
    qb_raw = _dot(cq, wuq_ref[...])
    kb_raw = _dot(jnp.concatenate([ckv, kr], axis=1), wkk_ref[...])
    vb = _dot(ckv, wuv_ref[...])

    cos = jnp.concatenate([cos_ref[...]] * MLA_HEADS, axis=1)
    sin = jnp.concatenate([sin_ref[...]] * MLA_HEADS, axis=1)
    lane = lax.broadcasted_iota(jnp.int32, (TOK_TILE, QB_WIDTH), 1) & (HEAD_PAD - 1)
    first_half = (lane >= MLA_NOPE) & (lane < MLA_NOPE + MLA_ROPE // 2)

    def norm_rope(y, g_ref):
        y = y * lax.rsqrt(_group_sumsq(y, e768_ref) * (1.0 / MLA_QK) + EPS) * g_ref[...]
        rot = jnp.where(first_half,
                        pltpu.roll(y, QB_WIDTH - MLA_ROPE // 2, 1),
                        pltpu.roll(y, MLA_ROPE // 2, 1))
        return (y * cos + rot * sin).astype(BF16)

    qb_o[0] = norm_rope(qb_raw, gqb_ref)
    kb_o[0] = norm_rope(kb_raw, gkb_ref)
    vbt = vb.T
    for h in range(MLA_HEADS):
        vbt_o[0, h, 0, 0:MLA_V, :] = vbt[h * MLA_V:(h + 1) * MLA_V, :].astype(BF16)
        vbt_o[0, h, 0, MLA_V:MLA_VROWS, :] = jnp.ones((MLA_VROWS - MLA_V, TOK_TILE), BF16)

    c1 = c0 + 512
    pc = _dot(xb, win_ref[:, c1:c1 + 3 * CONV_CH])
    u_o[0] = pc[:, 2 * CONV_CH:3 * CONV_CH] * pc[:, 0:CONV_CH]
    bc_o[0] = pc[:, CONV_CH:2 * CONV_CH]


def _proj_call(x, moe, lw, tabs):
    b, s, d = x.shape
    nt = s // TOK_TILE
    has_moe = moe is not None

    def const(shape):
        return pl.BlockSpec(shape, lambda bi, ti: (0,) * len(shape))

    in_specs = [pl.BlockSpec((1, TOK_TILE, d), lambda bi, ti: (bi, ti, 0))]
    args = [x]
    if has_moe:
        ncq = D_MODEL // MOE_COLS
        in_specs.append(pl.BlockSpec((1, ncq, TOK_TILE // MOE_CHUNK, MOE_COLS, MOE_CHUNK),
                                     lambda bi, ti: (bi, 0, ti, 0, 0)))
        args.append(moe)
    consts = [lw['nmix'], lw['w_in'], lw['gqa'], lw['gka'], tabs['e384'], lw['cqn'], lw['w_uq'],
              lw['ckvn'], lw['w_kk'], lw['w_uv'], lw['gqb'], lw['gkb'], tabs['e768']]
    for c in consts:
        in_specs.append(const(c.shape))
        args.append(c)
    in_specs += [pl.BlockSpec((TOK_TILE, HEAD_PAD), lambda bi, ti: (ti, 0))] * 2
    args += [tabs['cos'], tabs['sin']]

    def tok(width, dtype):
        return (jax.ShapeDtypeStruct((b, s, width), dtype),
                pl.BlockSpec((1, TOK_TILE, width), lambda bi, ti: (bi, ti, 0)))

    outs = [tok(A_WIDTH, BF16), tok(A_WIDTH, BF16), tok(A_WIDTH, BF16),
            tok(QB_WIDTH, BF16), tok(QB_WIDTH, BF16),
            (jax.ShapeDtypeStruct((b, MLA_HEADS, nt, MLA_VROWS, TOK_TILE), BF16),
             pl.BlockSpec((1, MLA_HEADS, 1, MLA_VROWS, TOK_TILE), lambda bi, ti: (bi, 0, ti, 0, 0))),
            tok(CONV_CH, F32), tok(CONV_CH, F32)]
    if has_moe:
        outs.append(tok(d, F32))
    return pl.pallas_call(
        functools.partial(_proj_kernel, has_moe=has_moe),
        grid=(b, nt),
        in_specs=in_specs,
        out_specs=[o[1] for o in outs],
        out_shape=[o[0] for o in outs],
        compiler_params=pltpu.CompilerParams(
            dimension_semantics=("arbitrary", "arbitrary"), vmem_limit_bytes=VMEM_LIMIT),
        name="proj_moe" if has_moe else "proj",
    )(*args)


def _bias_kernel(rpb_ref, o_ref):
    h = pl.program_id(0)
    delta = pl.program_id(1)
    nw = 2 * NA_WIN_W - 1
    shape = (GRID_W, NA_WIN_H * GRID_W)
    c = lax.broadcasted_iota(jnp.int32, shape, 0)
    col = lax.broadcasted_iota(jnp.int32, shape, 1)
    wrow = lax.shift_right_logical(col, 6)
    kc = col & (GRID_W - 1)
    c_lo = jnp.clip(c - NA_WIN_W // 2, 0, GRID_W - NA_WIN_W)
    inside = (kc >= c_lo) & (kc < c_lo + NA_WIN_W)
    d = kc - c + (NA_WIN_W - 1)
    acc = jnp.zeros(shape, F32)
    base = h * ((2 * NA_WIN_H - 1) * nw)
    for i in range(NA_WIN_H):
        ri = i - delta + (NA_WIN_H - 1)
        for dd in range(nw):
            val = rpb_ref[base + ri * nw + dd]
            acc = jnp.where((wrow == i) & (d == dd), val, acc)
    o_ref[0, 0] = jnp.where(inside, acc, NEG_BIG)


def _bias_call(rpb_l):
    flat = rpb_l.reshape(-1)
    return pl.pallas_call(
        _bias_kernel,
        grid=(NA_HEADS, NA_WIN_H),
        in_specs=[pl.BlockSpec(memory_space=pltpu.SMEM)],
        out_specs=pl.BlockSpec((1, 1, GRID_W, NA_WIN_H * GRID_W), lambda h, dl: (h, dl, 0, 0)),
        out_shape=jax.ShapeDtypeStruct((NA_HEADS, NA_WIN_H, GRID_W, NA_WIN_H * GRID_W), F32),
        name="na_bias",
    )(flat)


def _na_kernel(q_ref, kp_ref, kc_ref, kn_ref, vp_ref, vc_ref, vn_ref, bias_ref, o_ref,
               k_scr, v_scr, *, rows):
    j = pl.program_id(1)
    blk = NA_ROWS * GRID_W
    k_scr[0:blk] = kp_ref[0]
    k_scr[blk:2 * blk] = kc_ref[0]
    k_scr[2 * blk:3 * blk] = kn_ref[0]
    v_scr[0:blk] = vp_ref[0]
    v_scr[blk:2 * blk] = vc_ref[0]
    v_scr[2 * blk:3 * blk] = vn_ref[0]
    lane = lax.broadcasted_iota(jnp.int32, (GRID_W, LANES), 1)
    low = lane < HEAD_DIM
    for i in range(NA_ROWS):
        r = NA_ROWS * j + i
        r0 = jnp.clip(r - NA_WIN_H // 2, 0, rows - NA_WIN_H)
        delta = r - r0
        start = pl.multiple_of((r0 - NA_ROWS * j + NA_ROWS) * GRID_W, GRID_W)
        for p in range(NA_HEADS // 2):
            lsl = slice(p * LANES, (p + 1) * LANES)
            qp = q_ref[0, i * GRID_W:(i + 1) * GRID_W, lsl]
            kw = k_scr[pl.ds(start, NA_WIN_H * GRID_W), lsl]
            vw = v_scr[pl.ds(start, NA_WIN_H * GRID_W), lsl]
            res = []
            for a in range(2):
                qm = jnp.where(low if a == 0 else jnp.logical_not(low), qp, jnp.zeros_like(qp))
                sc = _dot_nt(qm, kw) + bias_ref[2 * p + a, delta]
                m = jnp.max(sc, axis=-1, keepdims=True)
                e = jnp.exp(sc - m)
                l = jnp.sum(e, axis=-1, keepdims=True)
                res.append(_dot(e.astype(BF16), vw) / l)
            o_ref[0, i * GRID_W:(i + 1) * GRID_W, lsl] = jnp.where(low, res[0], res[1])


def _na_call(qa, ka, va, bias_tab):
    b, s, w = qa.shape
    rows = s // GRID_W
    nb = rows // NA_ROWS
    blk = NA_ROWS * GRID_W
    cur = pl.BlockSpec((1, blk, w), lambda bi, j: (bi, j, 0))
    prev = pl.BlockSpec((1, blk, w), lambda bi, j: (bi, jnp.maximum(j - 1, 0), 0))
    nxt = pl.BlockSpec((1, blk, w), lambda bi, j: (bi, jnp.minimum(j + 1, nb - 1), 0))
    return pl.pallas_call(
        functools.partial(_na_kernel, rows=rows),
        grid=(b, nb),
        in_specs=[cur, prev, cur, nxt, prev, cur, nxt,
                  pl.BlockSpec(bias_tab.shape, lambda bi, j: (0, 0, 0, 0))],
        out_specs=pl.BlockSpec((1, blk, w), lambda bi, j: (bi, j, 0)),
        out_shape=jax.ShapeDtypeStruct((b, s, w), F32),
        scratch_shapes=[pltpu.VMEM((3 * blk, w), BF16), pltpu.VMEM((3 * blk, w), BF16)],
        compiler_params=pltpu.CompilerParams(
            dimension_semantics=("arbitrary", "arbitrary"), vmem_limit_bytes=VMEM_LIMIT),
        name="na_attn",
    )(qa, ka, ka, ka, va, va, va, bias_tab)


def _mla_kernel(q_ref, k_ref, vt_ref, o_ref, s_scr, acc_scr, m_scr, *, n_chunks):
    q = q_ref[0]

    def scores(c):
        kb = k_ref[0, pl.ds(pl.multiple_of(c * TOK_TILE, TOK_TILE), TOK_TILE), :]
        return _dot_nt(kb, q)

    def update(st, c):
        m = m_scr[...]
        m_new = jnp.maximum(m, jnp.max(st, axis=0, keepdims=True))
        alpha = jnp.exp2(m - m_new)
        p = jnp.exp2(st - m_new).astype(BF16)
        acc_scr[...] = alpha * acc_scr[...] + _dot(vt_ref[0, 0, c], p)
        m_scr[...] = m_new

    m_scr[...] = jnp.full(m_scr.shape, NEG_BIG, F32)
    acc_scr[...] = jnp.zeros(acc_scr.shape, F32)
    s_scr[0] = scores(0)

    def body(i, carry):
        for j in range(MLA_UNROLL):
            c = MLA_UNROLL * i + j
            s_scr[(j + 1) % 2] = scores(jnp.minimum(c + 1, n_chunks - 1))
            update(s_scr[j % 2], c)
        return carry

    lax.fori_loop(0, n_chunks // MLA_UNROLL, body, 0)
    acc = acc_scr[...]
    o_ref[0] = acc[0:MLA_V, :] / acc[MLA_V:MLA_V + 1, :]


def _mla_call(qb, kb, vbt):
    b, s, _ = qb.shape
    nt = s // TOK_TILE
    assert nt % MLA_UNROLL == 0 and MLA_UNROLL % 2 == 0
    return pl.pallas_call(
        functools.partial(_mla_kernel, n_chunks=nt),
        grid=(b, MLA_HEADS, nt),
        in_specs=[pl.BlockSpec((1, TOK_TILE, HEAD_PAD), lambda bi, h, qi: (bi, qi, h)),
                  pl.BlockSpec((1, s, HEAD_PAD), lambda bi, h, qi: (bi, 0, h)),
                  pl.BlockSpec((1, 1, nt, MLA_VROWS, TOK_TILE), lambda bi, h, qi: (bi, h, 0, 0, 0))],
        out_specs=pl.BlockSpec((1, MLA_V, TOK_TILE), lambda bi, h, qi: (bi, h, qi)),
        out_shape=jax.ShapeDtypeStruct((b, B_WIDTH, s), F32),
        scratch_shapes=[pltpu.VMEM((2, TOK_TILE, TOK_TILE), F32), pltpu.VMEM((MLA_VROWS, TOK_TILE), F32),
                        pltpu.VMEM((1, TOK_TILE), F32)],
        compiler_params=pltpu.CompilerParams(
            dimension_semantics=("arbitrary", "arbitrary", "arbitrary"), vmem_limit_bytes=VMEM_LIMIT),
        name="mla_attn",
    )(qb, kb, vbt)


def _out_kernel(x_ref, oa_ref, obt_ref, u_ref, up_ref, un_ref, bc_ref, ga_ref, gb_ref, gc_ref,
                woa_ref, wob_ref, woc_ref, cw_ref, nffn_ref, wrh_ref, wrl_ref, br_ref,
                x1_o, xn_o, aff_o):
    ti = pl.program_id(1)
    nt = pl.num_programs(1)
    t = TOK_TILE

    na = _rms(oa_ref[0], ga_ref[...]).astype(BF16)
    acc = _dot(na, woa_ref[...])

    obt = obt_ref[0]
    nbt = obt * lax.rsqrt(jnp.mean(obt * obt, axis=0, keepdims=True) + EPS) * gb_ref[...]
    acc = acc + _dot(nbt.T.astype(BF16), wob_ref[...])

    u = u_ref[0]
    row = lax.broadcasted_iota(jnp.int32, (t, CONV_CH), 0)
    prev_row = jnp.where(ti > 0, up_ref[0, 7:8, :], 0.0)
    next_row = jnp.where(ti < nt - 1, un_ref[0, 0:1, :], 0.0)
    u_m1 = jnp.where(row == 0, prev_row, pltpu.roll(u, 1, 0))
    u_p1 = jnp.where(row == t - 1, next_row, pltpu.roll(u, t - 1, 0))
    y = cw_ref[0:1, :] * u_m1 + cw_ref[1:2, :] * u + cw_ref[2:3, :] * u_p1
    oc = _rms(bc_ref[0] * y, gc_ref[...]).astype(BF16)
    acc = acc + _dot(oc, woc_ref[...])

    x1 = x_ref[0] + acc
    x1_o[0] = x1
    xn = _rms(x1, nffn_ref[...])
    hi = xn.astype(BF16)
    lo = (xn - hi.astype(F32)).astype(BF16)
    xn_o[0] = hi
    logits = _dot(hi, wrh_ref[...]) + _dot(lo, wrh_ref[...]) + _dot(hi, wrl_ref[...]) + br_ref[...]
    m = jnp.max(logits, axis=-1, keepdims=True)
    e = jnp.exp(logits - m)
    aff = e / jnp.sum(e, axis=-1, keepdims=True)
    aff_o[0] = aff.T[0:N_EXPERTS, :]


def _out_call(x, out_a, out_bt, u, bc, lw):
    b, s, d = x.shape
    nt = s // TOK_TILE
    h8 = TOK_TILE // 8

    def const(a):
        return pl.BlockSpec(a.shape, lambda bi, ti: (0,) * a.ndim)

    def tok(width):
        return pl.BlockSpec((1, TOK_TILE, width), lambda bi, ti: (bi, ti, 0))

    consts = [lw['ga'], lw['gb_col'], lw['gc'], lw['wo_a'], lw['wo_b'], lw['wo_c'], lw['conv_w'],
              lw['nffn'], lw['wr_hi'], lw['wr_lo'], lw['b_r']]
    in_specs = [tok(d), tok(A_WIDTH),
                pl.BlockSpec((1, B_WIDTH, TOK_TILE), lambda bi, ti: (bi, 0, ti)),
                tok(CONV_CH),
                pl.BlockSpec((1, 8, CONV_CH), lambda bi, ti: (bi, jnp.maximum(ti * h8 - 1, 0), 0)),
                pl.BlockSpec((1, 8, CONV_CH), lambda bi, ti: (bi, jnp.minimum((ti + 1) * h8, s // 8 - 1), 0)),
                tok(CONV_CH)] + [const(c) for c in consts]
    return pl.pallas_call(
        _out_kernel,
        grid=(b, nt),
        in_specs=in_specs,
        out_specs=[tok(d), tok(d), pl.BlockSpec((1, N_EXPERTS, TOK_TILE), lambda bi, ti: (bi, 0, ti))],
        out_shape=[jax.ShapeDtypeStruct((b, s, d), F32), jax.ShapeDtypeStruct((b, s, d), BF16),
                   jax.ShapeDtypeStruct((b, N_EXPERTS, s), F32)],
        compiler_params=pltpu.CompilerParams(
            dimension_semantics=("arbitrary", "arbitrary"), vmem_limit_bytes=VMEM_LIMIT),
        name="out_proj",
    )(x, out_a, out_bt, u, u, u, bc, *consts)


def _topk_kernel(a_ref, tri_ref, o_slot, o_gate, o_off, *, cap):
    a = a_ref[0, 0]
    bits = pltpu.bitcast(a, jnp.int32)

    def total(mask):
        v = jnp.where(mask, 1.0, 0.0)
        return jnp.sum(jnp.sum(v, axis=0, keepdims=True), axis=1, keepdims=True)

    thr = jnp.zeros((1, 1), jnp.int32)
    for bit in range(30, -1, -1):
        cand = thr | (1 << bit)
        thr = jnp.where(total(bits >= cand) >= cap, cand, thr)
    gt = bits > thr
    eq = bits == thr
    need = cap - total(gt)

    upper = tri_ref[0]
    lower_strict = tri_ref[1]
    upper_strict = tri_ref[2]

    def excl_rank(mask):
        f = jnp.where(mask, 1.0, 0.0)
        fb = f.astype(BF16)
        in_row = _dot(fb, upper)
        before_rows = jnp.sum(_dot(lower_strict, fb), axis=1, keepdims=True)
        return before_rows + in_row - f, fb

    eq_rank, _ = excl_rank(eq)
    sel = gt | (eq & (eq_rank < need))
    pos, sel_b = excl_rank(sel)
    o_slot[0, 0] = jnp.where(sel, pos, -1.0).astype(jnp.int32)
    o_gate[0, 0] = jnp.where(sel, a, 0.0)
    row_tot = _dot_nt(jnp.ones((8, LANES), BF16), sel_b)
    o_off[0, 0] = _dot(row_tot.astype(BF16), upper_strict)


def _topk_call(aff_t, tri, cap):
    b, e, s = aff_t.shape
    pad = TOPK_ROWS * LANES - s
    a = aff_t if pad == 0 else jnp.pad(aff_t, ((0, 0), (0, 0), (0, pad)), constant_values=-1.0)
    a = a.reshape(b, e, TOPK_ROWS, LANES)
    blk = pl.BlockSpec((1, 1, TOPK_ROWS, LANES), lambda bi, ei: (bi, ei, 0, 0))
    return pl.pallas_call(
        functools.partial(_topk_kernel, cap=cap),
        grid=(b, e),
        in_specs=[blk, pl.BlockSpec(tri.shape, lambda bi, ei: (0, 0, 0))],
        out_specs=[blk, blk, pl.BlockSpec((1, 1, 8, LANES), lambda bi, ei: (bi, ei, 0, 0))],
        out_shape=[jax.ShapeDtypeStruct((b, e, TOPK_ROWS, LANES), jnp.int32),
                   jax.ShapeDtypeStruct((b, e, TOPK_ROWS, LANES), F32),
                   jax.ShapeDtypeStruct((b, e, 8, LANES), F32)],
        name="topk_select",
    )(a, tri)


def _chunk_tiles(coff_ref, base, k):
    off0 = coff_ref[base + k]
    off1 = coff_ref[base + k + 1]
    r0 = lax.shift_right_logical(off0, 8)
    r1 = lax.shift_right_logical(jnp.maximum(off1 - 1, 0), 8)
    return off1 > off0, r0, r1


def _onehot(slot_row, r):
    i = lax.broadcasted_iota(jnp.int32, (MOE_CHUNK, MOE_CHUNK), 0)
    return jnp.where(i + r * MOE_CHUNK == slot_row, 1.0, 0.0).astype(BF16)


def _gather_kernel(coff_ref, x_ref, slot_ref, o_ref, acc_ref, *, n_chunks):
    bi = pl.program_id(0)
    ei = pl.program_id(2)
    base = (bi * N_EXPERTS + ei) * (n_chunks + 1)
    acc_ref[...] = jnp.zeros_like(acc_ref)

    def body(k, carry):
        some, r0, r1 = _chunk_tiles(coff_ref, base, k)

        def add(r):
            g = _onehot(slot_ref[0, 0, k], r)
            xk = x_ref[0, pl.ds(pl.multiple_of(k * MOE_CHUNK, MOE_CHUNK), MOE_CHUNK), :]
            rows = pl.ds(pl.multiple_of(r * MOE_CHUNK, MOE_CHUNK), MOE_CHUNK)
            acc_ref[rows, :] += _dot(g, xk)

        @pl.when(some)
        def _():
            add(r0)

        @pl.when(some & (r1 > r0))
        def _():
            add(r0 + 1)

        return carry

    lax.fori_loop(0, n_chunks, body, 0)
    o_ref[0, 0] = acc_ref[...].astype(BF16)


def _gather_call(coff, xn, slot5, cap):
    b, s, d = xn.shape
    n_chunks = s // MOE_CHUNK
    ncq = d // MOE_COLS
    grid_spec = pltpu.PrefetchScalarGridSpec(
        num_scalar_prefetch=1,
        grid=(b, ncq, N_EXPERTS),
        in_specs=[pl.BlockSpec((1, s, MOE_COLS), lambda bi, cq, ei, co: (bi, 0, cq)),
                  pl.BlockSpec((1, 1, n_chunks, 1, MOE_CHUNK), lambda bi, cq, ei, co: (bi, ei, 0, 0, 0))],
        out_specs=pl.BlockSpec((1, 1, cap, MOE_COLS), lambda bi, cq, ei, co: (bi, ei, 0, cq)),
        scratch_shapes=[pltpu.VMEM((cap, MOE_COLS), F32)],
    )
    return pl.pallas_call(
        functools.partial(_gather_kernel, n_chunks=n_chunks),
        grid_spec=grid_spec,
        out_shape=jax.ShapeDtypeStruct((b, N_EXPERTS, cap, d), BF16),
        compiler_params=pltpu.CompilerParams(
            dimension_semantics=("arbitrary", "arbitrary", "arbitrary"), vmem_limit_bytes=VMEM_LIMIT),
        name="moe_gather",
    )(coff, xn, slot5)


def _ffn_kernel(xc_ref, wg_ref, wu_ref, wdt_ref, o_ref, *, tiles):
    xc = xc_ref[0, 0]
    g = _dot(xc, wg_ref[0])
    u = _dot(xc, wu_ref[0])
    h = (g / (1.0 + jnp.exp(-g)) * u).astype(BF16)
    yt = _dot_nt(wdt_ref[0], h)
    for r in range(tiles):
        o_ref[0, 0, r] = yt[:, r * MOE_CHUNK:(r + 1) * MOE_CHUNK].astype(BF16)


def _ffn_call(xc, wg, wu, wdt):
    b, e, cap, d = xc.shape
    ff = wg.shape[-1]
    rt = min(cap, 1024)
    tiles = rt // MOE_CHUNK
    return pl.pallas_call(
        functools.partial(_ffn_kernel, tiles=tiles),
        grid=(e, b, cap // rt),
        in_specs=[pl.BlockSpec((1, 1, rt, d), lambda ei, bi, ri: (bi, ei, ri, 0)),
                  pl.BlockSpec((1, d, ff), lambda ei, bi, ri: (ei, 0, 0)),
                  pl.BlockSpec((1, d, ff), lambda ei, bi, ri: (ei, 0, 0)),
                  pl.BlockSpec((1, d, ff), lambda ei, bi, ri: (ei, 0, 0))],
        out_specs=pl.BlockSpec((1, 1, tiles, d, MOE_CHUNK), lambda ei, bi, ri: (bi, ei, ri, 0, 0)),
        out_shape=jax.ShapeDtypeStruct((b, e, cap // MOE_CHUNK, d, MOE_CHUNK), BF16),
        compiler_params=pltpu.CompilerParams(
            dimension_semantics=("arbitrary", "arbitrary", "arbitrary"), vmem_limit_bytes=VMEM_LIMIT),
        name="moe_ffn",
    )(xc, wg, wu, wdt)


def _scatter_kernel(coff_ref, y_ref, slot_ref, gate_ref, o_ref, *, n_chunks):
    bi = pl.program_id(0)
    ei = pl.program_id(2)
    base = (bi * N_EXPERTS + ei) * (n_chunks + 1)

    @pl.when(ei == 0)
    def _():
        o_ref[...] = jnp.zeros_like(o_ref)

    def body(k, carry):
        some, r0, r1 = _chunk_tiles(coff_ref, base, k)

        def add(r):
            g = _onehot(slot_ref[0, 0, k], r)
            o_ref[0, 0, k] += _dot(y_ref[0, 0, r], g) * gate_ref[0, 0, k]

        @pl.when(some)
        def _():
            add(r0)

        @pl.when(some & (r1 > r0))
        def _():
            add(r0 + 1)

        return carry

    lax.fori_loop(0, n_chunks, body, 0)


def _scatter_call(coff, y5, slot5, gate5):
    b, e, tiles, d, _ = y5.shape
    n_chunks = slot5.shape[2]
    ncq = d // MOE_COLS
    small = pl.BlockSpec((1, 1, n_chunks, 1, MOE_CHUNK), lambda bi, cq, ei, co: (bi, ei, 0, 0, 0))
    grid_spec = pltpu.PrefetchScalarGridSpec(
        num_scalar_prefetch=1,
        grid=(b, ncq, N_EXPERTS),
        in_specs=[pl.BlockSpec((1, 1, tiles, MOE_COLS, MOE_CHUNK), lambda bi, cq, ei, co: (bi, ei, 0, cq, 0)),
                  small, small],
        out_specs=pl.BlockSpec((1, 1, n_chunks, MOE_COLS, MOE_CHUNK), lambda bi, cq, ei, co: (bi, cq, 0, 0, 0)),
    )
    return pl.pallas_call(
        functools.partial(_scatter_kernel, n_chunks=n_chunks),
        grid_spec=grid_spec,
        out_shape=jax.ShapeDtypeStruct((b, ncq, n_chunks, MOE_COLS, MOE_CHUNK), F32),
        compiler_params=pltpu.CompilerParams(
            dimension_semantics=("arbitrary", "arbitrary", "arbitrary"), vmem_limit_bytes=VMEM_LIMIT),
        name="moe_scatter",
    )(coff, y5, slot5, gate5)


def _final_kernel(x_ref, moe_ref, o_ref):
    o_ref[0] = x_ref[0] + _moe_tile_to_rows(moe_ref, TOK_TILE // MOE_CHUNK)


def _final_call(x1, moe):
    b, s, d = x1.shape
    ncq = d // MOE_COLS
    tok = pl.BlockSpec((1, TOK_TILE, d), lambda bi, ti: (bi, ti, 0))
    return pl.pallas_call(
        _final_kernel,
        grid=(b, s // TOK_TILE),
        in_specs=[tok, pl.BlockSpec((1, ncq, TOK_TILE // MOE_CHUNK, MOE_COLS, MOE_CHUNK),
                                    lambda bi, ti: (bi, 0, ti, 0, 0))],
        out_specs=tok,
        out_shape=jax.ShapeDtypeStruct((b, s, d), F32),
        compiler_params=pltpu.CompilerParams(
            dimension_semantics=("arbitrary", "arbitrary"), vmem_limit_bytes=VMEM_LIMIT),
        name="final_add",
    )(x1, moe)


def _block_ones(n, group):
    i = np.arange(n)
    return jnp.asarray((i[:, None] // group) == (i[None, :] // group), BF16)


def _tables(s):
    half = MLA_ROPE // 2
    inv = ROPE_THETA ** (-jnp.arange(half, dtype=F32) / half)
    ang = jnp.arange(s, dtype=jnp.int32).astype(F32)[:, None] * inv[None, :]
    cos, sin = jnp.cos(ang), jnp.sin(ang)
    ones = jnp.ones((s, MLA_NOPE), F32)
    tail = HEAD_PAD - MLA_QK
    cos_t = jnp.concatenate([ones, cos, cos, jnp.ones((s, tail), F32)], axis=1)
    sin_t = jnp.concatenate([0 * ones, -sin, sin, jnp.zeros((s, tail), F32)], axis=1)
    i = np.arange(LANES)
    tri = np.stack([i[:, None] <= i[None, :], i[None, :] < i[:, None], i[:, None] < i[None, :]])
    return {'cos': cos_t, 'sin': sin_t, 'e384': _block_ones(A_WIDTH, HEAD_DIM),
            'e768': _block_ones(QB_WIDTH, HEAD_PAD), 'tri': jnp.asarray(tri, BF16)}


def _pad_heads(v, scale=1.0):
    g = jnp.pad(v.astype(F32) * scale, (0, HEAD_PAD - MLA_QK))
    return jnp.tile(g, MLA_HEADS)[None, :]


def _layer_weights(l, p):
    w_in = p['w_in'][l]
    kr_end = 3 * A_WIDTH + MLA_Q_RANK + MLA_KV_RANK + MLA_ROPE
    w_in_r = jnp.concatenate(
        [w_in[:, :kr_end], jnp.zeros((D_MODEL, LANES - MLA_ROPE), F32), w_in[:, kr_end:]], axis=1)
    w_uq = p['w_uq'][l].reshape(MLA_Q_RANK, MLA_HEADS, MLA_QK)
    w_uq = jnp.pad(w_uq, ((0, 0), (0, 0), (0, HEAD_PAD - MLA_QK))).reshape(MLA_Q_RANK, QB_WIDTH)
    w_ukv = p['w_ukv'][l].reshape(MLA_KV_RANK, MLA_HEADS, MLA_NOPE + MLA_V)
    w_uk = jnp.pad(w_ukv[:, :, :MLA_NOPE], ((0, 0), (0, 0), (0, HEAD_PAD - MLA_NOPE)))
    w_uk = w_uk.reshape(MLA_KV_RANK, QB_WIDTH)
    w_uv = w_ukv[:, :, MLA_NOPE:].reshape(MLA_KV_RANK, B_WIDTH)
    place = np.zeros((LANES, QB_WIDTH), np.float32)
    for h in range(MLA_HEADS):
        place[np.arange(MLA_ROPE), h * HEAD_PAD + MLA_NOPE + np.arange(MLA_ROPE)] = 1.0
    w_kk = jnp.concatenate([w_uk, jnp.asarray(place)], axis=0)
    out_norm = p['out_norm'][l]
    w_out = p['w_out'][l]
    w_r = jnp.pad(p['w_router'][l], ((0, 0), (0, LANES - N_EXPERTS)))
    wr_hi = w_r.astype(BF16)
    b_r = jnp.concatenate([p['b_router'][l].astype(F32), jnp.full((LANES - N_EXPERTS,), NEG_BIG, F32)])
    return {
        'nmix': p['norm_mix'][l][None, :], 'w_in': w_in_r.astype(BF16),
        'gqa': jnp.tile(p['q_norm_a'][l] * HEAD_DIM ** -0.5, NA_HEADS)[None, :],
        'gka': jnp.tile(p['k_norm_a'][l], NA_HEADS)[None, :],
        'cqn': p['cq_norm'][l][None, :], 'w_uq': w_uq.astype(BF16),
        'ckvn': p['ckv_norm'][l][None, :], 'w_kk': w_kk.astype(BF16), 'w_uv': w_uv.astype(BF16),
        'gqb': _pad_heads(p['q_norm_b'][l], MLA_QK ** -0.5 * LOG2_E), 'gkb': _pad_heads(p['k_norm_b'][l]),
        'ga': out_norm[None, :A_WIDTH], 'gb_col': out_norm[A_WIDTH:A_WIDTH + B_WIDTH, None],
        'gc': out_norm[None, A_WIDTH + B_WIDTH:],
        'wo_a': w_out[:A_WIDTH].astype(BF16), 'wo_b': w_out[A_WIDTH:A_WIDTH + B_WIDTH].astype(BF16),
        'wo_c': w_out[A_WIDTH + B_WIDTH:].astype(BF16),
        'conv_w': p['conv_w'][l], 'nffn': p['norm_ffn'][l][None, :],
        'wr_hi': wr_hi, 'wr_lo': (w_r - wr_hi.astype(F32)).astype(BF16), 'b_r': b_r[None, :],
        'w_gate': p['w_gate'][l].astype(BF16), 'w_up': p['w_up'][l].astype(BF16),
        'w_down_t': jnp.swapaxes(p['w_down'][l], 1, 2).astype(BF16),
    }


def _layer(x, moe, lw, tabs, rpb_l):
    b, s, d = x.shape
    cap = CAPACITY_FACTOR * s // N_EXPERTS
    n_chunks = s // MOE_CHUNK

    outs = _proj_call(x, moe, lw, tabs)
    if moe is not None:
        qa, ka, va, qb, kb, vbt, u, bc, x = outs
    else:
        qa, ka, va, qb, kb, vbt, u, bc = outs
    out_a = _na_call(qa, ka, va, _bias_call(rpb_l))
    out_bt = _mla_call(qb, kb, vbt)
    x1, xn, aff_t = _out_call(x, out_a, out_bt, u, bc, lw)

    slot, gate, off = _topk_call(aff_t, tabs['tri'], cap)
    rows_per_chunk = MOE_CHUNK // LANES
    coff = off[:, :, 0, :s // LANES:rows_per_chunk].astype(jnp.int32)
    coff = jnp.concatenate([coff, jnp.full((b, N_EXPERTS, 1), cap, jnp.int32)], axis=-1).reshape(-1)
    slot5 = slot.reshape(b, N_EXPERTS, -1)[:, :, :s].reshape(b, N_EXPERTS, n_chunks, 1, MOE_CHUNK)
    gate5 = gate.reshape(b, N_EXPERTS, -1)[:, :, :s].reshape(b, N_EXPERTS, n_chunks, 1, MOE_CHUNK)

    xc = _gather_call(coff, xn, slot5, cap)
    y5 = _ffn_call(xc, lw['w_gate'], lw['w_up'], lw['w_down_t'])
    moe_out = _scatter_call(coff, y5, slot5, gate5)
    return x1, moe_out


def kernel(x, norm_mix, w_in, q_norm_a, k_norm_a, rpb, cq_norm, w_uq, ckv_norm, w_ukv, q_norm_b,
           k_norm_b, conv_w, out_norm, w_out, norm_ffn, w_router, b_router, w_gate, w_up, w_down):
    p = dict(norm_mix=norm_mix, w_in=w_in, q_norm_a=q_norm_a, k_norm_a=k_norm_a, cq_norm=cq_norm,
             w_uq=w_uq, ckv_norm=ckv_norm, w_ukv=w_ukv, q_norm_b=q_norm_b, k_norm_b=k_norm_b,
             conv_w=conv_w, out_norm=out_norm, w_out=w_out, norm_ffn=norm_ffn, w_router=w_router,
             b_router=b_router, w_gate=w_gate, w_up=w_up, w_down=w_down)
    b, s, d = x.shape
    assert d == D_MODEL and (CAPACITY_FACTOR * s // N_EXPERTS) % MOE_CHUNK == 0
    assert s % TOK_TILE == 0 and (s // GRID_W) % NA_ROWS == 0 and s <= TOPK_ROWS * LANES
    tabs = _tables(s)
    moe = None
    for l in range(norm_mix.shape[0]):
        x, moe = _layer(x, moe, _layer_weights(l, p), tabs, rpb[l])
    return _final_call(x, moe)
```

```python
import functools

import numpy as np
import jax
import jax.numpy as jnp
from jax import lax
from jax.experimental import pallas as pl
from jax.experimental.pallas import tpu as pltpu

D_MODEL = 1024
GRID_W = 64
HEAD_DIM = 64
EPS = 1e-6

NA_HEADS = 6
NA_WIN_H = 8
NA_WIN_W = 16
A_WIDTH = NA_HEADS * HEAD_DIM

MLA_HEADS = 6
MLA_Q_RANK = 256
MLA_KV_RANK = 128
MLA_NOPE = 64
MLA_ROPE = 32
MLA_QK = MLA_NOPE + MLA_ROPE
MLA_V = 64
B_WIDTH = MLA_HEADS * MLA_V
ROPE_THETA = 10000.0

CONV_CH = 256
C_WIDTH = CONV_CH

N_EXPERTS = 16
EXPERT_FF = 1024
CAPACITY_FACTOR = 2

LANES = 128
HEAD_PAD = 128
QB_WIDTH = MLA_HEADS * HEAD_PAD
IN_COLS_PAD = 2432
TOK_TILE = 512
NA_ROWS = 8
MOE_CHUNK = 256
MOE_COLS = 256
GATHER_COLS = 512
SCATTER_CHUNKS = 2
TOPK_ROWS = 128
MLA_VROWS = MLA_V + 16
LOG2_E = 1.4426950408889634
MLA_UNROLL = 4
NEG_BIG = -1e30
VMEM_LIMIT = 56 * 1024 * 1024

F32 = jnp.float32
BF16 = jnp.bfloat16


def _dot(a, b):
    return jnp.dot(a, b, preferred_element_type=F32)


def _dot_nt(a, b):
    return lax.dot_general(a, b, (((1,), (1,)), ((), ())), preferred_element_type=F32)


def _rms(x, g):
    return x * lax.rsqrt(jnp.mean(x * x, axis=-1, keepdims=True) + EPS) * g


def _group_sumsq(y, e_ref):
    sq = y * y
    hi = sq.astype(BF16)
    lo = (sq - hi.astype(F32)).astype(BF16)
    e = e_ref[...]
    return _dot(hi, e) + _dot(lo, e)


def _moe_tile_to_rows(moe_ref, n_chunks):
    rows = []
    for ch in range(n_chunks):
        cols = [moe_ref[0, cq, ch].T for cq in range(D_MODEL // MOE_COLS)]
        rows.append(jnp.concatenate(cols, axis=1))
    return jnp.concatenate(rows, axis=0)


def _proj_kernel(*refs, has_moe):
    if has_moe:
        x_ref, moe_ref = refs[0], refs[1]
        refs = refs[2:]
    else:
        x_ref = refs[0]
        refs = refs[1:]
    (nmix_ref, win_ref, gqa_ref, gka_ref, e384_ref, cqn_ref, wuq_ref, ckvn_ref, wkk_ref,
     wuv_ref, gqb_ref, gkb_ref, e768_ref, cos_ref, sin_ref) = refs[:15]
    outs = refs[15:]
    if has_moe:
        qa_o, ka_o, va_o, qb_o, kb_o, vbt_o, u_o, bc_o, xs_o = outs
    else:
        qa_o, ka_o, va_o, qb_o, kb_o, vbt_o, u_o, bc_o = outs

    x = x_ref[0]
    if has_moe:
        x = x + _moe_tile_to_rows(moe_ref, TOK_TILE // MOE_CHUNK)
        xs_o[0] = x
    xb = _rms(x, nmix_ref[...]).astype(BF16)

    pa = _dot(xb, win_ref[:, 0:3 * A_WIDTH])
    qa = pa[:, 0:A_WIDTH]
    ka = pa[:, A_WIDTH:2 * A_WIDTH]
    qa_o[0] = (qa * lax.rsqrt(_group_sumsq(qa, e384_ref) * (1.0 / HEAD_DIM) + EPS) * gqa_ref[...]).astype(BF16)
    ka_o[0] = (ka * lax.rsqrt(_group_sumsq(ka, e384_ref) * (1.0 / HEAD_DIM) + EPS) * gka_ref[...]).astype(BF16)
    va_o[0] = pa[:, 2 * A_WIDTH:3 * A_WIDTH].astype(BF16)

    c0 = 3 * A_WIDTH
    pb = _dot(xb, win_ref[:, c0:c0 + 512])
    cq = _rms(pb[:, 0:MLA_Q_RANK], cqn_ref[...]).astype(BF16)
    ckv = _rms(pb[:, MLA_Q_RANK:MLA_Q_RANK + MLA_KV_RANK], ckvn_ref[...]).astype(BF16)
    kr = pb[:, MLA_Q_RANK + MLA_KV_RANK:512].astype(BF16)
    qb_raw = _dot(cq, wuq_ref[...])
    kb_raw = _dot(jnp.concatenate([ckv, kr], axis=1), wkk_ref[...])
    vb = _dot(ckv, wuv_ref[...])

    cos = jnp.concatenate([cos_ref[...]] * MLA_HEADS, axis=1)
    sin = jnp.concatenate([sin_ref[...]] * MLA_HEADS, axis=1)
    lane = lax.broadcasted_iota(jnp.int32, (TOK_TILE, QB_WIDTH), 1) & (HEAD_PAD - 1)
    first_half = (lane >= MLA_NOPE) & (lane < MLA_NOPE + MLA_ROPE // 2)

    def norm_rope(y, g_ref):
        y = y * lax.rsqrt(_group_sumsq(y, e768_ref) * (1.0 / MLA_QK) + EPS) * g_ref[...]
        rot = jnp.where(first_half,
                        pltpu.roll(y, QB_WIDTH - MLA_ROPE // 2, 1),
                        pltpu.roll(y, MLA_ROPE // 2, 1))
        return (y * cos + rot * sin).astype(BF16)

    qb_o[0] = norm_rope(qb_raw, gqb_ref)
    kb_o[0] = norm_rope(kb_raw, gkb_ref)
    vbt = vb.T
    for h in range(MLA_HEADS):
        vbt_o[0, h, 0, 0:MLA_V, :] = vbt[h * MLA_V:(h + 1) * MLA_V, :].astype(BF16)
        vbt_o[0, h, 0, MLA_V:MLA_VROWS, :] = jnp.ones((MLA_VROWS - MLA_V, TOK_TILE), BF16)

    c1 = c0 + 512
    pc = _dot(xb, win_ref[:, c1:c1 + 3 * CONV_CH])
    u_o[0] = pc[:, 2 * CONV_CH:3 * CONV_CH] * pc[:, 0:CONV_CH]
    bc_o[0] = pc[:, CONV_CH:2 * CONV_CH]


def _proj_call(x, moe, lw, tabs):
    b, s, d = x.shape
    nt = s // TOK_TILE
    has_moe = moe is not None

    def const(shape):
        return pl.BlockSpec(shape, lambda bi, ti: (0,) * len(shape))

    in_specs = [pl.BlockSpec((1, TOK_TILE, d), lambda bi, ti: (bi, ti, 0))]
    args = [x]
    if has_moe:
        ncq = D_MODEL // MOE_COLS
        in_specs.append(pl.BlockSpec((1, ncq, TOK_TILE // MOE_CHUNK, MOE_COLS, MOE_CHUNK),
                                     lambda bi, ti: (bi, 0, ti, 0, 0)))
        args.append(moe)
    consts = [lw['nmix'], lw['w_in'], lw['gqa'], lw['gka'], tabs['e384'], lw['cqn'], lw['w_uq'],
              lw['ckvn'], lw['w_kk'], lw['w_uv'], lw['gqb'], lw['gkb'], tabs['e768']]
    for c in consts:
        in_specs.append(const(c.shape))
        args.append(c)
    in_specs += [pl.BlockSpec((TOK_TILE, HEAD_PAD), lambda bi, ti: (ti, 0))] * 2
    args += [tabs['cos'], tabs['sin']]

    def tok(width, dtype):
        return (jax.ShapeDtypeStruct((b, s, width), dtype),
                pl.BlockSpec((1, TOK_TILE, width), lambda bi, ti: (bi, ti, 0)))

    outs = [tok(A_WIDTH, BF16), tok(A_WIDTH, BF16), tok(A_WIDTH, BF16),
            tok(QB_WIDTH, BF16), tok(QB_WIDTH, BF16),
            (jax.ShapeDtypeStruct((b, MLA_HEADS, nt, MLA_VROWS, TOK_TILE), BF16),
             pl.BlockSpec((1, MLA_HEADS, 1, MLA_VROWS, TOK_TILE), lambda bi, ti: (bi, 0, ti, 0, 0))),
            tok(CONV_CH, F32), tok(CONV_CH, F32)]
    if has_moe:
        outs.append(tok(d, F32))
    return pl.pallas_call(
        functools.partial(_proj_kernel, has_moe=has_moe),
        grid=(b, nt),
        in_specs=in_specs,
        out_specs=[o[1] for o in outs],
        out_shape=[o[0] for o in outs],
        compiler_params=pltpu.CompilerParams(
            dimension_semantics=("arbitrary", "arbitrary"), vmem_limit_bytes=VMEM_LIMIT),
        name="proj_moe" if has_moe else "proj",
    )(*args)


def _bias_kernel(rpb_ref, o_ref):
    h = pl.program_id(0)
    delta = pl.program_id(1)
    nw = 2 * NA_WIN_W - 1
    shape = (GRID_W, NA_WIN_H * GRID_W)
    c = lax.broadcasted_iota(jnp.int32, shape, 0)
    col = lax.broadcasted_iota(jnp.int32, shape, 1)
    wrow = lax.shift_right_logical(col, 6)
    kc = col & (GRID_W - 1)
    c_lo = jnp.clip(c - NA_WIN_W // 2, 0, GRID_W - NA_WIN_W)
    inside = (kc >= c_lo) & (kc < c_lo + NA_WIN_W)
    d = kc - c + (NA_WIN_W - 1)
    acc = jnp.zeros(shape, F32)
    base = h * ((2 * NA_WIN_H - 1) * nw)
    for i in range(NA_WIN_H):
        ri = i - delta + (NA_WIN_H - 1)
        for dd in range(nw):
            val = rpb_ref[base + ri * nw + dd]
            acc = jnp.where((wrow == i) & (d == dd), val, acc)
    o_ref[0, 0] = jnp.where(inside, acc, NEG_BIG)


def _bias_call(rpb_l):
    flat = rpb_l.reshape(-1)
    return pl.pallas_call(
        _bias_kernel,
        grid=(NA_HEADS, NA_WIN_H),
        in_specs=[pl.BlockSpec(memory_space=pltpu.SMEM)],
        out_specs=pl.BlockSpec((1, 1, GRID_W, NA_WIN_H * GRID_W), lambda h, dl: (h, dl, 0, 0)),
        out_shape=jax.ShapeDtypeStruct((NA_HEADS, NA_WIN_H, GRID_W, NA_WIN_H * GRID_W), F32),
        name="na_bias",
    )(flat)


def _na_kernel(q_ref, kp_ref, kc_ref, kn_ref, vp_ref, vc_ref, vn_ref, bias_ref, o_ref,
               k_scr, v_scr, *, rows):
    j = pl.program_id(1)
    blk = NA_ROWS * GRID_W
    k_scr[0:blk] = kp_ref[0]
    k_scr[blk:2 * blk] = kc_ref[0]
    k_scr[2 * blk:3 * blk] = kn_ref[0]
    v_scr[0:blk] = vp_ref[0]
    v_scr[blk:2 * blk] = vc_ref[0]
    v_scr[2 * blk:3 * blk] = vn_ref[0]
    lane = lax.broadcasted_iota(jnp.int32, (GRID_W, LANES), 1)
    low = lane < HEAD_DIM
    for i in range(NA_ROWS):
        r = NA_ROWS * j + i
        r0 = jnp.clip(r - NA_WIN_H // 2, 0, rows - NA_WIN_H)
        delta = r - r0
        start = pl.multiple_of((r0 - NA_ROWS * j + NA_ROWS) * GRID_W, GRID_W)
        for p in range(NA_HEADS // 2):
            lsl = slice(p * LANES, (p + 1) * LANES)
            qp = q_ref[0, i * GRID_W:(i + 1) * GRID_W, lsl]
            kw = k_scr[pl.ds(start, NA_WIN_H * GRID_W), lsl]
            vw = v_scr[pl.ds(start, NA_WIN_H * GRID_W), lsl]
            res = []
            for a in range(2):
                qm = jnp.where(low if a == 0 else jnp.logical_not(low), qp, jnp.zeros_like(qp))
                sc = _dot_nt(qm, kw) + bias_ref[2 * p + a, delta]
                m = jnp.max(sc, axis=-1, keepdims=True)
                e = jnp.exp(sc - m)
                l = jnp.sum(e, axis=-1, keepdims=True)
                res.append(_dot(e.astype(BF16), vw) / l)
            o_ref[0, i * GRID_W:(i + 1) * GRID_W, lsl] = jnp.where(low, res[0], res[1])


def _na_call(qa, ka, va, bias_tab):
    b, s, w = qa.shape
    rows = s // GRID_W
    nb = rows // NA_ROWS
    blk = NA_ROWS * GRID_W
    cur = pl.BlockSpec((1, blk, w), lambda bi, j: (bi, j, 0))
    prev = pl.BlockSpec((1, blk, w), lambda bi, j: (bi, jnp.maximum(j - 1, 0), 0))
    nxt = pl.BlockSpec((1, blk, w), lambda bi, j: (bi, jnp.minimum(j + 1, nb - 1), 0))
    return pl.pallas_call(
        functools.partial(_na_kernel, rows=rows),
        grid=(b, nb),
        in_specs=[cur, prev, cur, nxt, prev, cur, nxt,
                  pl.BlockSpec(bias_tab.shape, lambda bi, j: (0, 0, 0, 0))],
        out_specs=pl.BlockSpec((1, blk, w), lambda bi, j: (bi, j, 0)),
        out_shape=jax.ShapeDtypeStruct((b, s, w), F32),
        scratch_shapes=[pltpu.VMEM((3 * blk, w), BF16), pltpu.VMEM((3 * blk, w), BF16)],
        compiler_params=pltpu.CompilerParams(
            dimension_semantics=("arbitrary", "arbitrary"), vmem_limit_bytes=VMEM_LIMIT),
        name="na_attn",
    )(qa, ka, ka, ka, va, va, va, bias_tab)


def _mla_kernel(q_ref, k_ref, vt_ref, o_ref, s_scr, acc_scr, m_scr, *, n_chunks):
    q = q_ref[0]

    def scores(c):
        kb = k_ref[0, pl.ds(pl.multiple_of(c * TOK_TILE, TOK_TILE), TOK_TILE), :]
        return _dot_nt(kb, q)

    def update(st, c):
        m = m_scr[...]
        m_new = jnp.maximum(m, jnp.max(st, axis=0, keepdims=True))
        alpha = jnp.exp2(m - m_new)
        p = jnp.exp2(st - m_new).astype(BF16)
        acc_scr[...] = alpha * acc_scr[...] + _dot(vt_ref[0, 0, c], p)
        m_scr[...] = m_new

    m_scr[...] = jnp.full(m_scr.shape, NEG_BIG, F32)
    acc_scr[...] = jnp.zeros(acc_scr.shape, F32)
    s_scr[0] = scores(0)

    def body(i, carry):
        for j in range(MLA_UNROLL):
            c = MLA_UNROLL * i + j
            s_scr[(j + 1) % 2] = scores(jnp.minimum(c + 1, n_chunks - 1))
            update(s_scr[j % 2], c)
        return carry

    lax.fori_loop(0, n_chunks // MLA_UNROLL, body, 0)
    acc = acc_scr[...]
    o_ref[0] = acc[0:MLA_V, :] / acc[MLA_V:MLA_V + 1, :]


def _mla_call(qb, kb, vbt):
    b, s, _ = qb.shape
    nt = s // TOK_TILE
    assert nt % MLA_UNROLL == 0 and MLA_UNROLL % 2 == 0
    return pl.pallas_call(
        functools.partial(_mla_kernel, n_chunks=nt),
        grid=(b, MLA_HEADS, nt),
        in_specs=[pl.BlockSpec((1, TOK_TILE, HEAD_PAD), lambda bi, h, qi: (bi, qi, h)),
                  pl.BlockSpec((1, s, HEAD_PAD), lambda bi, h, qi: (bi, 0, h)),
                  pl.BlockSpec((1, 1, nt, MLA_VROWS, TOK_TILE), lambda bi, h, qi: (bi, h, 0, 0, 0))],
        out_specs=pl.BlockSpec((1, MLA_V, TOK_TILE), lambda bi, h, qi: (bi, h, qi)),
        out_shape=jax.ShapeDtypeStruct((b, B_WIDTH, s), F32),
        scratch_shapes=[pltpu.VMEM((2, TOK_TILE, TOK_TILE), F32), pltpu.VMEM((MLA_VROWS, TOK_TILE), F32),
                        pltpu.VMEM((1, TOK_TILE), F32)],
        compiler_params=pltpu.CompilerParams(
            dimension_semantics=("arbitrary", "arbitrary", "arbitrary"), vmem_limit_bytes=VMEM_LIMIT),
        name="mla_attn",
    )(qb, kb, vbt)


def _out_kernel(x_ref, oa_ref, obt_ref, u_ref, up_ref, un_ref, bc_ref, ga_ref, gb_ref, gc_ref,
                woa_ref, wob_ref, woc_ref, cw_ref, nffn_ref, wrh_ref, wrl_ref, br_ref,
                x1_o, xn_o, aff_o):
    ti = pl.program_id(1)
    nt = pl.num_programs(1)
    t = TOK_TILE

    na = _rms(oa_ref[0], ga_ref[...]).astype(BF16)
    acc = _dot(na, woa_ref[...])

    obt = obt_ref[0]
    nbt = obt * lax.rsqrt(jnp.mean(obt * obt, axis=0, keepdims=True) + EPS) * gb_ref[...]
    acc = acc + _dot(nbt.T.astype(BF16), wob_ref[...])

    u = u_ref[0]
    row = lax.broadcasted_iota(jnp.int32, (t, CONV_CH), 0)
    prev_row = jnp.where(ti > 0, up_ref[0, 7:8, :], 0.0)
    next_row = jnp.where(ti < nt - 1, un_ref[0, 0:1, :], 0.0)
    u_m1 = jnp.where(row == 0, prev_row, pltpu.roll(u, 1, 0))
    u_p1 = jnp.where(row == t - 1, next_row, pltpu.roll(u, t - 1, 0))
    y = cw_ref[0:1, :] * u_m1 + cw_ref[1:2, :] * u + cw_ref[2:3, :] * u_p1
    oc = _rms(bc_ref[0] * y, gc_ref[...]).astype(BF16)
    acc = acc + _dot(oc, woc_ref[...])

    x1 = x_ref[0] + acc
    x1_o[0] = x1
    xn = _rms(x1, nffn_ref[...])
    hi = xn.astype(BF16)
    lo = (xn - hi.astype(F32)).astype(BF16)
    xn_o[0] = hi
    logits = _dot(hi, wrh_ref[...]) + _dot(lo, wrh_ref[...]) + _dot(hi, wrl_ref[...]) + br_ref[...]
    m = jnp.max(logits, axis=-1, keepdims=True)
    e = jnp.exp(logits - m)
    aff = e / jnp.sum(e, axis=-1, keepdims=True)
    aff_o[0] = aff.T[0:N_EXPERTS, :]


def _out_call(x, out_a, out_bt, u, bc, lw):
    b, s, d = x.shape
    nt = s // TOK_TILE
    h8 = TOK_TILE // 8

    def const(a):
        return pl.BlockSpec(a.shape, lambda bi, ti: (0,) * a.ndim)

    def tok(width):
        return pl.BlockSpec((1, TOK_TILE, width), lambda bi, ti: (bi, ti, 0))

    consts = [lw['ga'], lw['gb_col'], lw['gc'], lw['wo_a'], lw['wo_b'], lw['wo_c'], lw['conv_w'],
              lw['nffn'], lw['wr_hi'], lw['wr_lo'], lw['b_r']]
    in_specs = [tok(d), tok(A_WIDTH),
                pl.BlockSpec((1, B_WIDTH, TOK_TILE), lambda bi, ti: (bi, 0, ti)),
                tok(CONV_CH),
                pl.BlockSpec((1, 8, CONV_CH), lambda bi, ti: (bi, jnp.maximum(ti * h8 - 1, 0), 0)),
                pl.BlockSpec((1, 8, CONV_CH), lambda bi, ti: (bi, jnp.minimum((ti + 1) * h8, s // 8 - 1), 0)),
                tok(CONV_CH)] + [const(c) for c in consts]
    return pl.pallas_call(
        _out_kernel,
        grid=(b, nt),
        in_specs=in_specs,
        out_specs=[tok(d), tok(d), pl.BlockSpec((1, N_EXPERTS, TOK_TILE), lambda bi, ti: (bi, 0, ti))],
        out_shape=[jax.ShapeDtypeStruct((b, s, d), F32), jax.ShapeDtypeStruct((b, s, d), BF16),
                   jax.ShapeDtypeStruct((b, N_EXPERTS, s), F32)],
        compiler_params=pltpu.CompilerParams(
            dimension_semantics=("arbitrary", "arbitrary"), vmem_limit_bytes=VMEM_LIMIT),
        name="out_proj",
    )(x, out_a, out_bt, u, u, u, bc, *consts)


def _topk_kernel(a_ref, tri_ref, o_slot, o_gate, o_off, *, cap):
    a = a_ref[0, 0]
    bits = pltpu.bitcast(a, jnp.int32)

    def total(mask):
        v = jnp.where(mask, 1.0, 0.0)
        return jnp.sum(jnp.sum(v, axis=0, keepdims=True), axis=1, keepdims=True)

    thr = jnp.zeros((1, 1), jnp.int32)
    for bit in range(30, -1, -1):
        cand = thr | (1 << bit)
        thr = jnp.where(total(bits >= cand) >= cap, cand, thr)
    gt = bits > thr
    eq = bits == thr
    need = cap - total(gt)

    upper = tri_ref[0]
    lower_strict = tri_ref[1]
    upper_strict = tri_ref[2]

    def excl_rank(mask):
        f = jnp.where(mask, 1.0, 0.0)
        fb = f.astype(BF16)
        in_row = _dot(fb, upper)
        before_rows = jnp.sum(_dot(lower_strict, fb), axis=1, keepdims=True)
        return before_rows + in_row - f, fb

    eq_rank, _ = excl_rank(eq)
    sel = gt | (eq & (eq_rank < need))
    pos, sel_b = excl_rank(sel)
    o_slot[0, 0] = jnp.where(sel, pos, -1.0).astype(jnp.int32)
    o_gate[0, 0] = jnp.where(sel, a, 0.0)
    row_tot = _dot_nt(jnp.ones((8, LANES), BF16), sel_b)
    o_off[0, 0] = _dot(row_tot.astype(BF16), upper_strict)


def _topk_call(aff_t, tri, cap):
    b, e, s = aff_t.shape
    pad = TOPK_ROWS * LANES - s
    a = aff_t if pad == 0 else jnp.pad(aff_t, ((0, 0), (0, 0), (0, pad)), constant_values=-1.0)
    a = a.reshape(b, e, TOPK_ROWS, LANES)
    blk = pl.BlockSpec((1, 1, TOPK_ROWS, LANES), lambda bi, ei: (bi, ei, 0, 0))
    return pl.pallas_call(
        functools.partial(_topk_kernel, cap=cap),
        grid=(b, e),
        in_specs=[blk, pl.BlockSpec(tri.shape, lambda bi, ei: (0, 0, 0))],
        out_specs=[blk, blk, pl.BlockSpec((1, 1, 8, LANES), lambda bi, ei: (bi, ei, 0, 0))],
        out_shape=[jax.ShapeDtypeStruct((b, e, TOPK_ROWS, LANES), jnp.int32),
                   jax.ShapeDtypeStruct((b, e, TOPK_ROWS, LANES), F32),
                   jax.ShapeDtypeStruct((b, e, 8, LANES), F32)],
        name="topk_select",
    )(a, tri)


def _window_onehot(slot_row, start):
    i = lax.broadcasted_iota(jnp.int32, (2 * MOE_CHUNK, MOE_CHUNK), 0)
    return jnp.where(i == slot_row - start, 1.0, 0.0).astype(BF16)


def _window_tile(off, tiles):
    return jnp.minimum(lax.shift_right_logical(off, 8), tiles - 1)


def _gather_kernel(coff_ref, x_ref, slot_ref, o_ref, acc_ref, *, n_chunks, cap):
    bi = pl.program_id(0)
    ei = pl.program_id(2)
    base = (bi * N_EXPERTS + ei) * (n_chunks + 1)
    acc_ref[...] = jnp.zeros_like(acc_ref)

    def body(k, carry):
        start = pl.multiple_of(_window_tile(coff_ref[base + k], cap // MOE_CHUNK) * MOE_CHUNK, MOE_CHUNK)
        g = _window_onehot(slot_ref[0, 0, k], start)
        xk = x_ref[0, pl.ds(pl.multiple_of(k * MOE_CHUNK, MOE_CHUNK), MOE_CHUNK), :]
        acc_ref[pl.ds(start, 2 * MOE_CHUNK), :] += _dot(g, xk)
        return carry

    lax.fori_loop(0, n_chunks, body, 0, unroll=2)
    o_ref[0, 0] = acc_ref[0:cap, :].astype(BF16)


def _gather_call(coff, xn, slot5, cap):
    b, s, d = xn.shape
    n_chunks = s // MOE_CHUNK
    ncq = d // GATHER_COLS
    grid_spec = pltpu.PrefetchScalarGridSpec(
        num_scalar_prefetch=1,
        grid=(b, ncq, N_EXPERTS),
        in_specs=[pl.BlockSpec((1, s, GATHER_COLS), lambda bi, cq, ei, co: (bi, 0, cq)),
                  pl.BlockSpec((1, 1, n_chunks, 1, MOE_CHUNK), lambda bi, cq, ei, co: (bi, ei, 0, 0, 0))],
        out_specs=pl.BlockSpec((1, 1, cap, GATHER_COLS), lambda bi, cq, ei, co: (bi, ei, 0, cq)),
        scratch_shapes=[pltpu.VMEM((cap + MOE_CHUNK, GATHER_COLS), F32)],
    )
    return pl.pallas_call(
        functools.partial(_gather_kernel, n_chunks=n_chunks, cap=cap),
        grid_spec=grid_spec,
        out_shape=jax.ShapeDtypeStruct((b, N_EXPERTS, cap, d), BF16),
        compiler_params=pltpu.CompilerParams(
            dimension_semantics=("arbitrary", "arbitrary", "arbitrary"), vmem_limit_bytes=VMEM_LIMIT),
        name="moe_gather",
    )(coff, xn, slot5)


def _ffn_kernel(xc_ref, wg_ref, wu_ref, wdt_ref, o_ref, *, tiles):
    xc = xc_ref[0, 0]
    g = _dot(xc, wg_ref[0])
    u = _dot(xc, wu_ref[0])
    h = (g / (1.0 + jnp.exp(-g)) * u).astype(BF16)
    yt = _dot_nt(wdt_ref[0], h)
    for r in range(tiles):
        o_ref[0, 0, r] = yt[:, r * MOE_CHUNK:(r + 1) * MOE_CHUNK].astype(BF16)


def _ffn_call(xc, wg, wu, wdt):
    b, e, cap, d = xc.shape
    ff = wg.shape[-1]
    rt = min(cap, 1024)
    tiles = rt // MOE_CHUNK
    return pl.pallas_call(
        functools.partial(_ffn_kernel, tiles=tiles),
        grid=(e, b, cap // rt),
        in_specs=[pl.BlockSpec((1, 1, rt, d), lambda ei, bi, ri: (bi, ei, ri, 0)),
                  pl.BlockSpec((1, d, ff), lambda ei, bi, ri: (ei, 0, 0)),
                  pl.BlockSpec((1, d, ff), lambda ei, bi, ri: (ei, 0, 0)),
                  pl.BlockSpec((1, d, ff), lambda ei, bi, ri: (ei, 0, 0))],
        out_specs=pl.BlockSpec((1, 1, tiles, d, MOE_CHUNK), lambda ei, bi, ri: (bi, ei, ri, 0, 0)),
        out_shape=jax.ShapeDtypeStruct((b, e, cap // MOE_CHUNK, d, MOE_CHUNK), BF16),
        compiler_params=pltpu.CompilerParams(
            dimension_semantics=("arbitrary", "arbitrary", "arbitrary"), vmem_limit_bytes=VMEM_LIMIT),
        name="moe_ffn",
    )(xc, wg, wu, wdt)


def _scatter_kernel(coff_ref, y_ref, slot_ref, gate_ref, o_ref, *, n_chunks, tiles):
    bi = pl.program_id(0)
    kg = pl.program_id(2)
    for j in range(SCATTER_CHUNKS):
        k = kg * SCATTER_CHUNKS + j
        acc = jnp.zeros((MOE_COLS, MOE_CHUNK), F32)
        for e in range(N_EXPERTS):
            r0 = _window_tile(coff_ref[(bi * N_EXPERTS + e) * (n_chunks + 1) + k], tiles)
            r1 = jnp.minimum(r0 + 1, tiles - 1)
            g = _window_onehot(slot_ref[0, e, j], r0 * MOE_CHUNK)
            yw = jnp.concatenate([y_ref[0, e, r0], y_ref[0, e, r1]], axis=1)
            acc = acc + _dot(yw, g) * gate_ref[0, e, j]
        o_ref[0, 0, j] = acc


def _scatter_call(coff, y5, slot5, gate5):
    b, e, tiles, d, _ = y5.shape
    n_chunks = slot5.shape[2]
    ncq = d // MOE_COLS
    small = pl.BlockSpec((1, e, SCATTER_CHUNKS, 1, MOE_CHUNK), lambda bi, cq, kg, co: (bi, 0, kg, 0, 0))
    grid_spec = pltpu.PrefetchScalarGridSpec(
        num_scalar_prefetch=1,
        grid=(b, ncq, n_chunks // SCATTER_CHUNKS),
        in_specs=[pl.BlockSpec((1, e, tiles, MOE_COLS, MOE_CHUNK), lambda bi, cq, kg, co: (bi, 0, 0, cq, 0)),
                  small, small],
        out_specs=pl.BlockSpec((1, 1, SCATTER_CHUNKS, MOE_COLS, MOE_CHUNK),
                               lambda bi, cq, kg, co: (bi, cq, kg, 0, 0)),
    )
    return pl.pallas_call(
        functools.partial(_scatter_kernel, n_chunks=n_chunks, tiles=tiles),
        grid_spec=grid_spec,
        out_shape=jax.ShapeDtypeStruct((b, ncq, n_chunks, MOE_COLS, MOE_CHUNK), F32),
        compiler_params=pltpu.CompilerParams(
            dimension_semantics=("arbitrary", "arbitrary", "arbitrary"), vmem_limit_bytes=VMEM_LIMIT),
        name="moe_scatter",
    )(coff, y5, slot5, gate5)


def _final_kernel(x_ref, moe_ref, o_ref):
    o_ref[0] = x_ref[0] + _moe_tile_to_rows(moe_ref, TOK_TILE // MOE_CHUNK)


def _final_call(x1, moe):
    b, s, d = x1.shape
    ncq = d // MOE_COLS
    tok = pl.BlockSpec((1, TOK_TILE, d), lambda bi, ti: (bi, ti, 0))
    return pl.pallas_call(
        _final_kernel,
        grid=(b, s // TOK_TILE),
        in_specs=[tok, pl.BlockSpec((1, ncq, TOK_TILE // MOE_CHUNK, MOE_COLS, MOE_CHUNK),
                                    lambda bi, ti: (bi, 0, ti, 0, 0))],
        out_specs=tok,
        out_shape=jax.ShapeDtypeStruct((b, s, d), F32),
        compiler_params=pltpu.CompilerParams(
            dimension_semantics=("arbitrary", "arbitrary"), vmem_limit_bytes=VMEM_LIMIT),
        name="final_add",
    )(x1, moe)


def _block_ones(n, group):
    i = np.arange(n)
    return jnp.asarray((i[:, None] // group) == (i[None, :] // group), BF16)


def _tables(s):
    half = MLA_ROPE // 2
    inv = ROPE_THETA ** (-jnp.arange(half, dtype=F32) / half)
    ang = jnp.arange(s, dtype=jnp.int32).astype(F32)[:, None] * inv[None, :]
    cos, sin = jnp.cos(ang), jnp.sin(ang)
    ones = jnp.ones((s, MLA_NOPE), F32)
    tail = HEAD_PAD - MLA_QK
    cos_t = jnp.concatenate([ones, cos, cos, jnp.ones((s, tail), F32)], axis=1)
    sin_t = jnp.concatenate([0 * ones, -sin, sin, jnp.zeros((s, tail), F32)], axis=1)
    i = np.arange(LANES)
    tri = np.stack([i[:, None] <= i[None, :], i[None, :] < i[:, None], i[:, None] < i[None, :]])
    return {'cos': cos_t, 'sin': sin_t, 'e384': _block_ones(A_WIDTH, HEAD_DIM),
            'e768': _block_ones(QB_WIDTH, HEAD_PAD), 'tri': jnp.asarray(tri, BF16)}


def _pad_heads(v, scale=1.0):
    g = jnp.pad(v.astype(F32) * scale, (0, HEAD_PAD - MLA_QK))
    return jnp.tile(g, MLA_HEADS)[None, :]


def _layer_weights(l, p):
    w_in = p['w_in'][l]
    kr_end = 3 * A_WIDTH + MLA_Q_RANK + MLA_KV_RANK + MLA_ROPE
    w_in_r = jnp.concatenate(
        [w_in[:, :kr_end], jnp.zeros((D_MODEL, LANES - MLA_ROPE), F32), w_in[:, kr_end:]], axis=1)
    w_uq = p['w_uq'][l].reshape(MLA_Q_RANK, MLA_HEADS, MLA_QK)
    w_uq = jnp.pad(w_uq, ((0, 0), (0, 0), (0, HEAD_PAD - MLA_QK))).reshape(MLA_Q_RANK, QB_WIDTH)
    w_ukv = p['w_ukv'][l].reshape(MLA_KV_RANK, MLA_HEADS, MLA_NOPE + MLA_V)
    w_uk = jnp.pad(w_ukv[:, :, :MLA_NOPE], ((0, 0), (0, 0), (0, HEAD_PAD - MLA_NOPE)))
    w_uk = w_uk.reshape(MLA_KV_RANK, QB_WIDTH)
    w_uv = w_ukv[:, :, MLA_NOPE:].reshape(MLA_KV_RANK, B_WIDTH)
    place = np.zeros((LANES, QB_WIDTH), np.float32)
    for h in range(MLA_HEADS):
        place[np.arange(MLA_ROPE), h * HEAD_PAD + MLA_NOPE + np.arange(MLA_ROPE)] = 1.0
    w_kk = jnp.concatenate([w_uk, jnp.asarray(place)], axis=0)
    out_norm = p['out_norm'][l]
    w_out = p['w_out'][l]
    w_r = jnp.pad(p['w_router'][l], ((0, 0), (0, LANES - N_EXPERTS)))
    wr_hi = w_r.astype(BF16)
    b_r = jnp.concatenate([p['b_router'][l].astype(F32), jnp.full((LANES - N_EXPERTS,), NEG_BIG, F32)])
    return {
        'nmix': p['norm_mix'][l][None, :], 'w_in': w_in_r.astype(BF16),
        'gqa': jnp.tile(p['q_norm_a'][l] * HEAD_DIM ** -0.5, NA_HEADS)[None, :],
        'gka': jnp.tile(p['k_norm_a'][l], NA_HEADS)[None, :],
        'cqn': p['cq_norm'][l][None, :], 'w_uq': w_uq.astype(BF16),
        'ckvn': p['ckv_norm'][l][None, :], 'w_kk': w_kk.astype(BF16), 'w_uv': w_uv.astype(BF16),
        'gqb': _pad_heads(p['q_norm_b'][l], MLA_QK ** -0.5 * LOG2_E), 'gkb': _pad_heads(p['k_norm_b'][l]),
        'ga': out_norm[None, :A_WIDTH], 'gb_col': out_norm[A_WIDTH:A_WIDTH + B_WIDTH, None],
        'gc': out_norm[None, A_WIDTH + B_WIDTH:],
        'wo_a': w_out[:A_WIDTH].astype(BF16), 'wo_b': w_out[A_WIDTH:A_WIDTH + B_WIDTH].astype(BF16),
        'wo_c': w_out[A_WIDTH + B_WIDTH:].astype(BF16),
        'conv_w': p['conv_w'][l], 'nffn': p['norm_ffn'][l][None, :],
        'wr_hi': wr_hi, 'wr_lo': (w_r - wr_hi.astype(F32)).astype(BF16), 'b_r': b_r[None, :],
        'w_gate': p['w_gate'][l].astype(BF16), 'w_up': p['w_up'][l].astype(BF16),
        'w_down_t': jnp.swapaxes(p['w_down'][l], 1, 2).astype(BF16),
    }


def _layer(x, moe, lw, tabs, rpb_l):
    b, s, d = x.shape
    cap = CAPACITY_FACTOR * s // N_EXPERTS
    n_chunks = s // MOE_CHUNK

    outs = _proj_call(x, moe, lw, tabs)
    if moe is not None:
        qa, ka, va, qb, kb, vbt, u, bc, x = outs
    else:
        qa, ka, va, qb, kb, vbt, u, bc = outs
    out_a = _na_call(qa, ka, va, _bias_call(rpb_l))
    out_bt = _mla_call(qb, kb, vbt)
    x1, xn, aff_t = _out_call(x, out_a, out_bt, u, bc, lw)

    slot, gate, off = _topk_call(aff_t, tabs['tri'], cap)
    rows_per_chunk = MOE_CHUNK // LANES
    coff = off[:, :, 0, :s // LANES:rows_per_chunk].astype(jnp.int32)
    coff = jnp.concatenate([coff, jnp.full((b, N_EXPERTS, 1), cap, jnp.int32)], axis=-1).reshape(-1)
    slot5 = slot.reshape(b, N_EXPERTS, -1)[:, :, :s].reshape(b, N_EXPERTS, n_chunks, 1, MOE_CHUNK)
    gate5 = gate.reshape(b, N_EXPERTS, -1)[:, :, :s].reshape(b, N_EXPERTS, n_chunks, 1, MOE_CHUNK)

    xc = _gather_call(coff, xn, slot5, cap)
    y5 = _ffn_call(xc, lw['w_gate'], lw['w_up'], lw['w_down_t'])
    moe_out = _scatter_call(coff, y5, slot5, gate5)
    return x1, moe_out


def kernel(x, norm_mix, w_in, q_norm_a, k_norm_a, rpb, cq_norm, w_uq, ckv_norm, w_ukv, q_norm_b,
           k_norm_b, conv_w, out_norm, w_out, norm_ffn, w_router, b_router, w_gate, w_up, w_down):
    p = dict(norm_mix=norm_mix, w_in=w_in, q_norm_a=q_norm_a, k_norm_a=k_norm_a, cq_norm=cq_norm,
             w_uq=w_uq, ckv_norm=ckv_norm, w_ukv=w_ukv, q_norm_b=q_norm_b, k_norm_b=k_norm_b,
             conv_w=conv_w, out_norm=out_norm, w_out=w_out, norm_ffn=norm_ffn, w_router=w_router,
             b_router=b_router, w_gate=w_gate, w_up=w_up, w_down=w_down)
    b, s, d = x.shape
    assert d == D_MODEL and (CAPACITY_FACTOR * s // N_EXPERTS) % MOE_CHUNK == 0
    assert s % TOK_TILE == 0 and (s // GRID_W) % NA_ROWS == 0 and s <= TOPK_ROWS * LANES
    tabs = _tables(s)
    moe = None
    for l in range(norm_mix.shape[0]):
        x, moe = _layer(x, moe, _layer_weights(l, p), tabs, rpb[l])
    return _final_call(x, moe)
```

```python
import functools

import numpy as np
import jax
import jax.numpy as jnp
from jax import lax
from jax.experimental import pallas as pl
from jax.experimental.pallas import tpu as pltpu

D_MODEL = 1024
GRID_W = 64
HEAD_DIM = 64
EPS = 1e-6

NA_HEADS = 6
NA_WIN_H = 8
NA_WIN_W = 16
A_WIDTH = NA_HEADS * HEAD_DIM

MLA_HEADS = 6
MLA_Q_RANK = 256
MLA_KV_RANK = 128
MLA_NOPE = 64
MLA_ROPE = 32
MLA_QK = MLA_NOPE + MLA_ROPE
MLA_V = 64
B_WIDTH = MLA_HEADS * MLA_V
ROPE_THETA = 10000.0

CONV_CH = 256
C_WIDTH = CONV_CH

N_EXPERTS = 16
EXPERT_FF = 1024
CAPACITY_FACTOR = 2

LANES = 128
HEAD_PAD = 128
QB_WIDTH = MLA_HEADS * HEAD_PAD
IN_COLS_PAD = 2432
TOK_TILE = 512
NA_ROWS = 8
MOE_CHUNK = 256
MOE_COLS = 256
GATHER_COLS = 512
SCATTER_CHUNKS = 2
TOPK_ROWS = 128
MLA_VROWS = MLA_V + 16
LOG2_E = 1.4426950408889634
MLA_UNROLL = 8
NEG_BIG = -1e30
VMEM_LIMIT = 56 * 1024 * 1024

F32 = jnp.float32
BF16 = jnp.bfloat16


def _dot(a, b):
    return jnp.dot(a, b, preferred_element_type=F32)


def _dot_nt(a, b):
    return lax.dot_general(a, b, (((1,), (1,)), ((), ())), preferred_element_type=F32)


def _rms(x, g):
    return x * lax.rsqrt(jnp.mean(x * x, axis=-1, keepdims=True) + EPS) * g


def _group_sumsq(y, e_ref):
    sq = y * y
    hi = sq.astype(BF16)
    lo = (sq - hi.astype(F32)).astype(BF16)
    e = e_ref[...]
    return _dot(hi, e) + _dot(lo, e)


def _moe_tile_to_rows(moe_ref, n_chunks):
    rows = []
    for ch in range(n_chunks):
        cols = [moe_ref[0, cq, ch].T for cq in range(D_MODEL // MOE_COLS)]
        rows.append(jnp.concatenate(cols, axis=1))
    return jnp.concatenate(rows, axis=0)


def _proj_kernel(*refs, has_moe):
    if has_moe:
        x_ref, moe_ref = refs[0], refs[1]
        refs = refs[2:]
    else:
        x_ref = refs[0]
        refs = refs[1:]
    (nmix_ref, win_ref, gqa_ref, gka_ref, e384_ref, cqn_ref, wuq_ref, ckvn_ref, wkk_ref,
     wuv_ref, gqb_ref, gkb_ref, e768_ref, cos_ref, sin_ref) = refs[:15]
    outs = refs[15:]
    if has_moe:
        qa_o, ka_o, va_o, qb_o, kb_o, vbt_o, u_o, bc_o, xs_o = outs
    else:
        qa_o, ka_o, va_o, qb_o, kb_o, vbt_o, u_o, bc_o = outs

    x = x_ref[0]
    if has_moe:
        x = x + _moe_tile_to_rows(moe_ref, TOK_TILE // MOE_CHUNK)
        xs_o[0] = x
    xb = _rms(x, nmix_ref[...]).astype(BF16)

    pa = _dot(xb, win_ref[:, 0:3 * A_WIDTH])
    qa = pa[:, 0:A_WIDTH]
    ka = pa[:, A_WIDTH:2 * A_WIDTH]
    qa_o[0] = (qa * lax.rsqrt(_group_sumsq(qa, e384_ref) * (1.0 / HEAD_DIM) + EPS) * gqa_ref[...]).astype(BF16)
    ka_o[0] = (ka * lax.rsqrt(_group_sumsq(ka, e384_ref) * (1.0 / HEAD_DIM) + EPS) * gka_ref[...]).astype(BF16)
    va_o[0] = pa[:, 2 * A_WIDTH:3 * A_WIDTH].astype(BF16)

    c0 = 3 * A_WIDTH
    pb = _dot(xb, win_ref[:, c0:c0 + 512])
    cq = _rms(pb[:, 0:MLA_Q_RANK], cqn_ref[...]).astype(BF16)
    ckv = _rms(pb[:, MLA_Q_RANK:MLA_Q_RANK + MLA_KV_RANK], ckvn_ref[...]).astype(BF16)
    kr = pb[:, MLA_Q_RANK + MLA_KV_RANK:512].astype(BF16)
    qb_raw = _dot(cq, wuq_ref[...])
    kb_raw = _dot(jnp.concatenate([ckv, kr], axis=1), wkk_ref[...])
    vb = _dot(ckv, wuv_ref[...])

    cos = jnp.concatenate([cos_ref[...]] * MLA_HEADS, axis=1)
    sin = jnp.concatenate([sin_ref[...]] * MLA_HEADS, axis=1)
    lane = lax.broadcasted_iota(jnp.int32, (TOK_TILE, QB_WIDTH), 1) & (HEAD_PAD - 1)
    first_half = (lane >= MLA_NOPE) & (lane < MLA_NOPE + MLA_ROPE // 2)

    def norm_rope(y, g_ref):
        y = y * lax.rsqrt(_group_sumsq(y, e768_ref) * (1.0 / MLA_QK) + EPS) * g_ref[...]
        rot = jnp.where(first_half,
                        pltpu.roll(y, QB_WIDTH - MLA_ROPE // 2, 1),
                        pltpu.roll(y, MLA_ROPE // 2, 1))
        return (y * cos + rot * sin).astype(BF16)

    qb_o[0] = norm_rope(qb_raw, gqb_ref)
    kb_o[0] = norm_rope(kb_raw, gkb_ref)
    vbt = vb.T
    for h in range(MLA_HEADS):
        vbt_o[0, h, 0, 0:MLA_V, :] = vbt[h * MLA_V:(h + 1) * MLA_V, :].astype(BF16)
        vbt_o[0, h, 0, MLA_V:MLA_VROWS, :] = jnp.ones((MLA_VROWS - MLA_V, TOK_TILE), BF16)

    c1 = c0 + 512
    pc = _dot(xb, win_ref[:, c1:c1 + 3 * CONV_CH])
    u_o[0] = pc[:, 2 * CONV_CH:3 * CONV_CH] * pc[:, 0:CONV_CH]
    bc_o[0] = pc[:, CONV_CH:2 * CONV_CH]


def _proj_call(x, moe, lw, tabs):
    b, s, d = x.shape
    nt = s // TOK_TILE
    has_moe = moe is not None

    def const(shape):
        return pl.BlockSpec(shape, lambda bi, ti: (0,) * len(shape))

    in_specs = [pl.BlockSpec((1, TOK_TILE, d), lambda bi, ti: (bi, ti, 0))]
    args = [x]
    if has_moe:
        ncq = D_MODEL // MOE_COLS
        in_specs.append(pl.BlockSpec((1, ncq, TOK_TILE // MOE_CHUNK, MOE_COLS, MOE_CHUNK),
                                     lambda bi, ti: (bi, 0, ti, 0, 0)))
        args.append(moe)
    consts = [lw['nmix'], lw['w_in'], lw['gqa'], lw['gka'], tabs['e384'], lw['cqn'], lw['w_uq'],
              lw['ckvn'], lw['w_kk'], lw['w_uv'], lw['gqb'], lw['gkb'], tabs['e768']]
    for c in consts:
        in_specs.append(const(c.shape))
        args.append(c)
    in_specs += [pl.BlockSpec((TOK_TILE, HEAD_PAD), lambda bi, ti: (ti, 0))] * 2
    args += [tabs['cos'], tabs['sin']]

    def tok(width, dtype):
        return (jax.ShapeDtypeStruct((b, s, width), dtype),
                pl.BlockSpec((1, TOK_TILE, width), lambda bi, ti: (bi, ti, 0)))

    outs = [tok(A_WIDTH, BF16), tok(A_WIDTH, BF16), tok(A_WIDTH, BF16),
            tok(QB_WIDTH, BF16), tok(QB_WIDTH, BF16),
            (jax.ShapeDtypeStruct((b, MLA_HEADS, nt, MLA_VROWS, TOK_TILE), BF16),
             pl.BlockSpec((1, MLA_HEADS, 1, MLA_VROWS, TOK_TILE), lambda bi, ti: (bi, 0, ti, 0, 0))),
            tok(CONV_CH, F32), tok(CONV_CH, F32)]
    if has_moe:
        outs.append(tok(d, F32))
    return pl.pallas_call(
        functools.partial(_proj_kernel, has_moe=has_moe),
        grid=(b, nt),
        in_specs=in_specs,
        out_specs=[o[1] for o in outs],
        out_shape=[o[0] for o in outs],
        compiler_params=pltpu.CompilerParams(
            dimension_semantics=("arbitrary", "arbitrary"), vmem_limit_bytes=VMEM_LIMIT),
        name="proj_moe" if has_moe else "proj",
    )(*args)


def _bias_kernel(rpb_ref, o_ref):
    h = pl.program_id(0)
    delta = pl.program_id(1)
    nw = 2 * NA_WIN_W - 1
    shape = (GRID_W, NA_WIN_H * GRID_W)
    c = lax.broadcasted_iota(jnp.int32, shape, 0)
    col = lax.broadcasted_iota(jnp.int32, shape, 1)
    wrow = lax.shift_right_logical(col, 6)
    kc = col & (GRID_W - 1)
    c_lo = jnp.clip(c - NA_WIN_W // 2, 0, GRID_W - NA_WIN_W)
    inside = (kc >= c_lo) & (kc < c_lo + NA_WIN_W)
    d = kc - c + (NA_WIN_W - 1)
    acc = jnp.zeros(shape, F32)
    base = h * ((2 * NA_WIN_H - 1) * nw)
    for i in range(NA_WIN_H):
        ri = i - delta + (NA_WIN_H - 1)
        for dd in range(nw):
            val = rpb_ref[base + ri * nw + dd]
            acc = jnp.where((wrow == i) & (d == dd), val, acc)
    o_ref[0, 0] = jnp.where(inside, acc, NEG_BIG)


def _bias_call(rpb_l):
    flat = rpb_l.reshape(-1)
    return pl.pallas_call(
        _bias_kernel,
        grid=(NA_HEADS, NA_WIN_H),
        in_specs=[pl.BlockSpec(memory_space=pltpu.SMEM)],
        out_specs=pl.BlockSpec((1, 1, GRID_W, NA_WIN_H * GRID_W), lambda h, dl: (h, dl, 0, 0)),
        out_shape=jax.ShapeDtypeStruct((NA_HEADS, NA_WIN_H, GRID_W, NA_WIN_H * GRID_W), F32),
        name="na_bias",
    )(flat)


def _na_kernel(q_ref, kp_ref, kc_ref, kn_ref, vp_ref, vc_ref, vn_ref, bias_ref, o_ref,
               k_scr, v_scr, *, rows):
    j = pl.program_id(1)
    blk = NA_ROWS * GRID_W
    k_scr[0:blk] = kp_ref[0]
    k_scr[blk:2 * blk] = kc_ref[0]
    k_scr[2 * blk:3 * blk] = kn_ref[0]
    v_scr[0:blk] = vp_ref[0]
    v_scr[blk:2 * blk] = vc_ref[0]
    v_scr[2 * blk:3 * blk] = vn_ref[0]
    lane = lax.broadcasted_iota(jnp.int32, (GRID_W, LANES), 1)
    low = lane < HEAD_DIM
    win = NA_WIN_H * GRID_W

    def window_start(i):
        r = NA_ROWS * j + i
        r0 = jnp.clip(r - NA_WIN_H // 2, 0, rows - NA_WIN_H)
        return r - r0, pl.multiple_of((r0 - NA_ROWS * j + NA_ROWS) * GRID_W, GRID_W)

    tiles = []
    for i in range(NA_ROWS):
        delta, start = window_start(i)
        for p in range(NA_HEADS // 2):
            lsl = slice(p * LANES, (p + 1) * LANES)
            qp = q_ref[0, i * GRID_W:(i + 1) * GRID_W, lsl]
            kw = k_scr[pl.ds(start, win), lsl]
            for a in range(2):
                qm = jnp.where(low if a == 0 else jnp.logical_not(low), qp, jnp.zeros_like(qp))
                tiles.append(_dot_nt(qm, kw) + bias_ref[2 * p + a, delta])
    sc = jnp.concatenate(tiles, axis=0)
    e = jnp.exp(sc - jnp.max(sc, axis=-1, keepdims=True))
    inv = 1.0 / jnp.sum(e, axis=-1, keepdims=True)
    eb = e.astype(BF16)
    t = 0
    for i in range(NA_ROWS):
        _, start = window_start(i)
        for p in range(NA_HEADS // 2):
            lsl = slice(p * LANES, (p + 1) * LANES)
            vw = v_scr[pl.ds(start, win), lsl]
            res = []
            for a in range(2):
                rsl = slice(t * GRID_W, (t + 1) * GRID_W)
                res.append(_dot(eb[rsl], vw) * inv[rsl])
                t += 1
            o_ref[0, i * GRID_W:(i + 1) * GRID_W, lsl] = jnp.where(low, res[0], res[1])


def _na_call(qa, ka, va, bias_tab):
    b, s, w = qa.shape
    rows = s // GRID_W
    nb = rows // NA_ROWS
    blk = NA_ROWS * GRID_W
    cur = pl.BlockSpec((1, blk, w), lambda bi, j: (bi, j, 0))
    prev = pl.BlockSpec((1, blk, w), lambda bi, j: (bi, jnp.maximum(j - 1, 0), 0))
    nxt = pl.BlockSpec((1, blk, w), lambda bi, j: (bi, jnp.minimum(j + 1, nb - 1), 0))
    return pl.pallas_call(
        functools.partial(_na_kernel, rows=rows),
        grid=(b, nb),
        in_specs=[cur, prev, cur, nxt, prev, cur, nxt,
                  pl.BlockSpec(bias_tab.shape, lambda bi, j: (0, 0, 0, 0))],
        out_specs=pl.BlockSpec((1, blk, w), lambda bi, j: (bi, j, 0)),
        out_shape=jax.ShapeDtypeStruct((b, s, w), F32),
        scratch_shapes=[pltpu.VMEM((3 * blk, w), BF16), pltpu.VMEM((3 * blk, w), BF16)],
        compiler_params=pltpu.CompilerParams(
            dimension_semantics=("arbitrary", "arbitrary"), vmem_limit_bytes=VMEM_LIMIT),
        name="na_attn",
    )(qa, ka, ka, ka, va, va, va, bias_tab)


def _mla_kernel(q_ref, k_ref, vt_ref, o_ref, s_scr, acc_scr, m_scr, *, n_chunks):
    q = q_ref[0]

    def scores(c):
        kb = k_ref[0, pl.ds(pl.multiple_of(c * TOK_TILE, TOK_TILE), TOK_TILE), :]
        return _dot_nt(kb, q)

    def update(st, c):
        m = m_scr[...]
        m_new = jnp.maximum(m, jnp.max(st, axis=0, keepdims=True))
        alpha = jnp.exp2(m - m_new)
        p = jnp.exp2(st - m_new).astype(BF16)
        acc_scr[...] = alpha * acc_scr[...] + _dot(vt_ref[0, 0, c], p)
        m_scr[...] = m_new

    m_scr[...] = jnp.full(m_scr.shape, NEG_BIG, F32)
    acc_scr[...] = jnp.zeros(acc_scr.shape, F32)
    s_scr[0] = scores(0)

    def body(i, carry):
        for j in range(MLA_UNROLL):
            c = MLA_UNROLL * i + j
            s_scr[(j + 1) % 2] = scores(jnp.minimum(c + 1, n_chunks - 1))
            update(s_scr[j % 2], c)
        return carry

    lax.fori_loop(0, n_chunks // MLA_UNROLL, body, 0)
    acc = acc_scr[...]
    o_ref[0] = acc[0:MLA_V, :] / acc[MLA_V:MLA_V + 1, :]


def _mla_call(qb, kb, vbt):
    b, s, _ = qb.shape
    nt = s // TOK_TILE
    assert nt % MLA_UNROLL == 0 and MLA_UNROLL % 2 == 0
    return pl.pallas_call(
        functools.partial(_mla_kernel, n_chunks=nt),
        grid=(b, MLA_HEADS, nt),
        in_specs=[pl.BlockSpec((1, TOK_TILE, HEAD_PAD), lambda bi, h, qi: (bi, qi, h)),
                  pl.BlockSpec((1, s, HEAD_PAD), lambda bi, h, qi: (bi, 0, h)),
                  pl.BlockSpec((1, 1, nt, MLA_VROWS, TOK_TILE), lambda bi, h, qi: (bi, h, 0, 0, 0))],
        out_specs=pl.BlockSpec((1, MLA_V, TOK_TILE), lambda bi, h, qi: (bi, h, qi)),
        out_shape=jax.ShapeDtypeStruct((b, B_WIDTH, s), F32),
        scratch_shapes=[pltpu.VMEM((2, TOK_TILE, TOK_TILE), F32), pltpu.VMEM((MLA_VROWS, TOK_TILE), F32),
                        pltpu.VMEM((1, TOK_TILE), F32)],
        compiler_params=pltpu.CompilerParams(
            dimension_semantics=("arbitrary", "arbitrary", "arbitrary"), vmem_limit_bytes=VMEM_LIMIT),
        name="mla_attn",
    )(qb, kb, vbt)


def _out_kernel(x_ref, oa_ref, obt_ref, u_ref, up_ref, un_ref, bc_ref, ga_ref, gb_ref, gc_ref,
                woa_ref, wob_ref, woc_ref, cw_ref, nffn_ref, wrh_ref, wrl_ref, br_ref,
                x1_o, xn_o, aff_o):
    ti = pl.program_id(1)
    nt = pl.num_programs(1)
    t = TOK_TILE

    na = _rms(oa_ref[0], ga_ref[...]).astype(BF16)
    acc = _dot(na, woa_ref[...])

    obt = obt_ref[0]
    nbt = obt * lax.rsqrt(jnp.mean(obt * obt, axis=0, keepdims=True) + EPS) * gb_ref[...]
    acc = acc + _dot(nbt.T.astype(BF16), wob_ref[...])

    u = u_ref[0]
    row = lax.broadcasted_iota(jnp.int32, (t, CONV_CH), 0)
    prev_row = jnp.where(ti > 0, up_ref[0, 7:8, :], 0.0)
    next_row = jnp.where(ti < nt - 1, un_ref[0, 0:1, :], 0.0)
    u_m1 = jnp.where(row == 0, prev_row, pltpu.roll(u, 1, 0))
    u_p1 = jnp.where(row == t - 1, next_row, pltpu.roll(u, t - 1, 0))
    y = cw_ref[0:1, :] * u_m1 + cw_ref[1:2, :] * u + cw_ref[2:3, :] * u_p1
    oc = _rms(bc_ref[0] * y, gc_ref[...]).astype(BF16)
    acc = acc + _dot(oc, woc_ref[...])

    x1 = x_ref[0] + acc
    x1_o[0] = x1
    xn = _rms(x1, nffn_ref[...])
    hi = xn.astype(BF16)
    lo = (xn - hi.astype(F32)).astype(BF16)
    xn_o[0] = hi
    logits = _dot(hi, wrh_ref[...]) + _dot(lo, wrh_ref[...]) + _dot(hi, wrl_ref[...]) + br_ref[...]
    m = jnp.max(logits, axis=-1, keepdims=True)
    e = jnp.exp(logits - m)
    aff = e / jnp.sum(e, axis=-1, keepdims=True)
    aff_o[0] = aff.T[0:N_EXPERTS, :]


def _out_call(x, out_a, out_bt, u, bc, lw):
    b, s, d = x.shape
    nt = s // TOK_TILE
    h8 = TOK_TILE // 8

    def const(a):
        return pl.BlockSpec(a.shape, lambda bi, ti: (0,) * a.ndim)

    def tok(width):
        return pl.BlockSpec((1, TOK_TILE, width), lambda bi, ti: (bi, ti, 0))

    consts = [lw['ga'], lw['gb_col'], lw['gc'], lw['wo_a'], lw['wo_b'], lw['wo_c'], lw['conv_w'],
              lw['nffn'], lw['wr_hi'], lw['wr_lo'], lw['b_r']]
    in_specs = [tok(d), tok(A_WIDTH),
                pl.BlockSpec((1, B_WIDTH, TOK_TILE), lambda bi, ti: (bi, 0, ti)),
                tok(CONV_CH),
                pl.BlockSpec((1, 8, CONV_CH), lambda bi, ti: (bi, jnp.maximum(ti * h8 - 1, 0), 0)),
                pl.BlockSpec((1, 8, CONV_CH), lambda bi, ti: (bi, jnp.minimum((ti + 1) * h8, s // 8 - 1), 0)),
                tok(CONV_CH)] + [const(c) for c in consts]
    return pl.pallas_call(
        _out_kernel,
        grid=(b, nt),
        in_specs=in_specs,
        out_specs=[tok(d), tok(d), pl.BlockSpec((1, N_EXPERTS, TOK_TILE), lambda bi, ti: (bi, 0, ti))],
        out_shape=[jax.ShapeDtypeStruct((b, s, d), F32), jax.ShapeDtypeStruct((b, s, d), BF16),
                   jax.ShapeDtypeStruct((b, N_EXPERTS, s), F32)],
        compiler_params=pltpu.CompilerParams(
            dimension_semantics=("arbitrary", "arbitrary"), vmem_limit_bytes=VMEM_LIMIT),
        name="out_proj",
    )(x, out_a, out_bt, u, u, u, bc, *consts)


def _topk_kernel(a_ref, tri_ref, o_slot, o_gate, o_off, *, cap):
    a = a_ref[0, 0]
    bits = pltpu.bitcast(a, jnp.int32)

    def total(mask):
        v = jnp.where(mask, 1.0, 0.0)
        return jnp.sum(jnp.sum(v, axis=0, keepdims=True), axis=1, keepdims=True)

    thr = jnp.zeros((1, 1), jnp.int32)
    for bit in range(30, -1, -1):
        cand = thr | (1 << bit)
        thr = jnp.where(total(bits >= cand) >= cap, cand, thr)
    gt = bits > thr
    eq = bits == thr
    need = cap - total(gt)

    upper = tri_ref[0]
    lower_strict = tri_ref[1]
    upper_strict = tri_ref[2]

    def excl_rank(mask):
        f = jnp.where(mask, 1.0, 0.0)
        fb = f.astype(BF16)
        in_row = _dot(fb, upper)
        before_rows = jnp.sum(_dot(lower_strict, fb), axis=1, keepdims=True)
        return before_rows + in_row - f, fb

    eq_rank, _ = excl_rank(eq)
    sel = gt | (eq & (eq_rank < need))
    pos, sel_b = excl_rank(sel)
    o_slot[0, 0] = jnp.where(sel, pos, -1.0).astype(jnp.int32)
    o_gate[0, 0] = jnp.where(sel, a, 0.0)
    row_tot = _dot_nt(jnp.ones((8, LANES), BF16), sel_b)
    o_off[0, 0] = _dot(row_tot.astype(BF16), upper_strict)


def _topk_call(aff_t, tri, cap):
    b, e, s = aff_t.shape
    pad = TOPK_ROWS * LANES - s
    a = aff_t if pad == 0 else jnp.pad(aff_t, ((0, 0), (0, 0), (0, pad)), constant_values=-1.0)
    a = a.reshape(b, e, TOPK_ROWS, LANES)
    blk = pl.BlockSpec((1, 1, TOPK_ROWS, LANES), lambda bi, ei: (bi, ei, 0, 0))
    return pl.pallas_call(
        functools.partial(_topk_kernel, cap=cap),
        grid=(b, e),
        in_specs=[blk, pl.BlockSpec(tri.shape, lambda bi, ei: (0, 0, 0))],
        out_specs=[blk, blk, pl.BlockSpec((1, 1, 8, LANES), lambda bi, ei: (bi, ei, 0, 0))],
        out_shape=[jax.ShapeDtypeStruct((b, e, TOPK_ROWS, LANES), jnp.int32),
                   jax.ShapeDtypeStruct((b, e, TOPK_ROWS, LANES), F32),
                   jax.ShapeDtypeStruct((b, e, 8, LANES), F32)],
        name="topk_select",
    )(a, tri)


def _window_onehot(slot_row, start):
    i = lax.broadcasted_iota(jnp.int32, (2 * MOE_CHUNK, MOE_CHUNK), 0)
    return jnp.where(i == slot_row - start, 1.0, 0.0).astype(BF16)


def _window_tile(off, tiles):
    return jnp.minimum(lax.shift_right_logical(off, 8), tiles - 1)


def _gather_kernel(coff_ref, x_ref, slot_ref, o_ref, acc_ref, *, n_chunks, cap):
    bi = pl.program_id(0)
    ei = pl.program_id(2)
    base = (bi * N_EXPERTS + ei) * (n_chunks + 1)
    acc_ref[...] = jnp.zeros_like(acc_ref)

    def body(k, carry):
        start = pl.multiple_of(_window_tile(coff_ref[base + k], cap // MOE_CHUNK) * MOE_CHUNK, MOE_CHUNK)
        g = _window_onehot(slot_ref[0, 0, k], start)
        xk = x_ref[0, pl.ds(pl.multiple_of(k * MOE_CHUNK, MOE_CHUNK), MOE_CHUNK), :]
        acc_ref[pl.ds(start, 2 * MOE_CHUNK), :] += _dot(g, xk)
        return carry

    lax.fori_loop(0, n_chunks, body, 0, unroll=2)
    o_ref[0, 0] = acc_ref[0:cap, :].astype(BF16)


def _gather_call(coff, xn, slot5, cap):
    b, s, d = xn.shape
    n_chunks = s // MOE_CHUNK
    ncq = d // GATHER_COLS
    grid_spec = pltpu.PrefetchScalarGridSpec(
        num_scalar_prefetch=1,
        grid=(b, ncq, N_EXPERTS),
        in_specs=[pl.BlockSpec((1, s, GATHER_COLS), lambda bi, cq, ei, co: (bi, 0, cq)),
                  pl.BlockSpec((1, 1, n_chunks, 1, MOE_CHUNK), lambda bi, cq, ei, co: (bi, ei, 0, 0, 0))],
        out_specs=pl.BlockSpec((1, 1, cap, GATHER_COLS), lambda bi, cq, ei, co: (bi, ei, 0, cq)),
        scratch_shapes=[pltpu.VMEM((cap + MOE_CHUNK, GATHER_COLS), F32)],
    )
    return pl.pallas_call(
        functools.partial(_gather_kernel, n_chunks=n_chunks, cap=cap),
        grid_spec=grid_spec,
        out_shape=jax.ShapeDtypeStruct((b, N_EXPERTS, cap, d), BF16),
        compiler_params=pltpu.CompilerParams(
            dimension_semantics=("arbitrary", "arbitrary", "arbitrary"), vmem_limit_bytes=VMEM_LIMIT),
        name="moe_gather",
    )(coff, xn, slot5)


def _ffn_kernel(xc_ref, wg_ref, wu_ref, wdt_ref, o_ref, *, tiles):
    xc = xc_ref[0, 0]
    g = _dot(xc, wg_ref[0])
    u = _dot(xc, wu_ref[0])
    h = (g / (1.0 + jnp.exp(-g)) * u).astype(BF16)
    yt = _dot_nt(wdt_ref[0], h)
    for r in range(tiles):
        o_ref[0, 0, r] = yt[:, r * MOE_CHUNK:(r + 1) * MOE_CHUNK].astype(BF16)


def _ffn_call(xc, wg, wu, wdt):
    b, e, cap, d = xc.shape
    ff = wg.shape[-1]
    rt = min(cap, 1024)
    tiles = rt // MOE_CHUNK
    return pl.pallas_call(
        functools.partial(_ffn_kernel, tiles=tiles),
        grid=(e, b, cap // rt),
        in_specs=[pl.BlockSpec((1, 1, rt, d), lambda ei, bi, ri: (bi, ei, ri, 0)),
                  pl.BlockSpec((1, d, ff), lambda ei, bi, ri: (ei, 0, 0)),
                  pl.BlockSpec((1, d, ff), lambda ei, bi, ri: (ei, 0, 0)),
                  pl.BlockSpec((1, d, ff), lambda ei, bi, ri: (ei, 0, 0))],
        out_specs=pl.BlockSpec((1, 1, tiles, d, MOE_CHUNK), lambda ei, bi, ri: (bi, ei, ri, 0, 0)),
        out_shape=jax.ShapeDtypeStruct((b, e, cap // MOE_CHUNK, d, MOE_CHUNK), BF16),
        compiler_params=pltpu.CompilerParams(
            dimension_semantics=("arbitrary", "arbitrary", "arbitrary"), vmem_limit_bytes=VMEM_LIMIT),
        name="moe_ffn",
    )(xc, wg, wu, wdt)


def _scatter_kernel(coff_ref, y_ref, slot_ref, gate_ref, o_ref, *, n_chunks, tiles):
    bi = pl.program_id(0)
    kg = pl.program_id(2)
    for j in range(SCATTER_CHUNKS):
        k = kg * SCATTER_CHUNKS + j
        acc = jnp.zeros((MOE_COLS, MOE_CHUNK), F32)
        for e in range(N_EXPERTS):
            r0 = _window_tile(coff_ref[(bi * N_EXPERTS + e) * (n_chunks + 1) + k], tiles)
            r1 = jnp.minimum(r0 + 1, tiles - 1)
            g = _window_onehot(slot_ref[0, e, j], r0 * MOE_CHUNK)
            yw = jnp.concatenate([y_ref[0, e, r0], y_ref[0, e, r1]], axis=1)
            acc = acc + _dot(yw, g) * gate_ref[0, e, j]
        o_ref[0, 0, j] = acc


def _scatter_call(coff, y5, slot5, gate5):
    b, e, tiles, d, _ = y5.shape
    n_chunks = slot5.shape[2]
    ncq = d // MOE_COLS
    small = pl.BlockSpec((1, e, SCATTER_CHUNKS, 1, MOE_CHUNK), lambda bi, cq, kg, co: (bi, 0, kg, 0, 0))
    grid_spec = pltpu.PrefetchScalarGridSpec(
        num_scalar_prefetch=1,
        grid=(b, ncq, n_chunks // SCATTER_CHUNKS),
        in_specs=[pl.BlockSpec((1, e, tiles, MOE_COLS, MOE_CHUNK), lambda bi, cq, kg, co: (bi, 0, 0, cq, 0)),
                  small, small],
        out_specs=pl.BlockSpec((1, 1, SCATTER_CHUNKS, MOE_COLS, MOE_CHUNK),
                               lambda bi, cq, kg, co: (bi, cq, kg, 0, 0)),
    )
    return pl.pallas_call(
        functools.partial(_scatter_kernel, n_chunks=n_chunks, tiles=tiles),
        grid_spec=grid_spec,
        out_shape=jax.ShapeDtypeStruct((b, ncq, n_chunks, MOE_COLS, MOE_CHUNK), F32),
        compiler_params=pltpu.CompilerParams(
            dimension_semantics=("arbitrary", "arbitrary", "arbitrary"), vmem_limit_bytes=VMEM_LIMIT),
        name="moe_scatter",
    )(coff, y5, slot5, gate5)


def _final_kernel(x_ref, moe_ref, o_ref):
    o_ref[0] = x_ref[0] + _moe_tile_to_rows(moe_ref, TOK_TILE // MOE_CHUNK)


def _final_call(x1, moe):
    b, s, d = x1.shape
    ncq = d // MOE_COLS
    tok = pl.BlockSpec((1, TOK_TILE, d), lambda bi, ti: (bi, ti, 0))
    return pl.pallas_call(
        _final_kernel,
        grid=(b, s // TOK_TILE),
        in_specs=[tok, pl.BlockSpec((1, ncq, TOK_TILE // MOE_CHUNK, MOE_COLS, MOE_CHUNK),
                                    lambda bi, ti: (bi, 0, ti, 0, 0))],
        out_specs=tok,
        out_shape=jax.ShapeDtypeStruct((b, s, d), F32),
        compiler_params=pltpu.CompilerParams(
            dimension_semantics=("arbitrary", "arbitrary"), vmem_limit_bytes=VMEM_LIMIT),
        name="final_add",
    )(x1, moe)


def _block_ones(n, group):
    i = np.arange(n)
    return jnp.asarray((i[:, None] // group) == (i[None, :] // group), BF16)


def _tables(s):
    half = MLA_ROPE // 2
    inv = ROPE_THETA ** (-jnp.arange(half, dtype=F32) / half)
    ang = jnp.arange(s, dtype=jnp.int32).astype(F32)[:, None] * inv[None, :]
    cos, sin = jnp.cos(ang), jnp.sin(ang)
    ones = jnp.ones((s, MLA_NOPE), F32)
    tail = HEAD_PAD - MLA_QK
    cos_t = jnp.concatenate([ones, cos, cos, jnp.ones((s, tail), F32)], axis=1)
    sin_t = jnp.concatenate([0 * ones, -sin, sin, jnp.zeros((s, tail), F32)], axis=1)
    i = np.arange(LANES)
    tri = np.stack([i[:, None] <= i[None, :], i[None, :] < i[:, None], i[:, None] < i[None, :]])
    return {'cos': cos_t, 'sin': sin_t, 'e384': _block_ones(A_WIDTH, HEAD_DIM),
            'e768': _block_ones(QB_WIDTH, HEAD_PAD), 'tri': jnp.asarray(tri, BF16)}


def _pad_heads(v, scale=1.0):
    g = jnp.pad(v.astype(F32) * scale, (0, HEAD_PAD - MLA_QK))
    return jnp.tile(g, MLA_HEADS)[None, :]


def _layer_weights(l, p):
    w_in = p['w_in'][l]
    kr_end = 3 * A_WIDTH + MLA_Q_RANK + MLA_KV_RANK + MLA_ROPE
    w_in_r = jnp.concatenate(
        [w_in[:, :kr_end], jnp.zeros((D_MODEL, LANES - MLA_ROPE), F32), w_in[:, kr_end:]], axis=1)
    w_uq = p['w_uq'][l].reshape(MLA_Q_RANK, MLA_HEADS, MLA_QK)
    w_uq = jnp.pad(w_uq, ((0, 0), (0, 0), (0, HEAD_PAD - MLA_QK))).reshape(MLA_Q_RANK, QB_WIDTH)
    w_ukv = p['w_ukv'][l].reshape(MLA_KV_RANK, MLA_HEADS, MLA_NOPE + MLA_V)
    w_uk = jnp.pad(w_ukv[:, :, :MLA_NOPE], ((0, 0), (0, 0), (0, HEAD_PAD - MLA_NOPE)))
    w_uk = w_uk.reshape(MLA_KV_RANK, QB_WIDTH)
    w_uv = w_ukv[:, :, MLA_NOPE:].reshape(MLA_KV_RANK, B_WIDTH)
    place = np.zeros((LANES, QB_WIDTH), np.float32)
    for h in range(MLA_HEADS):
        place[np.arange(MLA_ROPE), h * HEAD_PAD + MLA_NOPE + np.arange(MLA_ROPE)] = 1.0
    w_kk = jnp.concatenate([w_uk, jnp.asarray(place)], axis=0)
    out_norm = p['out_norm'][l]
    w_out = p['w_out'][l]
    w_r = jnp.pad(p['w_router'][l], ((0, 0), (0, LANES - N_EXPERTS)))
    wr_hi = w_r.astype(BF16)
    b_r = jnp.concatenate([p['b_router'][l].astype(F32), jnp.full((LANES - N_EXPERTS,), NEG_BIG, F32)])
    return {
        'nmix': p['norm_mix'][l][None, :], 'w_in': w_in_r.astype(BF16),
        'gqa': jnp.tile(p['q_norm_a'][l] * HEAD_DIM ** -0.5, NA_HEADS)[None, :],
        'gka': jnp.tile(p['k_norm_a'][l], NA_HEADS)[None, :],
        'cqn': p['cq_norm'][l][None, :], 'w_uq': w_uq.astype(BF16),
        'ckvn': p['ckv_norm'][l][None, :], 'w_kk': w_kk.astype(BF16), 'w_uv': w_uv.astype(BF16),
        'gqb': _pad_heads(p['q_norm_b'][l], MLA_QK ** -0.5 * LOG2_E), 'gkb': _pad_heads(p['k_norm_b'][l]),
        'ga': out_norm[None, :A_WIDTH], 'gb_col': out_norm[A_WIDTH:A_WIDTH + B_WIDTH, None],
        'gc': out_norm[None, A_WIDTH + B_WIDTH:],
        'wo_a': w_out[:A_WIDTH].astype(BF16), 'wo_b': w_out[A_WIDTH:A_WIDTH + B_WIDTH].astype(BF16),
        'wo_c': w_out[A_WIDTH + B_WIDTH:].astype(BF16),
        'conv_w': p['conv_w'][l], 'nffn': p['norm_ffn'][l][None, :],
        'wr_hi': wr_hi, 'wr_lo': (w_r - wr_hi.astype(F32)).astype(BF16), 'b_r': b_r[None, :],
        'w_gate': p['w_gate'][l].astype(BF16), 'w_up': p['w_up'][l].astype(BF16),
        'w_down_t': jnp.swapaxes(p['w_down'][l], 1, 2).astype(BF16),
    }


def _layer(x, moe, lw, tabs, rpb_l):
    b, s, d = x.shape
    cap = CAPACITY_FACTOR * s // N_EXPERTS
    n_chunks = s // MOE_CHUNK

    outs = _proj_call(x, moe, lw, tabs)
    if moe is not None:
        qa, ka, va, qb, kb, vbt, u, bc, x = outs
    else:
        qa, ka, va, qb, kb, vbt, u, bc = outs
    out_a = _na_call(qa, ka, va, _bias_call(rpb_l))
    out_bt = _mla_call(qb, kb, vbt)
    x1, xn, aff_t = _out_call(x, out_a, out_bt, u, bc, lw)

    slot, gate, off = _topk_call(aff_t, tabs['tri'], cap)
    rows_per_chunk = MOE_CHUNK // LANES
    coff = off[:, :, 0, :s // LANES:rows_per_chunk].astype(jnp.int32)
    coff = jnp.concatenate([coff, jnp.full((b, N_EXPERTS, 1), cap, jnp.int32)], axis=-1).reshape(-1)
    slot5 = slot.reshape(b, N_EXPERTS, -1)[:, :, :s].reshape(b, N_EXPERTS, n_chunks, 1, MOE_CHUNK)
    gate5 = gate.reshape(b, N_EXPERTS, -1)[:, :, :s].reshape(b, N_EXPERTS, n_chunks, 1, MOE_CHUNK)

    xc = _gather_call(coff, xn, slot5, cap)
    y5 = _ffn_call(xc, lw['w_gate'], lw['w_up'], lw['w_down_t'])
    moe_out = _scatter_call(coff, y5, slot5, gate5)
    return x1, moe_out


def kernel(x, norm_mix, w_in, q_norm_a, k_norm_a, rpb, cq_norm, w_uq, ckv_norm, w_ukv, q_norm_b,
           k_norm_b, conv_w, out_norm, w_out, norm_ffn, w_router, b_router, w_gate, w_up, w_down):
    p = dict(norm_mix=norm_mix, w_in=w_in, q_norm_a=q_norm_a, k_norm_a=k_norm_a, cq_norm=cq_norm,
             w_uq=w_uq, ckv_norm=ckv_norm, w_ukv=w_ukv, q_norm_b=q_norm_b, k_norm_b=k_norm_b,
             conv_w=conv_w, out_norm=out_norm, w_out=w_out, norm_ffn=norm_ffn, w_router=w_router,
             b_router=b_router, w_gate=w_gate, w_up=w_up, w_down=w_down)
    b, s, d = x.shape
    assert d == D_MODEL and (CAPACITY_FACTOR * s // N_EXPERTS) % MOE_CHUNK == 0
    assert s % TOK_TILE == 0 and (s // GRID_W) % NA_ROWS == 0 and s <= TOPK_ROWS * LANES
    tabs = _tables(s)
    moe = None
    for l in range(norm_mix.shape[0]):
        x, moe = _layer(x, moe, _layer_weights(l, p), tabs, rpb[l])
    return _final_call(x, moe)
```

```python
import functools

import numpy as np
import jax
import jax.numpy as jnp
from jax import lax
from jax.experimental import pallas as pl
from jax.experimental.pallas import tpu as pltpu

D_MODEL = 1024
GRID_W = 64
HEAD_DIM = 64
EPS = 1e-6

NA_HEADS = 6
NA_WIN_H = 8
NA_WIN_W = 16
A_WIDTH = NA_HEADS * HEAD_DIM

MLA_HEADS = 6
MLA_Q_RANK = 256
MLA_KV_RANK = 128
MLA_NOPE = 64
MLA_ROPE = 32
MLA_QK = MLA_NOPE + MLA_ROPE
MLA_V = 64
B_WIDTH = MLA_HEADS * MLA_V
ROPE_THETA = 10000.0

CONV_CH = 256
C_WIDTH = CONV_CH

N_EXPERTS = 16
EXPERT_FF = 1024
CAPACITY_FACTOR = 2

LANES = 128
HEAD_PAD = 128
QB_WIDTH = MLA_HEADS * HEAD_PAD
IN_COLS_PAD = 2432
TOK_TILE = 512
NA_ROWS = 8
MOE_CHUNK = 256
MOE_COLS = 256
GATHER_COLS = 512
SCATTER_CHUNKS = 2
TOPK_ROWS = 128
MLA_VROWS = MLA_V + 16
LOG2_E = 1.4426950408889634
MLA_UNROLL = 8
MLA_SHIFT_LIMIT = 60.0
NEG_BIG = -1e30
VMEM_LIMIT = 56 * 1024 * 1024

F32 = jnp.float32
BF16 = jnp.bfloat16


def _dot(a, b):
    return jnp.dot(a, b, preferred_element_type=F32)


def _dot_nt(a, b):
    return lax.dot_general(a, b, (((1,), (1,)), ((), ())), preferred_element_type=F32)


def _rms(x, g):
    return x * lax.rsqrt(jnp.mean(x * x, axis=-1, keepdims=True) + EPS) * g


def _group_sumsq(y, e_ref):
    sq = y * y
    hi = sq.astype(BF16)
    lo = (sq - hi.astype(F32)).astype(BF16)
    e = e_ref[...]
    return _dot(hi, e) + _dot(lo, e)


def _moe_tile_to_rows(moe_ref, n_chunks):
    rows = []
    for ch in range(n_chunks):
        cols = [moe_ref[0, cq, ch].T for cq in range(D_MODEL // MOE_COLS)]
        rows.append(jnp.concatenate(cols, axis=1))
    return jnp.concatenate(rows, axis=0)


def _proj_kernel(*refs, has_moe):
    if has_moe:
        x_ref, moe_ref = refs[0], refs[1]
        refs = refs[2:]
    else:
        x_ref = refs[0]
        refs = refs[1:]
    (nmix_ref, win_ref, gqa_ref, gka_ref, e384_ref, cqn_ref, wuq_ref, ckvn_ref, wkk_ref,
     wuv_ref, gqb_ref, gkb_ref, e768_ref, cos_ref, sin_ref) = refs[:15]
    outs = refs[15:]
    if has_moe:
        qa_o, ka_o, va_o, qb_o, kb_o, vbt_o, u_o, bc_o, xs_o = outs
    else:
        qa_o, ka_o, va_o, qb_o, kb_o, vbt_o, u_o, bc_o = outs

    x = x_ref[0]
    if has_moe:
        x = x + _moe_tile_to_rows(moe_ref, TOK_TILE // MOE_CHUNK)
        xs_o[0] = x
    xb = _rms(x, nmix_ref[...]).astype(BF16)

    pa = _dot(xb, win_ref[:, 0:3 * A_WIDTH])
    qa = pa[:, 0:A_WIDTH]
    ka = pa[:, A_WIDTH:2 * A_WIDTH]
    qa_o[0] = (qa * lax.rsqrt(_group_sumsq(qa, e384_ref) * (1.0 / HEAD_DIM) + EPS) * gqa_ref[...]).astype(BF16)
    ka_o[0] = (ka * lax.rsqrt(_group_sumsq(ka, e384_ref) * (1.0 / HEAD_DIM) + EPS) * gka_ref[...]).astype(BF16)
    va_o[0] = pa[:, 2 * A_WIDTH:3 * A_WIDTH].astype(BF16)

    c0 = 3 * A_WIDTH
    pb = _dot(xb, win_ref[:, c0:c0 + 512])
    cq = _rms(pb[:, 0:MLA_Q_RANK], cqn_ref[...]).astype(BF16)
    ckv = _rms(pb[:, MLA_Q_RANK:MLA_Q_RANK + MLA_KV_RANK], ckvn_ref[...]).astype(BF16)
    kr = pb[:, MLA_Q_RANK + MLA_KV_RANK:512].astype(BF16)
    qb_raw = _dot(cq, wuq_ref[...])
    kb_raw = _dot(jnp.concatenate([ckv, kr], axis=1), wkk_ref[...])
    vb = _dot(ckv, wuv_ref[...])

    cos = jnp.concatenate([cos_ref[...]] * MLA_HEADS, axis=1)
    sin = jnp.concatenate([sin_ref[...]] * MLA_HEADS, axis=1)
    lane = lax.broadcasted_iota(jnp.int32, (TOK_TILE, QB_WIDTH), 1) & (HEAD_PAD - 1)
    first_half = (lane >= MLA_NOPE) & (lane < MLA_NOPE + MLA_ROPE // 2)

    def norm_rope(y, g_ref):
        y = y * lax.rsqrt(_group_sumsq(y, e768_ref) * (1.0 / MLA_QK) + EPS) * g_ref[...]
        rot = jnp.where(first_half,
                        pltpu.roll(y, QB_WIDTH - MLA_ROPE // 2, 1),
                        pltpu.roll(y, MLA_ROPE // 2, 1))
        return (y * cos + rot * sin).astype(BF16)

    qb_o[0] = norm_rope(qb_raw, gqb_ref)
    kb_o[0] = norm_rope(kb_raw, gkb_ref)
    vbt = vb.T
    for h in range(MLA_HEADS):
        vbt_o[0, h, 0, 0:MLA_V, :] = vbt[h * MLA_V:(h + 1) * MLA_V, :].astype(BF16)
        vbt_o[0, h, 0, MLA_V:MLA_VROWS, :] = jnp.ones((MLA_VROWS - MLA_V, TOK_TILE), BF16)

    c1 = c0 + 512
    pc = _dot(xb, win_ref[:, c1:c1 + 3 * CONV_CH])
    u_o[0] = pc[:, 2 * CONV_CH:3 * CONV_CH] * pc[:, 0:CONV_CH]
    bc_o[0] = pc[:, CONV_CH:2 * CONV_CH]


def _proj_call(x, moe, lw, tabs):
    b, s, d = x.shape
    nt = s // TOK_TILE
    has_moe = moe is not None

    def const(shape):
        return pl.BlockSpec(shape, lambda bi, ti: (0,) * len(shape))

    in_specs = [pl.BlockSpec((1, TOK_TILE, d), lambda bi, ti: (bi, ti, 0))]
    args = [x]
    if has_moe:
        ncq = D_MODEL // MOE_COLS
        in_specs.append(pl.BlockSpec((1, ncq, TOK_TILE // MOE_CHUNK, MOE_COLS, MOE_CHUNK),
                                     lambda bi, ti: (bi, 0, ti, 0, 0)))
        args.append(moe)
    consts = [lw['nmix'], lw['w_in'], lw['gqa'], lw['gka'], tabs['e384'], lw['cqn'], lw['w_uq'],
              lw['ckvn'], lw['w_kk'], lw['w_uv'], lw['gqb'], lw['gkb'], tabs['e768']]
    for c in consts:
        in_specs.append(const(c.shape))
        args.append(c)
    in_specs += [pl.BlockSpec((TOK_TILE, HEAD_PAD), lambda bi, ti: (ti, 0))] * 2
    args += [tabs['cos'], tabs['sin']]

    def tok(width, dtype):
        return (jax.ShapeDtypeStruct((b, s, width), dtype),
                pl.BlockSpec((1, TOK_TILE, width), lambda bi, ti: (bi, ti, 0)))

    outs = [tok(A_WIDTH, BF16), tok(A_WIDTH, BF16), tok(A_WIDTH, BF16),
            tok(QB_WIDTH, BF16), tok(QB_WIDTH, BF16),
            (jax.ShapeDtypeStruct((b, MLA_HEADS, nt, MLA_VROWS, TOK_TILE), BF16),
             pl.BlockSpec((1, MLA_HEADS, 1, MLA_VROWS, TOK_TILE), lambda bi, ti: (bi, 0, ti, 0, 0))),
            tok(CONV_CH, F32), tok(CONV_CH, F32)]
    if has_moe:
        outs.append(tok(d, F32))
    return pl.pallas_call(
        functools.partial(_proj_kernel, has_moe=has_moe),
        grid=(b, nt),
        in_specs=in_specs,
        out_specs=[o[1] for o in outs],
        out_shape=[o[0] for o in outs],
        compiler_params=pltpu.CompilerParams(
            dimension_semantics=("arbitrary", "arbitrary"), vmem_limit_bytes=VMEM_LIMIT),
        name="proj_moe" if has_moe else "proj",
    )(*args)


def _bias_kernel(rpb_ref, o_ref):
    h = pl.program_id(0)
    delta = pl.program_id(1)
    nw = 2 * NA_WIN_W - 1
    shape = (GRID_W, NA_WIN_H * GRID_W)
    c = lax.broadcasted_iota(jnp.int32, shape, 0)
    col = lax.broadcasted_iota(jnp.int32, shape, 1)
    wrow = lax.shift_right_logical(col, 6)
    kc = col & (GRID_W - 1)
    c_lo = jnp.clip(c - NA_WIN_W // 2, 0, GRID_W - NA_WIN_W)
    inside = (kc >= c_lo) & (kc < c_lo + NA_WIN_W)
    d = kc - c + (NA_WIN_W - 1)
    acc = jnp.zeros(shape, F32)
    base = h * ((2 * NA_WIN_H - 1) * nw)
    for i in range(NA_WIN_H):
        ri = i - delta + (NA_WIN_H - 1)
        for dd in range(nw):
            val = rpb_ref[base + ri * nw + dd]
            acc = jnp.where((wrow == i) & (d == dd), val, acc)
    o_ref[0, 0] = jnp.where(inside, acc, NEG_BIG)


def _bias_call(rpb_l):
    flat = rpb_l.reshape(-1)
    return pl.pallas_call(
        _bias_kernel,
        grid=(NA_HEADS, NA_WIN_H),
        in_specs=[pl.BlockSpec(memory_space=pltpu.SMEM)],
        out_specs=pl.BlockSpec((1, 1, GRID_W, NA_WIN_H * GRID_W), lambda h, dl: (h, dl, 0, 0)),
        out_shape=jax.ShapeDtypeStruct((NA_HEADS, NA_WIN_H, GRID_W, NA_WIN_H * GRID_W), F32),
        name="na_bias",
    )(flat)


def _na_kernel(q_ref, kp_ref, kc_ref, kn_ref, vp_ref, vc_ref, vn_ref, bias_ref, o_ref,
               k_scr, v_scr, *, rows):
    j = pl.program_id(1)
    blk = NA_ROWS * GRID_W
    k_scr[0:blk] = kp_ref[0]
    k_scr[blk:2 * blk] = kc_ref[0]
    k_scr[2 * blk:3 * blk] = kn_ref[0]
    v_scr[0:blk] = vp_ref[0]
    v_scr[blk:2 * blk] = vc_ref[0]
    v_scr[2 * blk:3 * blk] = vn_ref[0]
    lane = lax.broadcasted_iota(jnp.int32, (GRID_W, LANES), 1)
    low = lane < HEAD_DIM
    win = NA_WIN_H * GRID_W

    def window_start(i):
        r = NA_ROWS * j + i
        r0 = jnp.clip(r - NA_WIN_H // 2, 0, rows - NA_WIN_H)
        return r - r0, pl.multiple_of((r0 - NA_ROWS * j + NA_ROWS) * GRID_W, GRID_W)

    tiles = []
    for i in range(NA_ROWS):
        delta, start = window_start(i)
        for p in range(NA_HEADS // 2):
            lsl = slice(p * LANES, (p + 1) * LANES)
            qp = q_ref[0, i * GRID_W:(i + 1) * GRID_W, lsl]
            kw = k_scr[pl.ds(start, win), lsl]
            for a in range(2):
                qm = jnp.where(low if a == 0 else jnp.logical_not(low), qp, jnp.zeros_like(qp))
                tiles.append(_dot_nt(qm, kw) + bias_ref[2 * p + a, delta])
    sc = jnp.concatenate(tiles, axis=0)
    e = jnp.exp(sc - jnp.max(sc, axis=-1, keepdims=True))
    inv = 1.0 / jnp.sum(e, axis=-1, keepdims=True)
    eb = e.astype(BF16)
    t = 0
    for i in range(NA_ROWS):
        _, start = window_start(i)
        for p in range(NA_HEADS // 2):
            lsl = slice(p * LANES, (p + 1) * LANES)
            vw = v_scr[pl.ds(start, win), lsl]
            res = []
            for a in range(2):
                rsl = slice(t * GRID_W, (t + 1) * GRID_W)
                res.append(_dot(eb[rsl], vw) * inv[rsl])
                t += 1
            o_ref[0, i * GRID_W:(i + 1) * GRID_W, lsl] = jnp.where(low, res[0], res[1])


def _na_call(qa, ka, va, bias_tab):
    b, s, w = qa.shape
    rows = s // GRID_W
    nb = rows // NA_ROWS
    blk = NA_ROWS * GRID_W
    cur = pl.BlockSpec((1, blk, w), lambda bi, j: (bi, j, 0))
    prev = pl.BlockSpec((1, blk, w), lambda bi, j: (bi, jnp.maximum(j - 1, 0), 0))
    nxt = pl.BlockSpec((1, blk, w), lambda bi, j: (bi, jnp.minimum(j + 1, nb - 1), 0))
    return pl.pallas_call(
        functools.partial(_na_kernel, rows=rows),
        grid=(b, nb),
        in_specs=[cur, prev, cur, nxt, prev, cur, nxt,
                  pl.BlockSpec(bias_tab.shape, lambda bi, j: (0, 0, 0, 0))],
        out_specs=pl.BlockSpec((1, blk, w), lambda bi, j: (bi, j, 0)),
        out_shape=jax.ShapeDtypeStruct((b, s, w), F32),
        scratch_shapes=[pltpu.VMEM((3 * blk, w), BF16), pltpu.VMEM((3 * blk, w), BF16)],
        compiler_params=pltpu.CompilerParams(
            dimension_semantics=("arbitrary", "arbitrary"), vmem_limit_bytes=VMEM_LIMIT),
        name="na_attn",
    )(qa, ka, ka, ka, va, va, va, bias_tab)


def _mla_kernel(bound_ref, q_ref, k_ref, vt_ref, o_ref, s_scr, acc_scr, m_scr, *, n_chunks):
    q = q_ref[0]
    bound = bound_ref[0]

    def scores(c):
        kb = k_ref[0, pl.ds(pl.multiple_of(c * TOK_TILE, TOK_TILE), TOK_TILE), :]
        return _dot_nt(kb, q)

    @pl.when(bound <= MLA_SHIFT_LIMIT)
    def _():
        acc_scr[...] = jnp.zeros(acc_scr.shape, F32)
        s_scr[0] = scores(0)

        def body(i, carry):
            st = s_scr[0]
            part = None
            for j in range(MLA_UNROLL):
                c = MLA_UNROLL * i + j
                nxt = scores(jnp.minimum(c + 1, n_chunks - 1))
                d = _dot(vt_ref[0, 0, c], jnp.exp2(st - bound).astype(BF16))
                part = d if part is None else part + d
                st = nxt
            s_scr[0] = st
            acc_scr[...] += part
            return carry

        lax.fori_loop(0, n_chunks // MLA_UNROLL, body, 0)

    @pl.when(bound > MLA_SHIFT_LIMIT)
    def _():
        def update(st, c):
            m = m_scr[...]
            m_new = jnp.maximum(m, jnp.max(st, axis=0, keepdims=True))
            alpha = jnp.exp2(m - m_new)
            p = jnp.exp2(st - m_new).astype(BF16)
            acc_scr[...] = alpha * acc_scr[...] + _dot(vt_ref[0, 0, c], p)
            m_scr[...] = m_new

        m_scr[...] = jnp.full(m_scr.shape, NEG_BIG, F32)
        acc_scr[...] = jnp.zeros(acc_scr.shape, F32)
        s_scr[0] = scores(0)

        def body(i, carry):
            for j in range(2):
                c = 2 * i + j
                s_scr[(j + 1) % 2] = scores(jnp.minimum(c + 1, n_chunks - 1))
                update(s_scr[j % 2], c)
            return carry

        lax.fori_loop(0, n_chunks // 2, body, 0)

    acc = acc_scr[...]
    o_ref[0] = acc[0:MLA_V, :] / acc[MLA_V:MLA_V + 1, :]


def _mla_call(bound, qb, kb, vbt):
    b, s, _ = qb.shape
    nt = s // TOK_TILE
    assert nt % MLA_UNROLL == 0 and nt % 2 == 0
    return pl.pallas_call(
        functools.partial(_mla_kernel, n_chunks=nt),
        grid=(b, MLA_HEADS, nt),
        in_specs=[pl.BlockSpec(memory_space=pltpu.SMEM),
                  pl.BlockSpec((1, TOK_TILE, HEAD_PAD), lambda bi, h, qi: (bi, qi, h)),
                  pl.BlockSpec((1, s, HEAD_PAD), lambda bi, h, qi: (bi, 0, h)),
                  pl.BlockSpec((1, 1, nt, MLA_VROWS, TOK_TILE), lambda bi, h, qi: (bi, h, 0, 0, 0))],
        out_specs=pl.BlockSpec((1, MLA_V, TOK_TILE), lambda bi, h, qi: (bi, h, qi)),
        out_shape=jax.ShapeDtypeStruct((b, B_WIDTH, s), F32),
        scratch_shapes=[pltpu.VMEM((2, TOK_TILE, TOK_TILE), F32), pltpu.VMEM((MLA_VROWS, TOK_TILE), F32),
                        pltpu.VMEM((1, TOK_TILE), F32)],
        compiler_params=pltpu.CompilerParams(
            dimension_semantics=("arbitrary", "arbitrary", "arbitrary"), vmem_limit_bytes=VMEM_LIMIT),
        name="mla_attn",
    )(bound, qb, kb, vbt)


def _out_kernel(x_ref, oa_ref, obt_ref, u_ref, up_ref, un_ref, bc_ref, ga_ref, gb_ref, gc_ref,
                woa_ref, wob_ref, woc_ref, cw_ref, nffn_ref, wrh_ref, wrl_ref, br_ref,
                x1_o, xn_o, aff_o):
    ti = pl.program_id(1)
    nt = pl.num_programs(1)
    t = TOK_TILE

    na = _rms(oa_ref[0], ga_ref[...]).astype(BF16)
    acc = _dot(na, woa_ref[...])

    obt = obt_ref[0]
    nbt = obt * lax.rsqrt(jnp.mean(obt * obt, axis=0, keepdims=True) + EPS) * gb_ref[...]
    acc = acc + _dot(nbt.T.astype(BF16), wob_ref[...])

    u = u_ref[0]
    row = lax.broadcasted_iota(jnp.int32, (t, CONV_CH), 0)
    prev_row = jnp.where(ti > 0, up_ref[0, 7:8, :], 0.0)
    next_row = jnp.where(ti < nt - 1, un_ref[0, 0:1, :], 0.0)
    u_m1 = jnp.where(row == 0, prev_row, pltpu.roll(u, 1, 0))
    u_p1 = jnp.where(row == t - 1, next_row, pltpu.roll(u, t - 1, 0))
    y = cw_ref[0:1, :] * u_m1 + cw_ref[1:2, :] * u + cw_ref[2:3, :] * u_p1
    oc = _rms(bc_ref[0] * y, gc_ref[...]).astype(BF16)
    acc = acc + _dot(oc, woc_ref[...])

    x1 = x_ref[0] + acc
    x1_o[0] = x1
    xn = _rms(x1, nffn_ref[...])
    hi = xn.astype(BF16)
    lo = (xn - hi.astype(F32)).astype(BF16)
    xn_o[0] = hi
    logits = _dot(hi, wrh_ref[...]) + _dot(lo, wrh_ref[...]) + _dot(hi, wrl_ref[...]) + br_ref[...]
    m = jnp.max(logits, axis=-1, keepdims=True)
    e = jnp.exp(logits - m)
    aff = e / jnp.sum(e, axis=-1, keepdims=True)
    aff_o[0] = aff.T[0:N_EXPERTS, :]


def _out_call(x, out_a, out_bt, u, bc, lw):
    b, s, d = x.shape
    nt = s // TOK_TILE
    h8 = TOK_TILE // 8

    def const(a):
        return pl.BlockSpec(a.shape, lambda bi, ti: (0,) * a.ndim)

    def tok(width):
        return pl.BlockSpec((1, TOK_TILE, width), lambda bi, ti: (bi, ti, 0))

    consts = [lw['ga'], lw['gb_col'], lw['gc'], lw['wo_a'], lw['wo_b'], lw['wo_c'], lw['conv_w'],
              lw['nffn'], lw['wr_hi'], lw['wr_lo'], lw['b_r']]
    in_specs = [tok(d), tok(A_WIDTH),
                pl.BlockSpec((1, B_WIDTH, TOK_TILE), lambda bi, ti: (bi, 0, ti)),
                tok(CONV_CH),
                pl.BlockSpec((1, 8, CONV_CH), lambda bi, ti: (bi, jnp.maximum(ti * h8 - 1, 0), 0)),
                pl.BlockSpec((1, 8, CONV_CH), lambda bi, ti: (bi, jnp.minimum((ti + 1) * h8, s // 8 - 1), 0)),
                tok(CONV_CH)] + [const(c) for c in consts]
    return pl.pallas_call(
        _out_kernel,
        grid=(b, nt),
        in_specs=in_specs,
        out_specs=[tok(d), tok(d), pl.BlockSpec((1, N_EXPERTS, TOK_TILE), lambda bi, ti: (bi, 0, ti))],
        out_shape=[jax.ShapeDtypeStruct((b, s, d), F32), jax.ShapeDtypeStruct((b, s, d), BF16),
                   jax.ShapeDtypeStruct((b, N_EXPERTS, s), F32)],
        compiler_params=pltpu.CompilerParams(
            dimension_semantics=("arbitrary", "arbitrary"), vmem_limit_bytes=VMEM_LIMIT),
        name="out_proj",
    )(x, out_a, out_bt, u, u, u, bc, *consts)


def _topk_kernel(a_ref, tri_ref, o_slot, o_gate, o_off, *, cap):
    a = a_ref[0, 0]
    bits = pltpu.bitcast(a, jnp.int32)

    def total(mask):
        v = jnp.where(mask, 1.0, 0.0)
        return jnp.sum(jnp.sum(v, axis=0, keepdims=True), axis=1, keepdims=True)

    thr = jnp.zeros((1, 1), jnp.int32)
    for bit in range(30, -1, -1):
        cand = thr | (1 << bit)
        thr = jnp.where(total(bits >= cand) >= cap, cand, thr)
    gt = bits > thr
    eq = bits == thr
    need = cap - total(gt)

    upper = tri_ref[0]
    lower_strict = tri_ref[1]
    upper_strict = tri_ref[2]

    def excl_rank(mask):
        f = jnp.where(mask, 1.0, 0.0)
        fb = f.astype(BF16)
        in_row = _dot(fb, upper)
        before_rows = jnp.sum(_dot(lower_strict, fb), axis=1, keepdims=True)
        return before_rows + in_row - f, fb

    eq_rank, _ = excl_rank(eq)
    sel = gt | (eq & (eq_rank < need))
    pos, sel_b = excl_rank(sel)
    o_slot[0, 0] = jnp.where(sel, pos, -1.0).astype(jnp.int32)
    o_gate[0, 0] = jnp.where(sel, a, 0.0)
    row_tot = _dot_nt(jnp.ones((8, LANES), BF16), sel_b)
    o_off[0, 0] = _dot(row_tot.astype(BF16), upper_strict)


def _topk_call(aff_t, tri, cap):
    b, e, s = aff_t.shape
    pad = TOPK_ROWS * LANES - s
    a = aff_t if pad == 0 else jnp.pad(aff_t, ((0, 0), (0, 0), (0, pad)), constant_values=-1.0)
    a = a.reshape(b, e, TOPK_ROWS, LANES)
    blk = pl.BlockSpec((1, 1, TOPK_ROWS, LANES), lambda bi, ei: (bi, ei, 0, 0))
    return pl.pallas_call(
        functools.partial(_topk_kernel, cap=cap),
        grid=(b, e),
        in_specs=[blk, pl.BlockSpec(tri.shape, lambda bi, ei: (0, 0, 0))],
        out_specs=[blk, blk, pl.BlockSpec((1, 1, 8, LANES), lambda bi, ei: (bi, ei, 0, 0))],
        out_shape=[jax.ShapeDtypeStruct((b, e, TOPK_ROWS, LANES), jnp.int32),
                   jax.ShapeDtypeStruct((b, e, TOPK_ROWS, LANES), F32),
                   jax.ShapeDtypeStruct((b, e, 8, LANES), F32)],
        name="topk_select",
    )(a, tri)


def _window_onehot(slot_row, start):
    i = lax.broadcasted_iota(jnp.int32, (2 * MOE_CHUNK, MOE_CHUNK), 0)
    return jnp.where(i == slot_row - start, 1.0, 0.0).astype(BF16)


def _window_tile(off, tiles):
    return jnp.minimum(lax.shift_right_logical(off, 8), tiles - 1)


def _gather_kernel(coff_ref, x_ref, slot_ref, o_ref, acc_ref, *, n_chunks, cap):
    bi = pl.program_id(0)
    ei = pl.program_id(2)
    base = (bi * N_EXPERTS + ei) * (n_chunks + 1)
    acc_ref[...] = jnp.zeros_like(acc_ref)

    def body(k, carry):
        start = pl.multiple_of(_window_tile(coff_ref[base + k], cap // MOE_CHUNK) * MOE_CHUNK, MOE_CHUNK)
        g = _window_onehot(slot_ref[0, 0, k], start)
        xk = x_ref[0, pl.ds(pl.multiple_of(k * MOE_CHUNK, MOE_CHUNK), MOE_CHUNK), :]
        acc_ref[pl.ds(start, 2 * MOE_CHUNK), :] += _dot(g, xk)
        return carry

    lax.fori_loop(0, n_chunks, body, 0, unroll=2)
    o_ref[0, 0] = acc_ref[0:cap, :].astype(BF16)


def _gather_call(coff, xn, slot5, cap):
    b, s, d = xn.shape
    n_chunks = s // MOE_CHUNK
    ncq = d // GATHER_COLS
    grid_spec = pltpu.PrefetchScalarGridSpec(
        num_scalar_prefetch=1,
        grid=(b, ncq, N_EXPERTS),
        in_specs=[pl.BlockSpec((1, s, GATHER_COLS), lambda bi, cq, ei, co: (bi, 0, cq)),
                  pl.BlockSpec((1, 1, n_chunks, 1, MOE_CHUNK), lambda bi, cq, ei, co: (bi, ei, 0, 0, 0))],
        out_specs=pl.BlockSpec((1, 1, cap, GATHER_COLS), lambda bi, cq, ei, co: (bi, ei, 0, cq)),
        scratch_shapes=[pltpu.VMEM((cap + MOE_CHUNK, GATHER_COLS), F32)],
    )
    return pl.pallas_call(
        functools.partial(_gather_kernel, n_chunks=n_chunks, cap=cap),
        grid_spec=grid_spec,
        out_shape=jax.ShapeDtypeStruct((b, N_EXPERTS, cap, d), BF16),
        compiler_params=pltpu.CompilerParams(
            dimension_semantics=("arbitrary", "arbitrary", "arbitrary"), vmem_limit_bytes=VMEM_LIMIT),
        name="moe_gather",
    )(coff, xn, slot5)


def _ffn_kernel(xc_ref, wg_ref, wu_ref, wdt_ref, o_ref, *, tiles):
    xc = xc_ref[0, 0]
    g = _dot(xc, wg_ref[0])
    u = _dot(xc, wu_ref[0])
    h = (g / (1.0 + jnp.exp(-g)) * u).astype(BF16)
    yt = _dot_nt(wdt_ref[0], h)
    for r in range(tiles):
        o_ref[0, 0, r] = yt[:, r * MOE_CHUNK:(r + 1) * MOE_CHUNK].astype(BF16)


def _ffn_call(xc, wg, wu, wdt):
    b, e, cap, d = xc.shape
    ff = wg.shape[-1]
    rt = min(cap, 1024)
    tiles = rt // MOE_CHUNK
    return pl.pallas_call(
        functools.partial(_ffn_kernel, tiles=tiles),
        grid=(e, b, cap // rt),
        in_specs=[pl.BlockSpec((1, 1, rt, d), lambda ei, bi, ri: (bi, ei, ri, 0)),
                  pl.BlockSpec((1, d, ff), lambda ei, bi, ri: (ei, 0, 0)),
                  pl.BlockSpec((1, d, ff), lambda ei, bi, ri: (ei, 0, 0)),
                  pl.BlockSpec((1, d, ff), lambda ei, bi, ri: (ei, 0, 0))],
        out_specs=pl.BlockSpec((1, 1, tiles, d, MOE_CHUNK), lambda ei, bi, ri: (bi, ei, ri, 0, 0)),
        out_shape=jax.ShapeDtypeStruct((b, e, cap // MOE_CHUNK, d, MOE_CHUNK), BF16),
        compiler_params=pltpu.CompilerParams(
            dimension_semantics=("arbitrary", "arbitrary", "arbitrary"), vmem_limit_bytes=VMEM_LIMIT),
        name="moe_ffn",
    )(xc, wg, wu, wdt)


def _scatter_kernel(coff_ref, y_ref, slot_ref, gate_ref, o_ref, *, n_chunks, tiles):
    bi = pl.program_id(0)
    kg = pl.program_id(2)
    for j in range(SCATTER_CHUNKS):
        k = kg * SCATTER_CHUNKS + j
        acc = jnp.zeros((MOE_COLS, MOE_CHUNK), F32)
        for e in range(N_EXPERTS):
            r0 = _window_tile(coff_ref[(bi * N_EXPERTS + e) * (n_chunks + 1) + k], tiles)
            r1 = jnp.minimum(r0 + 1, tiles - 1)
            g = _window_onehot(slot_ref[0, e, j], r0 * MOE_CHUNK)
            yw = jnp.concatenate([y_ref[0, e, r0], y_ref[0, e, r1]], axis=1)
            acc = acc + _dot(yw, g) * gate_ref[0, e, j]
        o_ref[0, 0, j] = acc


def _scatter_call(coff, y5, slot5, gate5):
    b, e, tiles, d, _ = y5.shape
    n_chunks = slot5.shape[2]
    ncq = d // MOE_COLS
    small = pl.BlockSpec((1, e, SCATTER_CHUNKS, 1, MOE_CHUNK), lambda bi, cq, kg, co: (bi, 0, kg, 0, 0))
    grid_spec = pltpu.PrefetchScalarGridSpec(
        num_scalar_prefetch=1,
        grid=(b, ncq, n_chunks // SCATTER_CHUNKS),
        in_specs=[pl.BlockSpec((1, e, tiles, MOE_COLS, MOE_CHUNK), lambda bi, cq, kg, co: (bi, 0, 0, cq, 0)),
                  small, small],
        out_specs=pl.BlockSpec((1, 1, SCATTER_CHUNKS, MOE_COLS, MOE_CHUNK),
                               lambda bi, cq, kg, co: (bi, cq, kg, 0, 0)),
    )
    return pl.pallas_call(
        functools.partial(_scatter_kernel, n_chunks=n_chunks, tiles=tiles),
        grid_spec=grid_spec,
        out_shape=jax.ShapeDtypeStruct((b, ncq, n_chunks, MOE_COLS, MOE_CHUNK), F32),
        compiler_params=pltpu.CompilerParams(
            dimension_semantics=("arbitrary", "arbitrary", "arbitrary"), vmem_limit_bytes=VMEM_LIMIT),
        name="moe_scatter",
    )(coff, y5, slot5, gate5)


def _final_kernel(x_ref, moe_ref, o_ref):
    o_ref[0] = x_ref[0] + _moe_tile_to_rows(moe_ref, TOK_TILE // MOE_CHUNK)


def _final_call(x1, moe):
    b, s, d = x1.shape
    ncq = d // MOE_COLS
    tok = pl.BlockSpec((1, TOK_TILE, d), lambda bi, ti: (bi, ti, 0))
    return pl.pallas_call(
        _final_kernel,
        grid=(b, s // TOK_TILE),
        in_specs=[tok, pl.BlockSpec((1, ncq, TOK_TILE // MOE_CHUNK, MOE_COLS, MOE_CHUNK),
                                    lambda bi, ti: (bi, 0, ti, 0, 0))],
        out_specs=tok,
        out_shape=jax.ShapeDtypeStruct((b, s, d), F32),
        compiler_params=pltpu.CompilerParams(
            dimension_semantics=("arbitrary", "arbitrary"), vmem_limit_bytes=VMEM_LIMIT),
        name="final_add",
    )(x1, moe)


def _block_ones(n, group):
    i = np.arange(n)
    return jnp.asarray((i[:, None] // group) == (i[None, :] // group), BF16)


def _tables(s):
    half = MLA_ROPE // 2
    inv = ROPE_THETA ** (-jnp.arange(half, dtype=F32) / half)
    ang = jnp.arange(s, dtype=jnp.int32).astype(F32)[:, None] * inv[None, :]
    cos, sin = jnp.cos(ang), jnp.sin(ang)
    ones = jnp.ones((s, MLA_NOPE), F32)
    tail = HEAD_PAD - MLA_QK
    cos_t = jnp.concatenate([ones, cos, cos, jnp.ones((s, tail), F32)], axis=1)
    sin_t = jnp.concatenate([0 * ones, -sin, sin, jnp.zeros((s, tail), F32)], axis=1)
    i = np.arange(LANES)
    tri = np.stack([i[:, None] <= i[None, :], i[None, :] < i[:, None], i[:, None] < i[None, :]])
    return {'cos': cos_t, 'sin': sin_t, 'e384': _block_ones(A_WIDTH, HEAD_DIM),
            'e768': _block_ones(QB_WIDTH, HEAD_PAD), 'tri': jnp.asarray(tri, BF16)}


def _pad_heads(v, scale=1.0):
    g = jnp.pad(v.astype(F32) * scale, (0, HEAD_PAD - MLA_QK))
    return jnp.tile(g, MLA_HEADS)[None, :]


def _layer_weights(l, p):
    w_in = p['w_in'][l]
    kr_end = 3 * A_WIDTH + MLA_Q_RANK + MLA_KV_RANK + MLA_ROPE
    w_in_r = jnp.concatenate(
        [w_in[:, :kr_end], jnp.zeros((D_MODEL, LANES - MLA_ROPE), F32), w_in[:, kr_end:]], axis=1)
    w_uq = p['w_uq'][l].reshape(MLA_Q_RANK, MLA_HEADS, MLA_QK)
    w_uq = jnp.pad(w_uq, ((0, 0), (0, 0), (0, HEAD_PAD - MLA_QK))).reshape(MLA_Q_RANK, QB_WIDTH)
    w_ukv = p['w_ukv'][l].reshape(MLA_KV_RANK, MLA_HEADS, MLA_NOPE + MLA_V)
    w_uk = jnp.pad(w_ukv[:, :, :MLA_NOPE], ((0, 0), (0, 0), (0, HEAD_PAD - MLA_NOPE)))
    w_uk = w_uk.reshape(MLA_KV_RANK, QB_WIDTH)
    w_uv = w_ukv[:, :, MLA_NOPE:].reshape(MLA_KV_RANK, B_WIDTH)
    place = np.zeros((LANES, QB_WIDTH), np.float32)
    for h in range(MLA_HEADS):
        place[np.arange(MLA_ROPE), h * HEAD_PAD + MLA_NOPE + np.arange(MLA_ROPE)] = 1.0
    w_kk = jnp.concatenate([w_uk, jnp.asarray(place)], axis=0)
    out_norm = p['out_norm'][l]
    w_out = p['w_out'][l]
    w_r = jnp.pad(p['w_router'][l], ((0, 0), (0, LANES - N_EXPERTS)))
    wr_hi = w_r.astype(BF16)
    b_r = jnp.concatenate([p['b_router'][l].astype(F32), jnp.full((LANES - N_EXPERTS,), NEG_BIG, F32)])
    return {
        'nmix': p['norm_mix'][l][None, :], 'w_in': w_in_r.astype(BF16),
        'gqa': jnp.tile(p['q_norm_a'][l] * HEAD_DIM ** -0.5, NA_HEADS)[None, :],
        'gka': jnp.tile(p['k_norm_a'][l], NA_HEADS)[None, :],
        'cqn': p['cq_norm'][l][None, :], 'w_uq': w_uq.astype(BF16),
        'ckvn': p['ckv_norm'][l][None, :], 'w_kk': w_kk.astype(BF16), 'w_uv': w_uv.astype(BF16),
        'gqb': _pad_heads(p['q_norm_b'][l], MLA_QK ** -0.5 * LOG2_E), 'gkb': _pad_heads(p['k_norm_b'][l]),
        'ga': out_norm[None, :A_WIDTH], 'gb_col': out_norm[A_WIDTH:A_WIDTH + B_WIDTH, None],
        'gc': out_norm[None, A_WIDTH + B_WIDTH:],
        'wo_a': w_out[:A_WIDTH].astype(BF16), 'wo_b': w_out[A_WIDTH:A_WIDTH + B_WIDTH].astype(BF16),
        'wo_c': w_out[A_WIDTH + B_WIDTH:].astype(BF16),
        'conv_w': p['conv_w'][l], 'nffn': p['norm_ffn'][l][None, :],
        'wr_hi': wr_hi, 'wr_lo': (w_r - wr_hi.astype(F32)).astype(BF16), 'b_r': b_r[None, :],
        'w_gate': p['w_gate'][l].astype(BF16), 'w_up': p['w_up'][l].astype(BF16),
        'w_down_t': jnp.swapaxes(p['w_down'][l], 1, 2).astype(BF16),
    }


def _layer(x, moe, lw, tabs, rpb_l):
    b, s, d = x.shape
    cap = CAPACITY_FACTOR * s // N_EXPERTS
    n_chunks = s // MOE_CHUNK

    outs = _proj_call(x, moe, lw, tabs)
    if moe is not None:
        qa, ka, va, qb, kb, vbt, u, bc, x = outs
    else:
        qa, ka, va, qb, kb, vbt, u, bc = outs
    out_a = _na_call(qa, ka, va, _bias_call(rpb_l))
    bound = (1.02 * MLA_QK * jnp.max(jnp.abs(lw['gqb'])) * jnp.max(jnp.abs(lw['gkb']))).reshape(1)
    out_bt = _mla_call(bound, qb, kb, vbt)
    x1, xn, aff_t = _out_call(x, out_a, out_bt, u, bc, lw)

    slot, gate, off = _topk_call(aff_t, tabs['tri'], cap)
    rows_per_chunk = MOE_CHUNK // LANES
    coff = off[:, :, 0, :s // LANES:rows_per_chunk].astype(jnp.int32)
    coff = jnp.concatenate([coff, jnp.full((b, N_EXPERTS, 1), cap, jnp.int32)], axis=-1).reshape(-1)
    slot5 = slot.reshape(b, N_EXPERTS, -1)[:, :, :s].reshape(b, N_EXPERTS, n_chunks, 1, MOE_CHUNK)
    gate5 = gate.reshape(b, N_EXPERTS, -1)[:, :, :s].reshape(b, N_EXPERTS, n_chunks, 1, MOE_CHUNK)

    xc = _gather_call(coff, xn, slot5, cap)
    y5 = _ffn_call(xc, lw['w_gate'], lw['w_up'], lw['w_down_t'])
    moe_out = _scatter_call(coff, y5, slot5, gate5)
    return x1, moe_out


def kernel(x, norm_mix, w_in, q_norm_a, k_norm_a, rpb, cq_norm, w_uq, ckv_norm, w_ukv, q_norm_b,
           k_norm_b, conv_w, out_norm, w_out, norm_ffn, w_router, b_router, w_gate, w_up, w_down):
    p = dict(norm_mix=norm_mix, w_in=w_in, q_norm_a=q_norm_a, k_norm_a=k_norm_a, cq_norm=cq_norm,
             w_uq=w_uq, ckv_norm=ckv_norm, w_ukv=w_ukv, q_norm_b=q_norm_b, k_norm_b=k_norm_b,
             conv_w=conv_w, out_norm=out_norm, w_out=w_out, norm_ffn=norm_ffn, w_router=w_router,
             b_router=b_router, w_gate=w_gate, w_up=w_up, w_down=w_down)
    b, s, d = x.shape
    assert d == D_MODEL and (CAPACITY_FACTOR * s // N_EXPERTS) % MOE_CHUNK == 0
    assert s % TOK_TILE == 0 and (s // GRID_W) % NA_ROWS == 0 and s <= TOPK_ROWS * LANES
    tabs = _tables(s)
    moe = None
    for l in range(norm_mix.shape[0]):
        x, moe = _layer(x, moe, _layer_weights(l, p), tabs, rpb[l])
    return _final_call(x, moe)
```

```python
import functools

import numpy as np
import jax
import jax.numpy as jnp
from jax import lax
from jax.experimental import pallas as pl
from jax.experimental.pallas import tpu as pltpu

D_MODEL = 1024
GRID_W = 64
HEAD_DIM = 64
EPS = 1e-6

NA_HEADS = 6
NA_WIN_H = 8
NA_WIN_W = 16
A_WIDTH = NA_HEADS * HEAD_DIM

MLA_HEADS = 6
MLA_Q_RANK = 256
MLA_KV_RANK = 128
MLA_NOPE = 64
MLA_ROPE = 32
MLA_QK = MLA_NOPE + MLA_ROPE
MLA_V = 64
B_WIDTH = MLA_HEADS * MLA_V
ROPE_THETA = 10000.0

CONV_CH = 256
C_WIDTH = CONV_CH

N_EXPERTS = 16
EXPERT_FF = 1024
CAPACITY_FACTOR = 2

LANES = 128
HEAD_PAD = 128
QB_WIDTH = MLA_HEADS * HEAD_PAD
IN_COLS_PAD = 2432
TOK_TILE = 512
NA_ROWS = 8
MOE_CHUNK = 256
MOE_COLS = 256
GATHER_COLS = 512
SCATTER_CHUNKS = 2
TOPK_ROWS = 128
MLA_VROWS = MLA_V + 16
LOG2_E = 1.4426950408889634
MLA_UNROLL = 8
NEG_BIG = -1e30
VMEM_LIMIT = 56 * 1024 * 1024

F32 = jnp.float32
BF16 = jnp.bfloat16


def _dot(a, b):
    return jnp.dot(a, b, preferred_element_type=F32)


def _dot_nt(a, b):
    return lax.dot_general(a, b, (((1,), (1,)), ((), ())), preferred_element_type=F32)


def _rms(x, g):
    return x * lax.rsqrt(jnp.mean(x * x, axis=-1, keepdims=True) + EPS) * g


def _group_sumsq(y, e_ref):
    sq = (y * y).astype(BF16)
    width = y.shape[1]
    parts = []
    for c in range(0, width, 2 * LANES):
        n = min(2 * LANES, width - c)
        parts.append(_dot(sq[:, c:c + n], e_ref[0:n, 0:n]))
    return jnp.concatenate(parts, axis=1)


def _moe_tile_to_rows(moe_ref, n_chunks):
    rows = []
    for ch in range(n_chunks):
        cols = [moe_ref[0, cq, ch].T for cq in range(D_MODEL // MOE_COLS)]
        rows.append(jnp.concatenate(cols, axis=1))
    return jnp.concatenate(rows, axis=0)


def _proj_kernel(*refs, has_moe):
    if has_moe:
        x_ref, moe_ref = refs[0], refs[1]
        refs = refs[2:]
    else:
        x_ref = refs[0]
        refs = refs[1:]
    (nmix_ref, win_ref, gqa_ref, gka_ref, e384_ref, cqn_ref, wuq_ref, ckvn_ref, wkk_ref,
     wuv_ref, gqb_ref, gkb_ref, e768_ref, cos_ref, sin_ref) = refs[:15]
    outs = refs[15:]
    if has_moe:
        qa_o, ka_o, va_o, qb_o, kb_o, vbt_o, u_o, bc_o, xs_o = outs
    else:
        qa_o, ka_o, va_o, qb_o, kb_o, vbt_o, u_o, bc_o = outs

    x = x_ref[0]
    if has_moe:
        x = x + _moe_tile_to_rows(moe_ref, TOK_TILE // MOE_CHUNK)
        xs_o[0] = x
    xb = _rms(x, nmix_ref[...]).astype(BF16)

    pa = _dot(xb, win_ref[:, 0:3 * A_WIDTH])
    qa = pa[:, 0:A_WIDTH]
    ka = pa[:, A_WIDTH:2 * A_WIDTH]
    qa_o[0] = (qa * lax.rsqrt(_group_sumsq(qa, e384_ref) * (1.0 / HEAD_DIM) + EPS) * gqa_ref[...]).astype(BF16)
    ka_o[0] = (ka * lax.rsqrt(_group_sumsq(ka, e384_ref) * (1.0 / HEAD_DIM) + EPS) * gka_ref[...]).astype(BF16)
    va_o[0] = pa[:, 2 * A_WIDTH:3 * A_WIDTH].astype(BF16)

    c0 = 3 * A_WIDTH
    pb = _dot(xb, win_ref[:, c0:c0 + 512])
    cq = _rms(pb[:, 0:MLA_Q_RANK], cqn_ref[...]).astype(BF16)
    ckv = _rms(pb[:, MLA_Q_RANK:MLA_Q_RANK + MLA_KV_RANK], ckvn_ref[...]).astype(BF16)
    kr = pb[:, MLA_Q_RANK + MLA_KV_RANK:512].astype(BF16)
    qb_raw = _dot(cq, wuq_ref[...])
    kb_raw = _dot(jnp.concatenate([ckv, kr], axis=1), wkk_ref[...])
    vb = _dot(ckv, wuv_ref[...])

    cos = jnp.concatenate([cos_ref[...]] * MLA_HEADS, axis=1)
    sin = jnp.concatenate([sin_ref[...]] * MLA_HEADS, axis=1)
    lane = lax.broadcasted_iota(jnp.int32, (TOK_TILE, QB_WIDTH), 1) & (HEAD_PAD - 1)
    first_half = (lane >= MLA_NOPE) & (lane < MLA_NOPE + MLA_ROPE // 2)

    def norm_rope(y, g_ref):
        y = y * lax.rsqrt(_group_sumsq(y, e768_ref) * (1.0 / MLA_QK) + EPS) * g_ref[...]
        rot = jnp.where(first_half,
                        pltpu.roll(y, QB_WIDTH - MLA_ROPE // 2, 1),
                        pltpu.roll(y, MLA_ROPE // 2, 1))
        return (y * cos + rot * sin).astype(BF16)

    qb_o[0] = norm_rope(qb_raw, gqb_ref)
    kb_o[0] = norm_rope(kb_raw, gkb_ref)
    vbt = vb.T
    for h in range(MLA_HEADS):
        vbt_o[0, h, 0, 0:MLA_V, :] = vbt[h * MLA_V:(h + 1) * MLA_V, :].astype(BF16)
        vbt_o[0, h, 0, MLA_V:MLA_VROWS, :] = jnp.ones((MLA_VROWS - MLA_V, TOK_TILE), BF16)

    c1 = c0 + 512
    pc = _dot(xb, win_ref[:, c1:c1 + 3 * CONV_CH])
    u_o[0] = pc[:, 2 * CONV_CH:3 * CONV_CH] * pc[:, 0:CONV_CH]
    bc_o[0] = pc[:, CONV_CH:2 * CONV_CH]


def _proj_call(x, moe, lw, tabs):
    b, s, d = x.shape
    nt = s // TOK_TILE
    has_moe = moe is not None

    def const(shape):
        return pl.BlockSpec(shape, lambda bi, ti: (0,) * len(shape))

    in_specs = [pl.BlockSpec((1, TOK_TILE, d), lambda bi, ti: (bi, ti, 0))]
    args = [x]
    if has_moe:
        ncq = D_MODEL // MOE_COLS
        in_specs.append(pl.BlockSpec((1, ncq, TOK_TILE // MOE_CHUNK, MOE_COLS, MOE_CHUNK),
                                     lambda bi, ti: (bi, 0, ti, 0, 0)))
        args.append(moe)
    consts = [lw['nmix'], lw['w_in'], lw['gqa'], lw['gka'], tabs['e384'], lw['cqn'], lw['w_uq'],
              lw['ckvn'], lw['w_kk'], lw['w_uv'], lw['gqb'], lw['gkb'], tabs['e768']]
    for c in consts:
        in_specs.append(const(c.shape))
        args.append(c)
    in_specs += [pl.BlockSpec((TOK_TILE, HEAD_PAD), lambda bi, ti: (ti, 0))] * 2
    args += [tabs['cos'], tabs['sin']]

    def tok(width, dtype):
        return (jax.ShapeDtypeStruct((b, s, width), dtype),
                pl.BlockSpec((1, TOK_TILE, width), lambda bi, ti: (bi, ti, 0)))

    outs = [tok(A_WIDTH, BF16), tok(A_WIDTH, BF16), tok(A_WIDTH, BF16),
            tok(QB_WIDTH, BF16), tok(QB_WIDTH, BF16),
            (jax.ShapeDtypeStruct((b, MLA_HEADS, nt, MLA_VROWS, TOK_TILE), BF16),
             pl.BlockSpec((1, MLA_HEADS, 1, MLA_VROWS, TOK_TILE), lambda bi, ti: (bi, 0, ti, 0, 0))),
            tok(CONV_CH, F32), tok(CONV_CH, F32)]
    if has_moe:
        outs.append(tok(d, F32))
    return pl.pallas_call(
        functools.partial(_proj_kernel, has_moe=has_moe),
        grid=(b, nt),
        in_specs=in_specs,
        out_specs=[o[1] for o in outs],
        out_shape=[o[0] for o in outs],
        compiler_params=pltpu.CompilerParams(
            dimension_semantics=("arbitrary", "arbitrary"), vmem_limit_bytes=VMEM_LIMIT),
        name="proj_moe" if has_moe else "proj",
    )(*args)


def _bias_kernel(rpb_ref, o_ref):
    h = pl.program_id(0)
    delta = pl.program_id(1)
    nw = 2 * NA_WIN_W - 1
    shape = (GRID_W, NA_WIN_H * GRID_W)
    c = lax.broadcasted_iota(jnp.int32, shape, 0)
    col = lax.broadcasted_iota(jnp.int32, shape, 1)
    wrow = lax.shift_right_logical(col, 6)
    kc = col & (GRID_W - 1)
    c_lo = jnp.clip(c - NA_WIN_W // 2, 0, GRID_W - NA_WIN_W)
    inside = (kc >= c_lo) & (kc < c_lo + NA_WIN_W)
    d = kc - c + (NA_WIN_W - 1)
    acc = jnp.zeros(shape, F32)
    base = h * ((2 * NA_WIN_H - 1) * nw)
    for i in range(NA_WIN_H):
        ri = i - delta + (NA_WIN_H - 1)
        for dd in range(nw):
            val = rpb_ref[base + ri * nw + dd]
            acc = jnp.where((wrow == i) & (d == dd), val, acc)
    o_ref[0, 0] = jnp.where(inside, acc, NEG_BIG)


def _bias_call(rpb_l):
    flat = rpb_l.reshape(-1)
    return pl.pallas_call(
        _bias_kernel,
        grid=(NA_HEADS, NA_WIN_H),
        in_specs=[pl.BlockSpec(memory_space=pltpu.SMEM)],
        out_specs=pl.BlockSpec((1, 1, GRID_W, NA_WIN_H * GRID_W), lambda h, dl: (h, dl, 0, 0)),
        out_shape=jax.ShapeDtypeStruct((NA_HEADS, NA_WIN_H, GRID_W, NA_WIN_H * GRID_W), F32),
        name="na_bias",
    )(flat)


def _na_kernel(q_ref, kp_ref, kc_ref, kn_ref, vp_ref, vc_ref, vn_ref, bias_ref, o_ref,
               k_scr, v_scr, *, rows):
    j = pl.program_id(1)
    blk = NA_ROWS * GRID_W
    k_scr[0:blk] = kp_ref[0]
    k_scr[blk:2 * blk] = kc_ref[0]
    k_scr[2 * blk:3 * blk] = kn_ref[0]
    v_scr[0:blk] = vp_ref[0]
    v_scr[blk:2 * blk] = vc_ref[0]
    v_scr[2 * blk:3 * blk] = vn_ref[0]
    lane = lax.broadcasted_iota(jnp.int32, (GRID_W, LANES), 1)
    low = lane < HEAD_DIM
    win = NA_WIN_H * GRID_W

    def window_start(i):
        r = NA_ROWS * j + i
        r0 = jnp.clip(r - NA_WIN_H // 2, 0, rows - NA_WIN_H)
        return r - r0, pl.multiple_of((r0 - NA_ROWS * j + NA_ROWS) * GRID_W, GRID_W)

    tiles = []
    for i in range(NA_ROWS):
        delta, start = window_start(i)
        for p in range(NA_HEADS // 2):
            lsl = slice(p * LANES, (p + 1) * LANES)
            qp = q_ref[0, i * GRID_W:(i + 1) * GRID_W, lsl]
            kw = k_scr[pl.ds(start, win), lsl]
            for a in range(2):
                qm = jnp.where(low if a == 0 else jnp.logical_not(low), qp, jnp.zeros_like(qp))
                tiles.append(_dot_nt(qm, kw) + bias_ref[2 * p + a, delta])
    sc = jnp.concatenate(tiles, axis=0)
    e = jnp.exp(sc - jnp.max(sc, axis=-1, keepdims=True))
    inv = 1.0 / jnp.sum(e, axis=-1, keepdims=True)
    eb = e.astype(BF16)
    t = 0
    for i in range(NA_ROWS):
        _, start = window_start(i)
        for p in range(NA_HEADS // 2):
            lsl = slice(p * LANES, (p + 1) * LANES)
            vw = v_scr[pl.ds(start, win), lsl]
            res = []
            for a in range(2):
                rsl = slice(t * GRID_W, (t + 1) * GRID_W)
                res.append(_dot(eb[rsl], vw) * inv[rsl])
                t += 1
            o_ref[0, i * GRID_W:(i + 1) * GRID_W, lsl] = jnp.where(low, res[0], res[1])


def _na_call(qa, ka, va, bias_tab):
    b, s, w = qa.shape
    rows = s // GRID_W
    nb = rows // NA_ROWS
    blk = NA_ROWS * GRID_W
    cur = pl.BlockSpec((1, blk, w), lambda bi, j: (bi, j, 0))
    prev = pl.BlockSpec((1, blk, w), lambda bi, j: (bi, jnp.maximum(j - 1, 0), 0))
    nxt = pl.BlockSpec((1, blk, w), lambda bi, j: (bi, jnp.minimum(j + 1, nb - 1), 0))
    return pl.pallas_call(
        functools.partial(_na_kernel, rows=rows),
        grid=(b, nb),
        in_specs=[cur, prev, cur, nxt, prev, cur, nxt,
                  pl.BlockSpec(bias_tab.shape, lambda bi, j: (0, 0, 0, 0))],
        out_specs=pl.BlockSpec((1, blk, w), lambda bi, j: (bi, j, 0)),
        out_shape=jax.ShapeDtypeStruct((b, s, w), F32),
        scratch_shapes=[pltpu.VMEM((3 * blk, w), BF16), pltpu.VMEM((3 * blk, w), BF16)],
        compiler_params=pltpu.CompilerParams(
            dimension_semantics=("arbitrary", "arbitrary"), vmem_limit_bytes=VMEM_LIMIT),
        name="na_attn",
    )(qa, ka, ka, ka, va, va, va, bias_tab)


def _mla_kernel(q_ref, k_ref, vt_ref, o_ref, s_scr, acc_scr, m_scr, *, n_chunks):
    q = q_ref[0]

    def scores(c):
        kb = k_ref[0, pl.ds(pl.multiple_of(c * TOK_TILE, TOK_TILE), TOK_TILE), :]
        return _dot_nt(kb, q)

    def update(st, c):
        m = m_scr[...]
        m_new = jnp.maximum(m, jnp.max(st, axis=0, keepdims=True))
        alpha = jnp.exp2(m - m_new)
        p = jnp.exp2(st - m_new).astype(BF16)
        acc_scr[...] = alpha * acc_scr[...] + _dot(vt_ref[0, 0, c], p)
        m_scr[...] = m_new

    m_scr[...] = jnp.full(m_scr.shape, NEG_BIG, F32)
    acc_scr[...] = jnp.zeros(acc_scr.shape, F32)
    s_scr[0] = scores(0)

    def body(i, carry):
        for j in range(MLA_UNROLL):
            c = MLA_UNROLL * i + j
            s_scr[(j + 1) % 2] = scores(jnp.minimum(c + 1, n_chunks - 1))
            update(s_scr[j % 2], c)
        return carry

    lax.fori_loop(0, n_chunks // MLA_UNROLL, body, 0)
    acc = acc_scr[...]
    o_ref[0] = acc[0:MLA_V, :] / acc[MLA_V:MLA_V + 1, :]


def _mla_call(qb, kb, vbt):
    b, s, _ = qb.shape
    nt = s // TOK_TILE
    assert nt % MLA_UNROLL == 0 and MLA_UNROLL % 2 == 0
    return pl.pallas_call(
        functools.partial(_mla_kernel, n_chunks=nt),
        grid=(b, MLA_HEADS, nt),
        in_specs=[pl.BlockSpec((1, TOK_TILE, HEAD_PAD), lambda bi, h, qi: (bi, qi, h)),
                  pl.BlockSpec((1, s, HEAD_PAD), lambda bi, h, qi: (bi, 0, h)),
                  pl.BlockSpec((1, 1, nt, MLA_VROWS, TOK_TILE), lambda bi, h, qi: (bi, h, 0, 0, 0))],
        out_specs=pl.BlockSpec((1, MLA_V, TOK_TILE), lambda bi, h, qi: (bi, h, qi)),
        out_shape=jax.ShapeDtypeStruct((b, B_WIDTH, s), F32),
        scratch_shapes=[pltpu.VMEM((2, TOK_TILE, TOK_TILE), F32), pltpu.VMEM((MLA_VROWS, TOK_TILE), F32),
                        pltpu.VMEM((1, TOK_TILE), F32)],
        compiler_params=pltpu.CompilerParams(
            dimension_semantics=("arbitrary", "arbitrary", "arbitrary"), vmem_limit_bytes=VMEM_LIMIT),
        name="mla_attn",
    )(qb, kb, vbt)


def _out_kernel(x_ref, oa_ref, obt_ref, u_ref, up_ref, un_ref, bc_ref, ga_ref, gb_ref, gc_ref,
                woa_ref, wob_ref, woc_ref, cw_ref, nffn_ref, wrh_ref, wrl_ref, br_ref,
                x1_o, xn_o, aff_o):
    ti = pl.program_id(1)
    nt = pl.num_programs(1)
    t = TOK_TILE

    na = _rms(oa_ref[0], ga_ref[...]).astype(BF16)
    acc = _dot(na, woa_ref[...])

    obt = obt_ref[0]
    nbt = obt * lax.rsqrt(jnp.mean(obt * obt, axis=0, keepdims=True) + EPS) * gb_ref[...]
    acc = acc + _dot(nbt.T.astype(BF16), wob_ref[...])

    u = u_ref[0]
    row = lax.broadcasted_iota(jnp.int32, (t, CONV_CH), 0)
    prev_row = jnp.where(ti > 0, up_ref[0, 7:8, :], 0.0)
    next_row = jnp.where(ti < nt - 1, un_ref[0, 0:1, :], 0.0)
    u_m1 = jnp.where(row == 0, prev_row, pltpu.roll(u, 1, 0))
    u_p1 = jnp.where(row == t - 1, next_row, pltpu.roll(u, t - 1, 0))
    y = cw_ref[0:1, :] * u_m1 + cw_ref[1:2, :] * u + cw_ref[2:3, :] * u_p1
    oc = _rms(bc_ref[0] * y, gc_ref[...]).astype(BF16)
    acc = acc + _dot(oc, woc_ref[...])

    x1 = x_ref[0] + acc
    x1_o[0] = x1
    xn = _rms(x1, nffn_ref[...])
    hi = xn.astype(BF16)
    lo = (xn - hi.astype(F32)).astype(BF16)
    xn_o[0] = hi
    logits = _dot(hi, wrh_ref[...]) + _dot(lo, wrh_ref[...]) + _dot(hi, wrl_ref[...]) + br_ref[...]
    m = jnp.max(logits, axis=-1, keepdims=True)
    e = jnp.exp(logits - m)
    aff = e / jnp.sum(e, axis=-1, keepdims=True)
    aff_o[0] = aff.T[0:N_EXPERTS, :]


def _out_call(x, out_a, out_bt, u, bc, lw):
    b, s, d = x.shape
    nt = s // TOK_TILE
    h8 = TOK_TILE // 8

    def const(a):
        return pl.BlockSpec(a.shape, lambda bi, ti: (0,) * a.ndim)

    def tok(width):
        return pl.BlockSpec((1, TOK_TILE, width), lambda bi, ti: (bi, ti, 0))

    consts = [lw['ga'], lw['gb_col'], lw['gc'], lw['wo_a'], lw['wo_b'], lw['wo_c'], lw['conv_w'],
              lw['nffn'], lw['wr_hi'], lw['wr_lo'], lw['b_r']]
    in_specs = [tok(d), tok(A_WIDTH),
                pl.BlockSpec((1, B_WIDTH, TOK_TILE), lambda bi, ti: (bi, 0, ti)),
                tok(CONV_CH),
                pl.BlockSpec((1, 8, CONV_CH), lambda bi, ti: (bi, jnp.maximum(ti * h8 - 1, 0), 0)),
                pl.BlockSpec((1, 8, CONV_CH), lambda bi, ti: (bi, jnp.minimum((ti + 1) * h8, s // 8 - 1), 0)),
                tok(CONV_CH)] + [const(c) for c in consts]
    return pl.pallas_call(
        _out_kernel,
        grid=(b, nt),
        in_specs=in_specs,
        out_specs=[tok(d), tok(d), pl.BlockSpec((1, N_EXPERTS, TOK_TILE), lambda bi, ti: (bi, 0, ti))],
        out_shape=[jax.ShapeDtypeStruct((b, s, d), F32), jax.ShapeDtypeStruct((b, s, d), BF16),
                   jax.ShapeDtypeStruct((b, N_EXPERTS, s), F32)],
        compiler_params=pltpu.CompilerParams(
            dimension_semantics=("arbitrary", "arbitrary"), vmem_limit_bytes=VMEM_LIMIT),
        name="out_proj",
    )(x, out_a, out_bt, u, u, u, bc, *consts)


def _topk_kernel(a_ref, tri_ref, o_slot, o_gate, o_off, *, cap):
    a = a_ref[0, 0]
    bits = pltpu.bitcast(a, jnp.int32)

    def total(mask):
        v = jnp.where(mask, 1.0, 0.0)
        return jnp.sum(jnp.sum(v, axis=0, keepdims=True), axis=1, keepdims=True)

    thr = jnp.zeros((1, 1), jnp.int32)
    for bit in range(30, -1, -1):
        cand = thr | (1 << bit)
        thr = jnp.where(total(bits >= cand) >= cap, cand, thr)
    gt = bits > thr
    eq = bits == thr
    need = cap - total(gt)

    upper = tri_ref[0]
    lower_strict = tri_ref[1]
    upper_strict = tri_ref[2]

    def excl_rank(mask):
        f = jnp.where(mask, 1.0, 0.0)
        fb = f.astype(BF16)
        in_row = _dot(fb, upper)
        before_rows = jnp.sum(_dot(lower_strict, fb), axis=1, keepdims=True)
        return before_rows + in_row - f, fb

    eq_rank, _ = excl_rank(eq)
    sel = gt | (eq & (eq_rank < need))
    pos, sel_b = excl_rank(sel)
    o_slot[0, 0] = jnp.where(sel, pos, -1.0).astype(jnp.int32)
    o_gate[0, 0] = jnp.where(sel, a, 0.0)
    row_tot = _dot_nt(jnp.ones((8, LANES), BF16), sel_b)
    o_off[0, 0] = _dot(row_tot.astype(BF16), upper_strict)


def _topk_call(aff_t, tri, cap):
    b, e, s = aff_t.shape
    pad = TOPK_ROWS * LANES - s
    a = aff_t if pad == 0 else jnp.pad(aff_t, ((0, 0), (0, 0), (0, pad)), constant_values=-1.0)
    a = a.reshape(b, e, TOPK_ROWS, LANES)
    blk = pl.BlockSpec((1, 1, TOPK_ROWS, LANES), lambda bi, ei: (bi, ei, 0, 0))
    return pl.pallas_call(
        functools.partial(_topk_kernel, cap=cap),
        grid=(b, e),
        in_specs=[blk, pl.BlockSpec(tri.shape, lambda bi, ei: (0, 0, 0))],
        out_specs=[blk, blk, pl.BlockSpec((1, 1, 8, LANES), lambda bi, ei: (bi, ei, 0, 0))],
        out_shape=[jax.ShapeDtypeStruct((b, e, TOPK_ROWS, LANES), jnp.int32),
                   jax.ShapeDtypeStruct((b, e, TOPK_ROWS, LANES), F32),
                   jax.ShapeDtypeStruct((b, e, 8, LANES), F32)],
        name="topk_select",
    )(a, tri)


def _window_onehot(slot_row, start):
    i = lax.broadcasted_iota(jnp.int32, (2 * MOE_CHUNK, MOE_CHUNK), 0)
    return jnp.where(i == slot_row - start, 1.0, 0.0).astype(BF16)


def _window_tile(off, tiles):
    return jnp.minimum(lax.shift_right_logical(off, 8), tiles - 1)


def _gather_kernel(coff_ref, x_ref, slot_ref, o_ref, acc_ref, *, n_chunks, cap):
    bi = pl.program_id(0)
    ei = pl.program_id(2)
    base = (bi * N_EXPERTS + ei) * (n_chunks + 1)
    acc_ref[...] = jnp.zeros_like(acc_ref)

    def body(k, carry):
        start = pl.multiple_of(_window_tile(coff_ref[base + k], cap // MOE_CHUNK) * MOE_CHUNK, MOE_CHUNK)
        g = _window_onehot(slot_ref[0, 0, k], start)
        xk = x_ref[0, pl.ds(pl.multiple_of(k * MOE_CHUNK, MOE_CHUNK), MOE_CHUNK), :]
        acc_ref[pl.ds(start, 2 * MOE_CHUNK), :] += _dot(g, xk)
        return carry

    lax.fori_loop(0, n_chunks, body, 0, unroll=2)
    o_ref[0, 0] = acc_ref[0:cap, :].astype(BF16)


def _gather_call(coff, xn, slot5, cap):
    b, s, d = xn.shape
    n_chunks = s // MOE_CHUNK
    ncq = d // GATHER_COLS
    grid_spec = pltpu.PrefetchScalarGridSpec(
        num_scalar_prefetch=1,
        grid=(b, ncq, N_EXPERTS),
        in_specs=[pl.BlockSpec((1, s, GATHER_COLS), lambda bi, cq, ei, co: (bi, 0, cq)),
                  pl.BlockSpec((1, 1, n_chunks, 1, MOE_CHUNK), lambda bi, cq, ei, co: (bi, ei, 0, 0, 0))],
        out_specs=pl.BlockSpec((1, 1, cap, GATHER_COLS), lambda bi, cq, ei, co: (bi, ei, 0, cq)),
        scratch_shapes=[pltpu.VMEM((cap + MOE_CHUNK, GATHER_COLS), F32)],
    )
    return pl.pallas_call(
        functools.partial(_gather_kernel, n_chunks=n_chunks, cap=cap),
        grid_spec=grid_spec,
        out_shape=jax.ShapeDtypeStruct((b, N_EXPERTS, cap, d), BF16),
        compiler_params=pltpu.CompilerParams(
            dimension_semantics=("arbitrary", "arbitrary", "arbitrary"), vmem_limit_bytes=VMEM_LIMIT),
        name="moe_gather",
    )(coff, xn, slot5)


def _ffn_kernel(xc_ref, wg_ref, wu_ref, wdt_ref, o_ref, *, tiles):
    xc = xc_ref[0, 0]
    g = _dot(xc, wg_ref[0])
    u = _dot(xc, wu_ref[0])
    h = (g / (1.0 + jnp.exp(-g)) * u).astype(BF16)
    yt = _dot_nt(wdt_ref[0], h)
    for r in range(tiles):
        o_ref[0, 0, r] = yt[:, r * MOE_CHUNK:(r + 1) * MOE_CHUNK].astype(BF16)


def _ffn_call(xc, wg, wu, wdt):
    b, e, cap, d = xc.shape
    ff = wg.shape[-1]
    rt = min(cap, 1024)
    tiles = rt // MOE_CHUNK
    return pl.pallas_call(
        functools.partial(_ffn_kernel, tiles=tiles),
        grid=(e, b, cap // rt),
        in_specs=[pl.BlockSpec((1, 1, rt, d), lambda ei, bi, ri: (bi, ei, ri, 0)),
                  pl.BlockSpec((1, d, ff), lambda ei, bi, ri: (ei, 0, 0)),
                  pl.BlockSpec((1, d, ff), lambda ei, bi, ri: (ei, 0, 0)),
                  pl.BlockSpec((1, d, ff), lambda ei, bi, ri: (ei, 0, 0))],
        out_specs=pl.BlockSpec((1, 1, tiles, d, MOE_CHUNK), lambda ei, bi, ri: (bi, ei, ri, 0, 0)),
        out_shape=jax.ShapeDtypeStruct((b, e, cap // MOE_CHUNK, d, MOE_CHUNK), BF16),
        compiler_params=pltpu.CompilerParams(
            dimension_semantics=("arbitrary", "arbitrary", "arbitrary"), vmem_limit_bytes=VMEM_LIMIT),
        name="moe_ffn",
    )(xc, wg, wu, wdt)


def _scatter_kernel(coff_ref, y_ref, slot_ref, gate_ref, o_ref, *, n_chunks, tiles):
    bi = pl.program_id(0)
    kg = pl.program_id(2)
    for j in range(SCATTER_CHUNKS):
        k = kg * SCATTER_CHUNKS + j
        acc = jnp.zeros((MOE_COLS, MOE_CHUNK), F32)
        for e in range(N_EXPERTS):
            r0 = _window_tile(coff_ref[(bi * N_EXPERTS + e) * (n_chunks + 1) + k], tiles)
            r1 = jnp.minimum(r0 + 1, tiles - 1)
            g = _window_onehot(slot_ref[0, e, j], r0 * MOE_CHUNK)
            yw = jnp.concatenate([y_ref[0, e, r0], y_ref[0, e, r1]], axis=1)
            acc = acc + _dot(yw, g) * gate_ref[0, e, j]
        o_ref[0, 0, j] = acc


def _scatter_call(coff, y5, slot5, gate5):
    b, e, tiles, d, _ = y5.shape
    n_chunks = slot5.shape[2]
    ncq = d // MOE_COLS
    small = pl.BlockSpec((1, e, SCATTER_CHUNKS, 1, MOE_CHUNK), lambda bi, cq, kg, co: (bi, 0, kg, 0, 0))
    grid_spec = pltpu.PrefetchScalarGridSpec(
        num_scalar_prefetch=1,
        grid=(b, ncq, n_chunks // SCATTER_CHUNKS),
        in_specs=[pl.BlockSpec((1, e, tiles, MOE_COLS, MOE_CHUNK), lambda bi, cq, kg, co: (bi, 0, 0, cq, 0)),
                  small, small],
        out_specs=pl.BlockSpec((1, 1, SCATTER_CHUNKS, MOE_COLS, MOE_CHUNK),
                               lambda bi, cq, kg, co: (bi, cq, kg, 0, 0)),
    )
    return pl.pallas_call(
        functools.partial(_scatter_kernel, n_chunks=n_chunks, tiles=tiles),
        grid_spec=grid_spec,
        out_shape=jax.ShapeDtypeStruct((b, ncq, n_chunks, MOE_COLS, MOE_CHUNK), F32),
        compiler_params=pltpu.CompilerParams(
            dimension_semantics=("arbitrary", "arbitrary", "arbitrary"), vmem_limit_bytes=VMEM_LIMIT),
        name="moe_scatter",
    )(coff, y5, slot5, gate5)


def _final_kernel(x_ref, moe_ref, o_ref):
    o_ref[0] = x_ref[0] + _moe_tile_to_rows(moe_ref, TOK_TILE // MOE_CHUNK)


def _final_call(x1, moe):
    b, s, d = x1.shape
    ncq = d // MOE_COLS
    tok = pl.BlockSpec((1, TOK_TILE, d), lambda bi, ti: (bi, ti, 0))
    return pl.pallas_call(
        _final_kernel,
        grid=(b, s // TOK_TILE),
        in_specs=[tok, pl.BlockSpec((1, ncq, TOK_TILE // MOE_CHUNK, MOE_COLS, MOE_CHUNK),
                                    lambda bi, ti: (bi, 0, ti, 0, 0))],
        out_specs=tok,
        out_shape=jax.ShapeDtypeStruct((b, s, d), F32),
        compiler_params=pltpu.CompilerParams(
            dimension_semantics=("arbitrary", "arbitrary"), vmem_limit_bytes=VMEM_LIMIT),
        name="final_add",
    )(x1, moe)


def _block_ones(n, group):
    i = np.arange(n)
    return jnp.asarray((i[:, None] // group) == (i[None, :] // group), BF16)


def _tables(s):
    half = MLA_ROPE // 2
    inv = ROPE_THETA ** (-jnp.arange(half, dtype=F32) / half)
    ang = jnp.arange(s, dtype=jnp.int32).astype(F32)[:, None] * inv[None, :]
    cos, sin = jnp.cos(ang), jnp.sin(ang)
    ones = jnp.ones((s, MLA_NOPE), F32)
    tail = HEAD_PAD - MLA_QK
    cos_t = jnp.concatenate([ones, cos, cos, jnp.ones((s, tail), F32)], axis=1)
    sin_t = jnp.concatenate([0 * ones, -sin, sin, jnp.zeros((s, tail), F32)], axis=1)
    i = np.arange(LANES)
    tri = np.stack([i[:, None] <= i[None, :], i[None, :] < i[:, None], i[:, None] < i[None, :]])
    return {'cos': cos_t, 'sin': sin_t, 'e384': _block_ones(2 * LANES, HEAD_DIM),
            'e768': _block_ones(2 * LANES, HEAD_PAD), 'tri': jnp.asarray(tri, BF16)}


def _pad_heads(v, scale=1.0):
    g = jnp.pad(v.astype(F32) * scale, (0, HEAD_PAD - MLA_QK))
    return jnp.tile(g, MLA_HEADS)[None, :]


def _layer_weights(l, p):
    w_in = p['w_in'][l]
    kr_end = 3 * A_WIDTH + MLA_Q_RANK + MLA_KV_RANK + MLA_ROPE
    w_in_r = jnp.concatenate(
        [w_in[:, :kr_end], jnp.zeros((D_MODEL, LANES - MLA_ROPE), F32), w_in[:, kr_end:]], axis=1)
    w_uq = p['w_uq'][l].reshape(MLA_Q_RANK, MLA_HEADS, MLA_QK)
    w_uq = jnp.pad(w_uq, ((0, 0), (0, 0), (0, HEAD_PAD - MLA_QK))).reshape(MLA_Q_RANK, QB_WIDTH)
    w_ukv = p['w_ukv'][l].reshape(MLA_KV_RANK, MLA_HEADS, MLA_NOPE + MLA_V)
    w_uk = jnp.pad(w_ukv[:, :, :MLA_NOPE], ((0, 0), (0, 0), (0, HEAD_PAD - MLA_NOPE)))
    w_uk = w_uk.reshape(MLA_KV_RANK, QB_WIDTH)
    w_uv = w_ukv[:, :, MLA_NOPE:].reshape(MLA_KV_RANK, B_WIDTH)
    place = np.zeros((LANES, QB_WIDTH), np.float32)
    for h in range(MLA_HEADS):
        place[np.arange(MLA_ROPE), h * HEAD_PAD + MLA_NOPE + np.arange(MLA_ROPE)] = 1.0
    w_kk = jnp.concatenate([w_uk, jnp.asarray(place)], axis=0)
    out_norm = p['out_norm'][l]
    w_out = p['w_out'][l]
    w_r = jnp.pad(p['w_router'][l], ((0, 0), (0, LANES - N_EXPERTS)))
    wr_hi = w_r.astype(BF16)
    b_r = jnp.concatenate([p['b_router'][l].astype(F32), jnp.full((LANES - N_EXPERTS,), NEG_BIG, F32)])
    return {
        'nmix': p['norm_mix'][l][None, :], 'w_in': w_in_r.astype(BF16),
        'gqa': jnp.tile(p['q_norm_a'][l] * HEAD_DIM ** -0.5, NA_HEADS)[None, :],
        'gka': jnp.tile(p['k_norm_a'][l], NA_HEADS)[None, :],
        'cqn': p['cq_norm'][l][None, :], 'w_uq': w_uq.astype(BF16),
        'ckvn': p['ckv_norm'][l][None, :], 'w_kk': w_kk.astype(BF16), 'w_uv': w_uv.astype(BF16),
        'gqb': _pad_heads(p['q_norm_b'][l], MLA_QK ** -0.5 * LOG2_E), 'gkb': _pad_heads(p['k_norm_b'][l]),
        'ga': out_norm[None, :A_WIDTH], 'gb_col': out_norm[A_WIDTH:A_WIDTH + B_WIDTH, None],
        'gc': out_norm[None, A_WIDTH + B_WIDTH:],
        'wo_a': w_out[:A_WIDTH].astype(BF16), 'wo_b': w_out[A_WIDTH:A_WIDTH + B_WIDTH].astype(BF16),
        'wo_c': w_out[A_WIDTH + B_WIDTH:].astype(BF16),
        'conv_w': p['conv_w'][l], 'nffn': p['norm_ffn'][l][None, :],
        'wr_hi': wr_hi, 'wr_lo': (w_r - wr_hi.astype(F32)).astype(BF16), 'b_r': b_r[None, :],
        'w_gate': p['w_gate'][l].astype(BF16), 'w_up': p['w_up'][l].astype(BF16),
        'w_down_t': jnp.swapaxes(p['w_down'][l], 1, 2).astype(BF16),
    }


def _layer(x, moe, lw, tabs, rpb_l):
    b, s, d = x.shape
    cap = CAPACITY_FACTOR * s // N_EXPERTS
    n_chunks = s // MOE_CHUNK

    outs = _proj_call(x, moe, lw, tabs)
    if moe is not None:
        qa, ka, va, qb, kb, vbt, u, bc, x = outs
    else:
        qa, ka, va, qb, kb, vbt, u, bc = outs
    out_a = _na_call(qa, ka, va, _bias_call(rpb_l))
    out_bt = _mla_call(qb, kb, vbt)
    x1, xn, aff_t = _out_call(x, out_a, out_bt, u, bc, lw)

    slot, gate, off = _topk_call(aff_t, tabs['tri'], cap)
    rows_per_chunk = MOE_CHUNK // LANES
    coff = off[:, :, 0, :s // LANES:rows_per_chunk].astype(jnp.int32)
    coff = jnp.concatenate([coff, jnp.full((b, N_EXPERTS, 1), cap, jnp.int32)], axis=-1).reshape(-1)
    slot5 = slot.reshape(b, N_EXPERTS, -1)[:, :, :s].reshape(b, N_EXPERTS, n_chunks, 1, MOE_CHUNK)
    gate5 = gate.reshape(b, N_EXPERTS, -1)[:, :, :s].reshape(b, N_EXPERTS, n_chunks, 1, MOE_CHUNK)

    xc = _gather_call(coff, xn, slot5, cap)
    y5 = _ffn_call(xc, lw['w_gate'], lw['w_up'], lw['w_down_t'])
    moe_out = _scatter_call(coff, y5, slot5, gate5)
    return x1, moe_out


def kernel(x, norm_mix, w_in, q_norm_a, k_norm_a, rpb, cq_norm, w_uq, ckv_norm, w_ukv, q_norm_b,
           k_norm_b, conv_w, out_norm, w_out, norm_ffn, w_router, b_router, w_gate, w_up, w_down):
    p = dict(norm_mix=norm_mix, w_in=w_in, q_norm_a=q_norm_a, k_norm_a=k_norm_a, cq_norm=cq_norm,
             w_uq=w_uq, ckv_norm=ckv_norm, w_ukv=w_ukv, q_norm_b=q_norm_b, k_norm_b=k_norm_b,
             conv_w=conv_w, out_norm=out_norm, w_out=w_out, norm_ffn=norm_ffn, w_router=w_router,
             b_router=b_router, w_gate=w_gate, w_up=w_up, w_down=w_down)
    b, s, d = x.shape
    assert d == D_MODEL and (CAPACITY_FACTOR * s // N_EXPERTS) % MOE_CHUNK == 0
    assert s % TOK_TILE == 0 and (s // GRID_W) % NA_ROWS == 0 and s <= TOPK_ROWS * LANES
    tabs = _tables(s)
    moe = None
    for l in range(norm_mix.shape[0]):
        x, moe = _layer(x, moe, _layer_weights(l, p), tabs, rpb[l])
    return _final_call(x, moe)
```

```python
import functools

import numpy as np
import jax
import jax.numpy as jnp
from jax import lax
from jax.experimental import pallas as pl
from jax.experimental.pallas import tpu as pltpu

D_MODEL = 1024
GRID_W = 64
HEAD_DIM = 64
EPS = 1e-6

NA_HEADS = 6
NA_WIN_H = 8
NA_WIN_W = 16
A_WIDTH = NA_HEADS * HEAD_DIM

MLA_HEADS = 6
MLA_Q_RANK = 256
MLA_KV_RANK = 128
MLA_NOPE = 64
MLA_ROPE = 32
MLA_QK = MLA_NOPE + MLA_ROPE
MLA_V = 64
B_WIDTH = MLA_HEADS * MLA_V
ROPE_THETA = 10000.0

CONV_CH = 256
C_WIDTH = CONV_CH

N_EXPERTS = 16
EXPERT_FF = 1024
CAPACITY_FACTOR = 2

LANES = 128
HEAD_PAD = 128
QB_WIDTH = MLA_HEADS * HEAD_PAD
IN_COLS_PAD = 2432
TOK_TILE = 512
NA_ROWS = 8
MOE_CHUNK = 256
MOE_COLS = 256
GATHER_COLS = 512
SCATTER_CHUNKS = 2
TOPK_ROWS = 128
MLA_VROWS = MLA_V + 16
LOG2_E = 1.4426950408889634
MLA_UNROLL = 16
NEG_BIG = -1e30
VMEM_LIMIT = 56 * 1024 * 1024

F32 = jnp.float32
BF16 = jnp.bfloat16


def _dot(a, b):
    return jnp.dot(a, b, preferred_element_type=F32)


def _dot_nt(a, b):
    return lax.dot_general(a, b, (((1,), (1,)), ((), ())), preferred_element_type=F32)


def _rms(x, g):
    return x * lax.rsqrt(jnp.mean(x * x, axis=-1, keepdims=True) + EPS) * g


def _group_sumsq(y, e_ref):
    sq = (y * y).astype(BF16)
    width = y.shape[1]
    parts = []
    for c in range(0, width, 2 * LANES):
        n = min(2 * LANES, width - c)
        parts.append(_dot(sq[:, c:c + n], e_ref[0:n, 0:n]))
    return jnp.concatenate(parts, axis=1)


def _moe_tile_to_rows(moe_ref, n_chunks):
    rows = []
    for ch in range(n_chunks):
        cols = [moe_ref[0, cq, ch].T for cq in range(D_MODEL // MOE_COLS)]
        rows.append(jnp.concatenate(cols, axis=1))
    return jnp.concatenate(rows, axis=0)


def _proj_kernel(*refs, has_moe):
    if has_moe:
        x_ref, moe_ref = refs[0], refs[1]
        refs = refs[2:]
    else:
        x_ref = refs[0]
        refs = refs[1:]
    (nmix_ref, win_ref, gqa_ref, gka_ref, e384_ref, cqn_ref, wuq_ref, ckvn_ref, wkk_ref,
     wuv_ref, gqb_ref, gkb_ref, e768_ref, cos_ref, sin_ref) = refs[:15]
    outs = refs[15:]
    if has_moe:
        qa_o, ka_o, va_o, qb_o, kb_o, vbt_o, u_o, bc_o, xs_o = outs
    else:
        qa_o, ka_o, va_o, qb_o, kb_o, vbt_o, u_o, bc_o = outs

    x = x_ref[0]
    if has_moe:
        x = x + _moe_tile_to_rows(moe_ref, TOK_TILE // MOE_CHUNK)
        xs_o[0] = x
    xb = _rms(x, nmix_ref[...]).astype(BF16)

    pa = _dot(xb, win_ref[:, 0:3 * A_WIDTH])
    qa = pa[:, 0:A_WIDTH]
    ka = pa[:, A_WIDTH:2 * A_WIDTH]
    qa_o[0] = (qa * lax.rsqrt(_group_sumsq(qa, e384_ref) * (1.0 / HEAD_DIM) + EPS) * gqa_ref[...]).astype(BF16)
    ka_o[0] = (ka * lax.rsqrt(_group_sumsq(ka, e384_ref) * (1.0 / HEAD_DIM) + EPS) * gka_ref[...]).astype(BF16)
    va_o[0] = pa[:, 2 * A_WIDTH:3 * A_WIDTH].astype(BF16)

    c0 = 3 * A_WIDTH
    pb = _dot(xb, win_ref[:, c0:c0 + 512])
    cq = _rms(pb[:, 0:MLA_Q_RANK], cqn_ref[...]).astype(BF16)
    ckv = _rms(pb[:, MLA_Q_RANK:MLA_Q_RANK + MLA_KV_RANK], ckvn_ref[...]).astype(BF16)
    kr = pb[:, MLA_Q_RANK + MLA_KV_RANK:512].astype(BF16)
    qb_raw = _dot(cq, wuq_ref[...])
    kb_raw = _dot(jnp.concatenate([ckv, kr], axis=1), wkk_ref[...])
    vb = _dot(ckv, wuv_ref[...])

    cos = jnp.concatenate([cos_ref[...]] * MLA_HEADS, axis=1)
    sin = jnp.concatenate([sin_ref[...]] * MLA_HEADS, axis=1)
    lane = lax.broadcasted_iota(jnp.int32, (TOK_TILE, QB_WIDTH), 1) & (HEAD_PAD - 1)
    first_half = (lane >= MLA_NOPE) & (lane < MLA_NOPE + MLA_ROPE // 2)

    def norm_rope(y, g_ref):
        y = y * lax.rsqrt(_group_sumsq(y, e768_ref) * (1.0 / MLA_QK) + EPS) * g_ref[...]
        rot = jnp.where(first_half,
                        pltpu.roll(y, QB_WIDTH - MLA_ROPE // 2, 1),
                        pltpu.roll(y, MLA_ROPE // 2, 1))
        return y * cos + rot * sin

    qb_o[0] = norm_rope(qb_raw, gqb_ref).T.astype(BF16)
    kb_o[0] = norm_rope(kb_raw, gkb_ref).astype(BF16)
    vbt = vb.T
    for h in range(MLA_HEADS):
        vbt_o[0, h, 0, 0:MLA_V, :] = vbt[h * MLA_V:(h + 1) * MLA_V, :].astype(BF16)
        vbt_o[0, h, 0, MLA_V:MLA_VROWS, :] = jnp.ones((MLA_VROWS - MLA_V, TOK_TILE), BF16)

    c1 = c0 + 512
    pc = _dot(xb, win_ref[:, c1:c1 + 3 * CONV_CH])
    u_o[0] = pc[:, 2 * CONV_CH:3 * CONV_CH] * pc[:, 0:CONV_CH]
    bc_o[0] = pc[:, CONV_CH:2 * CONV_CH]


def _proj_call(x, moe, lw, tabs):
    b, s, d = x.shape
    nt = s // TOK_TILE
    has_moe = moe is not None

    def const(shape):
        return pl.BlockSpec(shape, lambda bi, ti: (0,) * len(shape))

    in_specs = [pl.BlockSpec((1, TOK_TILE, d), lambda bi, ti: (bi, ti, 0))]
    args = [x]
    if has_moe:
        ncq = D_MODEL // MOE_COLS
        in_specs.append(pl.BlockSpec((1, ncq, TOK_TILE // MOE_CHUNK, MOE_COLS, MOE_CHUNK),
                                     lambda bi, ti: (bi, 0, ti, 0, 0)))
        args.append(moe)
    consts = [lw['nmix'], lw['w_in'], lw['gqa'], lw['gka'], tabs['e384'], lw['cqn'], lw['w_uq'],
              lw['ckvn'], lw['w_kk'], lw['w_uv'], lw['gqb'], lw['gkb'], tabs['e768']]
    for c in consts:
        in_specs.append(const(c.shape))
        args.append(c)
    in_specs += [pl.BlockSpec((TOK_TILE, HEAD_PAD), lambda bi, ti: (ti, 0))] * 2
    args += [tabs['cos'], tabs['sin']]

    def tok(width, dtype):
        return (jax.ShapeDtypeStruct((b, s, width), dtype),
                pl.BlockSpec((1, TOK_TILE, width), lambda bi, ti: (bi, ti, 0)))

    outs = [tok(A_WIDTH, BF16), tok(A_WIDTH, BF16), tok(A_WIDTH, BF16),
            (jax.ShapeDtypeStruct((b, QB_WIDTH, s), BF16),
             pl.BlockSpec((1, QB_WIDTH, TOK_TILE), lambda bi, ti: (bi, 0, ti))),
            tok(QB_WIDTH, BF16),
            (jax.ShapeDtypeStruct((b, MLA_HEADS, nt, MLA_VROWS, TOK_TILE), BF16),
             pl.BlockSpec((1, MLA_HEADS, 1, MLA_VROWS, TOK_TILE), lambda bi, ti: (bi, 0, ti, 0, 0))),
            tok(CONV_CH, F32), tok(CONV_CH, F32)]
    if has_moe:
        outs.append(tok(d, F32))
    return pl.pallas_call(
        functools.partial(_proj_kernel, has_moe=has_moe),
        grid=(b, nt),
        in_specs=in_specs,
        out_specs=[o[1] for o in outs],
        out_shape=[o[0] for o in outs],
        compiler_params=pltpu.CompilerParams(
            dimension_semantics=("arbitrary", "arbitrary"), vmem_limit_bytes=VMEM_LIMIT),
        name="proj_moe" if has_moe else "proj",
    )(*args)


def _bias_kernel(rpb_ref, o_ref):
    h = pl.program_id(0)
    delta = pl.program_id(1)
    nw = 2 * NA_WIN_W - 1
    shape = (GRID_W, NA_WIN_H * GRID_W)
    c = lax.broadcasted_iota(jnp.int32, shape, 0)
    col = lax.broadcasted_iota(jnp.int32, shape, 1)
    wrow = lax.shift_right_logical(col, 6)
    kc = col & (GRID_W - 1)
    c_lo = jnp.clip(c - NA_WIN_W // 2, 0, GRID_W - NA_WIN_W)
    inside = (kc >= c_lo) & (kc < c_lo + NA_WIN_W)
    d = kc - c + (NA_WIN_W - 1)
    acc = jnp.zeros(shape, F32)
    base = h * ((2 * NA_WIN_H - 1) * nw)
    for i in range(NA_WIN_H):
        ri = i - delta + (NA_WIN_H - 1)
        for dd in range(nw):
            val = rpb_ref[base + ri * nw + dd]
            acc = jnp.where((wrow == i) & (d == dd), val, acc)
    o_ref[0, 0] = jnp.where(inside, acc, NEG_BIG)


def _bias_call(rpb_l):
    flat = rpb_l.reshape(-1)
    return pl.pallas_call(
        _bias_kernel,
        grid=(NA_HEADS, NA_WIN_H),
        in_specs=[pl.BlockSpec(memory_space=pltpu.SMEM)],
        out_specs=pl.BlockSpec((1, 1, GRID_W, NA_WIN_H * GRID_W), lambda h, dl: (h, dl, 0, 0)),
        out_shape=jax.ShapeDtypeStruct((NA_HEADS, NA_WIN_H, GRID_W, NA_WIN_H * GRID_W), F32),
        name="na_bias",
    )(flat)


def _na_kernel(q_ref, kp_ref, kc_ref, kn_ref, vp_ref, vc_ref, vn_ref, bias_ref, o_ref,
               k_scr, v_scr, *, rows):
    j = pl.program_id(1)
    blk = NA_ROWS * GRID_W
    k_scr[0:blk] = kp_ref[0]
    k_scr[blk:2 * blk] = kc_ref[0]
    k_scr[2 * blk:3 * blk] = kn_ref[0]
    v_scr[0:blk] = vp_ref[0]
    v_scr[blk:2 * blk] = vc_ref[0]
    v_scr[2 * blk:3 * blk] = vn_ref[0]
    lane = lax.broadcasted_iota(jnp.int32, (GRID_W, LANES), 1)
    low = lane < HEAD_DIM
    win = NA_WIN_H * GRID_W

    def window_start(i):
        r = NA_ROWS * j + i
        r0 = jnp.clip(r - NA_WIN_H // 2, 0, rows - NA_WIN_H)
        return r - r0, pl.multiple_of((r0 - NA_ROWS * j + NA_ROWS) * GRID_W, GRID_W)

    tiles = []
    for i in range(NA_ROWS):
        delta, start = window_start(i)
        for p in range(NA_HEADS // 2):
            lsl = slice(p * LANES, (p + 1) * LANES)
            qp = q_ref[0, i * GRID_W:(i + 1) * GRID_W, lsl]
            kw = k_scr[pl.ds(start, win), lsl]
            for a in range(2):
                qm = jnp.where(low if a == 0 else jnp.logical_not(low), qp, jnp.zeros_like(qp))
                tiles.append(_dot_nt(qm, kw) + bias_ref[2 * p + a, delta])
    sc = jnp.concatenate(tiles, axis=0)
    e = jnp.exp(sc - jnp.max(sc, axis=-1, keepdims=True))
    inv = 1.0 / jnp.sum(e, axis=-1, keepdims=True)
    eb = e.astype(BF16)
    t = 0
    for i in range(NA_ROWS):
        _, start = window_start(i)
        for p in range(NA_HEADS // 2):
            lsl = slice(p * LANES, (p + 1) * LANES)
            vw = v_scr[pl.ds(start, win), lsl]
            res = []
            for a in range(2):
                rsl = slice(t * GRID_W, (t + 1) * GRID_W)
                res.append(_dot(eb[rsl], vw) * inv[rsl])
                t += 1
            o_ref[0, i * GRID_W:(i + 1) * GRID_W, lsl] = jnp.where(low, res[0], res[1])


def _na_call(qa, ka, va, bias_tab):
    b, s, w = qa.shape
    rows = s // GRID_W
    nb = rows // NA_ROWS
    blk = NA_ROWS * GRID_W
    cur = pl.BlockSpec((1, blk, w), lambda bi, j: (bi, j, 0))
    prev = pl.BlockSpec((1, blk, w), lambda bi, j: (bi, jnp.maximum(j - 1, 0), 0))
    nxt = pl.BlockSpec((1, blk, w), lambda bi, j: (bi, jnp.minimum(j + 1, nb - 1), 0))
    return pl.pallas_call(
        functools.partial(_na_kernel, rows=rows),
        grid=(b, nb),
        in_specs=[cur, prev, cur, nxt, prev, cur, nxt,
                  pl.BlockSpec(bias_tab.shape, lambda bi, j: (0, 0, 0, 0))],
        out_specs=pl.BlockSpec((1, blk, w), lambda bi, j: (bi, j, 0)),
        out_shape=jax.ShapeDtypeStruct((b, s, w), F32),
        scratch_shapes=[pltpu.VMEM((3 * blk, w), BF16), pltpu.VMEM((3 * blk, w), BF16)],
        compiler_params=pltpu.CompilerParams(
            dimension_semantics=("arbitrary", "arbitrary"), vmem_limit_bytes=VMEM_LIMIT),
        name="na_attn",
    )(qa, ka, ka, ka, va, va, va, bias_tab)


def _mla_kernel(q_ref, k_ref, vt_ref, o_ref, s_scr, acc_scr, m_scr, *, n_chunks):
    qt = q_ref[0]

    def scores(c):
        kb = k_ref[0, pl.ds(pl.multiple_of(c * TOK_TILE, TOK_TILE), TOK_TILE), :]
        return _dot(kb, qt)

    def update(st, c):
        m = m_scr[...]
        m_new = jnp.maximum(m, jnp.max(st, axis=0, keepdims=True))
        alpha = jnp.exp2(m - m_new)
        p = jnp.exp2(st - m_new).astype(BF16)
        acc_scr[...] = alpha * acc_scr[...] + _dot(vt_ref[0, 0, c], p)
        m_scr[...] = m_new

    m_scr[...] = jnp.full(m_scr.shape, NEG_BIG, F32)
    acc_scr[...] = jnp.zeros(acc_scr.shape, F32)
    s_scr[0] = scores(0)

    def body(i, carry):
        for j in range(MLA_UNROLL):
            c = MLA_UNROLL * i + j
            s_scr[(j + 1) % 2] = scores(jnp.minimum(c + 1, n_chunks - 1))
            update(s_scr[j % 2], c)
        return carry

    lax.fori_loop(0, n_chunks // MLA_UNROLL, body, 0)
    acc = acc_scr[...]
    o_ref[0] = acc[0:MLA_V, :] / acc[MLA_V:MLA_V + 1, :]


def _mla_call(qb, kb, vbt):
    b, s, _ = kb.shape
    nt = s // TOK_TILE
    assert nt % MLA_UNROLL == 0 and MLA_UNROLL % 2 == 0
    return pl.pallas_call(
        functools.partial(_mla_kernel, n_chunks=nt),
        grid=(b, MLA_HEADS, nt),
        in_specs=[pl.BlockSpec((1, HEAD_PAD, TOK_TILE), lambda bi, h, qi: (bi, h, qi)),
                  pl.BlockSpec((1, s, HEAD_PAD), lambda bi, h, qi: (bi, 0, h)),
                  pl.BlockSpec((1, 1, nt, MLA_VROWS, TOK_TILE), lambda bi, h, qi: (bi, h, 0, 0, 0))],
        out_specs=pl.BlockSpec((1, MLA_V, TOK_TILE), lambda bi, h, qi: (bi, h, qi)),
        out_shape=jax.ShapeDtypeStruct((b, B_WIDTH, s), F32),
        scratch_shapes=[pltpu.VMEM((2, TOK_TILE, TOK_TILE), F32), pltpu.VMEM((MLA_VROWS, TOK_TILE), F32),
                        pltpu.VMEM((1, TOK_TILE), F32)],
        compiler_params=pltpu.CompilerParams(
            dimension_semantics=("arbitrary", "arbitrary", "arbitrary"), vmem_limit_bytes=VMEM_LIMIT),
        name="mla_attn",
    )(qb, kb, vbt)


def _out_kernel(x_ref, oa_ref, obt_ref, u_ref, up_ref, un_ref, bc_ref, ga_ref, gb_ref, gc_ref,
                woa_ref, wob_ref, woc_ref, cw_ref, nffn_ref, wrh_ref, wrl_ref, br_ref,
                x1_o, xn_o, aff_o):
    ti = pl.program_id(1)
    nt = pl.num_programs(1)
    t = TOK_TILE

    na = _rms(oa_ref[0], ga_ref[...]).astype(BF16)
    acc = _dot(na, woa_ref[...])

    obt = obt_ref[0]
    nbt = obt * lax.rsqrt(jnp.mean(obt * obt, axis=0, keepdims=True) + EPS) * gb_ref[...]
    acc = acc + _dot(nbt.T.astype(BF16), wob_ref[...])

    u = u_ref[0]
    row = lax.broadcasted_iota(jnp.int32, (t, CONV_CH), 0)
    prev_row = jnp.where(ti > 0, up_ref[0, 7:8, :], 0.0)
    next_row = jnp.where(ti < nt - 1, un_ref[0, 0:1, :], 0.0)
    u_m1 = jnp.where(row == 0, prev_row, pltpu.roll(u, 1, 0))
    u_p1 = jnp.where(row == t - 1, next_row, pltpu.roll(u, t - 1, 0))
    y = cw_ref[0:1, :] * u_m1 + cw_ref[1:2, :] * u + cw_ref[2:3, :] * u_p1
    oc = _rms(bc_ref[0] * y, gc_ref[...]).astype(BF16)
    acc = acc + _dot(oc, woc_ref[...])

    x1 = x_ref[0] + acc
    x1_o[0] = x1
    xn = _rms(x1, nffn_ref[...])
    hi = xn.astype(BF16)
    lo = (xn - hi.astype(F32)).astype(BF16)
    xn_o[0] = hi
    logits = _dot(hi, wrh_ref[...]) + _dot(lo, wrh_ref[...]) + _dot(hi, wrl_ref[...]) + br_ref[...]
    m = jnp.max(logits, axis=-1, keepdims=True)
    e = jnp.exp(logits - m)
    aff = e / jnp.sum(e, axis=-1, keepdims=True)
    aff_o[0] = aff.T[0:N_EXPERTS, :]


def _out_call(x, out_a, out_bt, u, bc, lw):
    b, s, d = x.shape
    nt = s // TOK_TILE
    h8 = TOK_TILE // 8

    def const(a):
        return pl.BlockSpec(a.shape, lambda bi, ti: (0,) * a.ndim)

    def tok(width):
        return pl.BlockSpec((1, TOK_TILE, width), lambda bi, ti: (bi, ti, 0))

    consts = [lw['ga'], lw['gb_col'], lw['gc'], lw['wo_a'], lw['wo_b'], lw['wo_c'], lw['conv_w'],
              lw['nffn'], lw['wr_hi'], lw['wr_lo'], lw['b_r']]
    in_specs = [tok(d), tok(A_WIDTH),
                pl.BlockSpec((1, B_WIDTH, TOK_TILE), lambda bi, ti: (bi, 0, ti)),
                tok(CONV_CH),
                pl.BlockSpec((1, 8, CONV_CH), lambda bi, ti: (bi, jnp.maximum(ti * h8 - 1, 0), 0)),
                pl.BlockSpec((1, 8, CONV_CH), lambda bi, ti: (bi, jnp.minimum((ti + 1) * h8, s // 8 - 1), 0)),
                tok(CONV_CH)] + [const(c) for c in consts]
    return pl.pallas_call(
        _out_kernel,
        grid=(b, nt),
        in_specs=in_specs,
        out_specs=[tok(d), tok(d), pl.BlockSpec((1, N_EXPERTS, TOK_TILE), lambda bi, ti: (bi, 0, ti))],
        out_shape=[jax.ShapeDtypeStruct((b, s, d), F32), jax.ShapeDtypeStruct((b, s, d), BF16),
                   jax.ShapeDtypeStruct((b, N_EXPERTS, s), F32)],
        compiler_params=pltpu.CompilerParams(
            dimension_semantics=("arbitrary", "arbitrary"), vmem_limit_bytes=VMEM_LIMIT),
        name="out_proj",
    )(x, out_a, out_bt, u, u, u, bc, *consts)


def _topk_kernel(a_ref, tri_ref, o_slot, o_gate, o_off, *, cap):
    a = a_ref[0, 0]
    bits = pltpu.bitcast(a, jnp.int32)

    def total(mask):
        v = jnp.where(mask, 1.0, 0.0)
        return jnp.sum(jnp.sum(v, axis=0, keepdims=True), axis=1, keepdims=True)

    thr = jnp.zeros((1, 1), jnp.int32)
    for bit in range(30, -1, -1):
        cand = thr | (1 << bit)
        thr = jnp.where(total(bits >= cand) >= cap, cand, thr)
    gt = bits > thr
    eq = bits == thr
    need = cap - total(gt)

    upper = tri_ref[0]
    lower_strict = tri_ref[1]
    upper_strict = tri_ref[2]

    def excl_rank(mask):
        f = jnp.where(mask, 1.0, 0.0)
        fb = f.astype(BF16)
        in_row = _dot(fb, upper)
        before_rows = jnp.sum(_dot(lower_strict, fb), axis=1, keepdims=True)
        return before_rows + in_row - f, fb

    eq_rank, _ = excl_rank(eq)
    sel = gt | (eq & (eq_rank < need))
    pos, sel_b = excl_rank(sel)
    o_slot[0, 0] = jnp.where(sel, pos, -1.0).astype(jnp.int32)
    o_gate[0, 0] = jnp.where(sel, a, 0.0)
    row_tot = _dot_nt(jnp.ones((8, LANES), BF16), sel_b)
    o_off[0, 0] = _dot(row_tot.astype(BF16), upper_strict)


def _topk_call(aff_t, tri, cap):
    b, e, s = aff_t.shape
    pad = TOPK_ROWS * LANES - s
    a = aff_t if pad == 0 else jnp.pad(aff_t, ((0, 0), (0, 0), (0, pad)), constant_values=-1.0)
    a = a.reshape(b, e, TOPK_ROWS, LANES)
    blk = pl.BlockSpec((1, 1, TOPK_ROWS, LANES), lambda bi, ei: (bi, ei, 0, 0))
    return pl.pallas_call(
        functools.partial(_topk_kernel, cap=cap),
        grid=(b, e),
        in_specs=[blk, pl.BlockSpec(tri.shape, lambda bi, ei: (0, 0, 0))],
        out_specs=[blk, blk, pl.BlockSpec((1, 1, 8, LANES), lambda bi, ei: (bi, ei, 0, 0))],
        out_shape=[jax.ShapeDtypeStruct((b, e, TOPK_ROWS, LANES), jnp.int32),
                   jax.ShapeDtypeStruct((b, e, TOPK_ROWS, LANES), F32),
                   jax.ShapeDtypeStruct((b, e, 8, LANES), F32)],
        name="topk_select",
    )(a, tri)


def _window_onehot(slot_row, start):
    i = lax.broadcasted_iota(jnp.int32, (2 * MOE_CHUNK, MOE_CHUNK), 0)
    return jnp.where(i == slot_row - start, 1.0, 0.0).astype(BF16)


def _window_tile(off, tiles):
    return jnp.minimum(lax.shift_right_logical(off, 8), tiles - 1)


def _gather_kernel(coff_ref, x_ref, slot_ref, o_ref, acc_ref, *, n_chunks, cap):
    bi = pl.program_id(0)
    ei = pl.program_id(2)
    base = (bi * N_EXPERTS + ei) * (n_chunks + 1)
    acc_ref[...] = jnp.zeros_like(acc_ref)

    def body(k, carry):
        start = pl.multiple_of(_window_tile(coff_ref[base + k], cap // MOE_CHUNK) * MOE_CHUNK, MOE_CHUNK)
        g = _window_onehot(slot_ref[0, 0, k], start)
        xk = x_ref[0, pl.ds(pl.multiple_of(k * MOE_CHUNK, MOE_CHUNK), MOE_CHUNK), :]
        acc_ref[pl.ds(start, 2 * MOE_CHUNK), :] += _dot(g, xk)
        return carry

    lax.fori_loop(0, n_chunks, body, 0, unroll=2)
    o_ref[0, 0] = acc_ref[0:cap, :].astype(BF16)


def _gather_call(coff, xn, slot5, cap):
    b, s, d = xn.shape
    n_chunks = s // MOE_CHUNK
    ncq = d // GATHER_COLS
    grid_spec = pltpu.PrefetchScalarGridSpec(
        num_scalar_prefetch=1,
        grid=(b, ncq, N_EXPERTS),
        in_specs=[pl.BlockSpec((1, s, GATHER_COLS), lambda bi, cq, ei, co: (bi, 0, cq)),
                  pl.BlockSpec((1, 1, n_chunks, 1, MOE_CHUNK), lambda bi, cq, ei, co: (bi, ei, 0, 0, 0))],
        out_specs=pl.BlockSpec((1, 1, cap, GATHER_COLS), lambda bi, cq, ei, co: (bi, ei, 0, cq)),
        scratch_shapes=[pltpu.VMEM((cap + MOE_CHUNK, GATHER_COLS), F32)],
    )
    return pl.pallas_call(
        functools.partial(_gather_kernel, n_chunks=n_chunks, cap=cap),
        grid_spec=grid_spec,
        out_shape=jax.ShapeDtypeStruct((b, N_EXPERTS, cap, d), BF16),
        compiler_params=pltpu.CompilerParams(
            dimension_semantics=("arbitrary", "arbitrary", "arbitrary"), vmem_limit_bytes=VMEM_LIMIT),
        name="moe_gather",
    )(coff, xn, slot5)


def _ffn_kernel(xc_ref, wg_ref, wu_ref, wdt_ref, o_ref, *, tiles):
    xc = xc_ref[0, 0]
    g = _dot(xc, wg_ref[0])
    u = _dot(xc, wu_ref[0])
    h = (g / (1.0 + jnp.exp(-g)) * u).astype(BF16)
    yt = _dot_nt(wdt_ref[0], h)
    for r in range(tiles):
        o_ref[0, 0, r] = yt[:, r * MOE_CHUNK:(r + 1) * MOE_CHUNK].astype(BF16)


def _ffn_call(xc, wg, wu, wdt):
    b, e, cap, d = xc.shape
    ff = wg.shape[-1]
    rt = min(cap, 1024)
    tiles = rt // MOE_CHUNK
    return pl.pallas_call(
        functools.partial(_ffn_kernel, tiles=tiles),
        grid=(e, b, cap // rt),
        in_specs=[pl.BlockSpec((1, 1, rt, d), lambda ei, bi, ri: (bi, ei, ri, 0)),
                  pl.BlockSpec((1, d, ff), lambda ei, bi, ri: (ei, 0, 0)),
                  pl.BlockSpec((1, d, ff), lambda ei, bi, ri: (ei, 0, 0)),
                  pl.BlockSpec((1, d, ff), lambda ei, bi, ri: (ei, 0, 0))],
        out_specs=pl.BlockSpec((1, 1, tiles, d, MOE_CHUNK), lambda ei, bi, ri: (bi, ei, ri, 0, 0)),
        out_shape=jax.ShapeDtypeStruct((b, e, cap // MOE_CHUNK, d, MOE_CHUNK), BF16),
        compiler_params=pltpu.CompilerParams(
            dimension_semantics=("arbitrary", "arbitrary", "arbitrary"), vmem_limit_bytes=VMEM_LIMIT),
        name="moe_ffn",
    )(xc, wg, wu, wdt)


def _scatter_kernel(coff_ref, y_ref, slot_ref, gate_ref, o_ref, *, n_chunks, tiles):
    bi = pl.program_id(0)
    kg = pl.program_id(2)
    for j in range(SCATTER_CHUNKS):
        k = kg * SCATTER_CHUNKS + j
        acc = jnp.zeros((MOE_COLS, MOE_CHUNK), F32)
        for e in range(N_EXPERTS):
            r0 = _window_tile(coff_ref[(bi * N_EXPERTS + e) * (n_chunks + 1) + k], tiles)
            r1 = jnp.minimum(r0 + 1, tiles - 1)
            g = _window_onehot(slot_ref[0, e, j], r0 * MOE_CHUNK)
            yw = jnp.concatenate([y_ref[0, e, r0], y_ref[0, e, r1]], axis=1)
            acc = acc + _dot(yw, g) * gate_ref[0, e, j]
        o_ref[0, 0, j] = acc


def _scatter_call(coff, y5, slot5, gate5):
    b, e, tiles, d, _ = y5.shape
    n_chunks = slot5.shape[2]
    ncq = d // MOE_COLS
    small = pl.BlockSpec((1, e, SCATTER_CHUNKS, 1, MOE_CHUNK), lambda bi, cq, kg, co: (bi, 0, kg, 0, 0))
    grid_spec = pltpu.PrefetchScalarGridSpec(
        num_scalar_prefetch=1,
        grid=(b, ncq, n_chunks // SCATTER_CHUNKS),
        in_specs=[pl.BlockSpec((1, e, tiles, MOE_COLS, MOE_CHUNK), lambda bi, cq, kg, co: (bi, 0, 0, cq, 0)),
                  small, small],
        out_specs=pl.BlockSpec((1, 1, SCATTER_CHUNKS, MOE_COLS, MOE_CHUNK),
                               lambda bi, cq, kg, co: (bi, cq, kg, 0, 0)),
    )
    return pl.pallas_call(
        functools.partial(_scatter_kernel, n_chunks=n_chunks, tiles=tiles),
        grid_spec=grid_spec,
        out_shape=jax.ShapeDtypeStruct((b, ncq, n_chunks, MOE_COLS, MOE_CHUNK), F32),
        compiler_params=pltpu.CompilerParams(
            dimension_semantics=("arbitrary", "arbitrary", "arbitrary"), vmem_limit_bytes=VMEM_LIMIT),
        name="moe_scatter",
    )(coff, y5, slot5, gate5)


def _final_kernel(x_ref, moe_ref, o_ref):
    o_ref[0] = x_ref[0] + _moe_tile_to_rows(moe_ref, TOK_TILE // MOE_CHUNK)


def _final_call(x1, moe):
    b, s, d = x1.shape
    ncq = d // MOE_COLS
    tok = pl.BlockSpec((1, TOK_TILE, d), lambda bi, ti: (bi, ti, 0))
    return pl.pallas_call(
        _final_kernel,
        grid=(b, s // TOK_TILE),
        in_specs=[tok, pl.BlockSpec((1, ncq, TOK_TILE // MOE_CHUNK, MOE_COLS, MOE_CHUNK),
                                    lambda bi, ti: (bi, 0, ti, 0, 0))],
        out_specs=tok,
        out_shape=jax.ShapeDtypeStruct((b, s, d), F32),
        compiler_params=pltpu.CompilerParams(
            dimension_semantics=("arbitrary", "arbitrary"), vmem_limit_bytes=VMEM_LIMIT),
        name="final_add",
    )(x1, moe)


def _block_ones(n, group):
    i = np.arange(n)
    return jnp.asarray((i[:, None] // group) == (i[None, :] // group), BF16)


def _tables(s):
    half = MLA_ROPE // 2
    inv = ROPE_THETA ** (-jnp.arange(half, dtype=F32) / half)
    ang = jnp.arange(s, dtype=jnp.int32).astype(F32)[:, None] * inv[None, :]
    cos, sin = jnp.cos(ang), jnp.sin(ang)
    ones = jnp.ones((s, MLA_NOPE), F32)
    tail = HEAD_PAD - MLA_QK
    cos_t = jnp.concatenate([ones, cos, cos, jnp.ones((s, tail), F32)], axis=1)
    sin_t = jnp.concatenate([0 * ones, -sin, sin, jnp.zeros((s, tail), F32)], axis=1)
    i = np.arange(LANES)
    tri = np.stack([i[:, None] <= i[None, :], i[None, :] < i[:, None], i[:, None] < i[None, :]])
    return {'cos': cos_t, 'sin': sin_t, 'e384': _block_ones(2 * LANES, HEAD_DIM),
            'e768': _block_ones(2 * LANES, HEAD_PAD), 'tri': jnp.asarray(tri, BF16)}


def _pad_heads(v, scale=1.0):
    g = jnp.pad(v.astype(F32) * scale, (0, HEAD_PAD - MLA_QK))
    return jnp.tile(g, MLA_HEADS)[None, :]


def _layer_weights(l, p):
    w_in = p['w_in'][l]
    kr_end = 3 * A_WIDTH + MLA_Q_RANK + MLA_KV_RANK + MLA_ROPE
    w_in_r = jnp.concatenate(
        [w_in[:, :kr_end], jnp.zeros((D_MODEL, LANES - MLA_ROPE), F32), w_in[:, kr_end:]], axis=1)
    w_uq = p['w_uq'][l].reshape(MLA_Q_RANK, MLA_HEADS, MLA_QK)
    w_uq = jnp.pad(w_uq, ((0, 0), (0, 0), (0, HEAD_PAD - MLA_QK))).reshape(MLA_Q_RANK, QB_WIDTH)
    w_ukv = p['w_ukv'][l].reshape(MLA_KV_RANK, MLA_HEADS, MLA_NOPE + MLA_V)
    w_uk = jnp.pad(w_ukv[:, :, :MLA_NOPE], ((0, 0), (0, 0), (0, HEAD_PAD - MLA_NOPE)))
    w_uk = w_uk.reshape(MLA_KV_RANK, QB_WIDTH)
    w_uv = w_ukv[:, :, MLA_NOPE:].reshape(MLA_KV_RANK, B_WIDTH)
    place = np.zeros((LANES, QB_WIDTH), np.float32)
    for h in range(MLA_HEADS):
        place[np.arange(MLA_ROPE), h * HEAD_PAD + MLA_NOPE + np.arange(MLA_ROPE)] = 1.0
    w_kk = jnp.concatenate([w_uk, jnp.asarray(place)], axis=0)
    out_norm = p['out_norm'][l]
    w_out = p['w_out'][l]
    w_r = jnp.pad(p['w_router'][l], ((0, 0), (0, LANES - N_EXPERTS)))
    wr_hi = w_r.astype(BF16)
    b_r = jnp.concatenate([p['b_router'][l].astype(F32), jnp.full((LANES - N_EXPERTS,), NEG_BIG, F32)])
    return {
        'nmix': p['norm_mix'][l][None, :], 'w_in': w_in_r.astype(BF16),
        'gqa': jnp.tile(p['q_norm_a'][l] * HEAD_DIM ** -0.5, NA_HEADS)[None, :],
        'gka': jnp.tile(p['k_norm_a'][l], NA_HEADS)[None, :],
        'cqn': p['cq_norm'][l][None, :], 'w_uq': w_uq.astype(BF16),
        'ckvn': p['ckv_norm'][l][None, :], 'w_kk': w_kk.astype(BF16), 'w_uv': w_uv.astype(BF16),
        'gqb': _pad_heads(p['q_norm_b'][l], MLA_QK ** -0.5 * LOG2_E), 'gkb': _pad_heads(p['k_norm_b'][l]),
        'ga': out_norm[None, :A_WIDTH], 'gb_col': out_norm[A_WIDTH:A_WIDTH + B_WIDTH, None],
        'gc': out_norm[None, A_WIDTH + B_WIDTH:],
        'wo_a': w_out[:A_WIDTH].astype(BF16), 'wo_b': w_out[A_WIDTH:A_WIDTH + B_WIDTH].astype(BF16),
        'wo_c': w_out[A_WIDTH + B_WIDTH:].astype(BF16),
        'conv_w': p['conv_w'][l], 'nffn': p['norm_ffn'][l][None, :],
        'wr_hi': wr_hi, 'wr_lo': (w_r - wr_hi.astype(F32)).astype(BF16), 'b_r': b_r[None, :],
        'w_gate': p['w_gate'][l].astype(BF16), 'w_up': p['w_up'][l].astype(BF16),
        'w_down_t': jnp.swapaxes(p['w_down'][l], 1, 2).astype(BF16),
    }


def _layer(x, moe, lw, tabs, rpb_l):
    b, s, d = x.shape
    cap = CAPACITY_FACTOR * s // N_EXPERTS
    n_chunks = s // MOE_CHUNK

    outs = _proj_call(x, moe, lw, tabs)
    if moe is not None:
        qa, ka, va, qb, kb, vbt, u, bc, x = outs
    else:
        qa, ka, va, qb, kb, vbt, u, bc = outs
    out_a = _na_call(qa, ka, va, _bias_call(rpb_l))
    out_bt = _mla_call(qb, kb, vbt)
    x1, xn, aff_t = _out_call(x, out_a, out_bt, u, bc, lw)

    slot, gate, off = _topk_call(aff_t, tabs['tri'], cap)
    rows_per_chunk = MOE_CHUNK // LANES
    coff = off[:, :, 0, :s // LANES:rows_per_chunk].astype(jnp.int32)
    coff = jnp.concatenate([coff, jnp.full((b, N_EXPERTS, 1), cap, jnp.int32)], axis=-1).reshape(-1)
    slot5 = slot.reshape(b, N_EXPERTS, -1)[:, :, :s].reshape(b, N_EXPERTS, n_chunks, 1, MOE_CHUNK)
    gate5 = gate.reshape(b, N_EXPERTS, -1)[:, :, :s].reshape(b, N_EXPERTS, n_chunks, 1, MOE_CHUNK)

    xc = _gather_call(coff, xn, slot5, cap)
    y5 = _ffn_call(xc, lw['w_gate'], lw['w_up'], lw['w_down_t'])
    moe_out = _scatter_call(coff, y5, slot5, gate5)
    return x1, moe_out


def kernel(x, norm_mix, w_in, q_norm_a, k_norm_a, rpb, cq_norm, w_uq, ckv_norm, w_ukv, q_norm_b,
           k_norm_b, conv_w, out_norm, w_out, norm_ffn, w_router, b_router, w_gate, w_up, w_down):
    p = dict(norm_mix=norm_mix, w_in=w_in, q_norm_a=q_norm_a, k_norm_a=k_norm_a, cq_norm=cq_norm,
             w_uq=w_uq, ckv_norm=ckv_norm, w_ukv=w_ukv, q_norm_b=q_norm_b, k_norm_b=k_norm_b,
             conv_w=conv_w, out_norm=out_norm, w_out=w_out, norm_ffn=norm_ffn, w_router=w_router,
             b_router=b_router, w_gate=w_gate, w_up=w_up, w_down=w_down)
    b, s, d = x.shape
    assert d == D_MODEL and (CAPACITY_FACTOR * s // N_EXPERTS) % MOE_CHUNK == 0
    assert s % TOK_TILE == 0 and (s // GRID_W) % NA_ROWS == 0 and s <= TOPK_ROWS * LANES
    tabs = _tables(s)
    moe = None
    for l in range(norm_mix.shape[0]):
        x, moe = _layer(x, moe, _layer_weights(l, p), tabs, rpb[l])
    return _final_call(x, moe)
```

```python
import functools

import numpy as np
import jax
import jax.numpy as jnp
from jax import lax
from jax.experimental import pallas as pl
from jax.experimental.pallas import tpu as pltpu

D_MODEL = 1024
GRID_W = 64
HEAD_DIM = 64
EPS = 1e-6

NA_HEADS = 6
NA_WIN_H = 8
NA_WIN_W = 16
A_WIDTH = NA_HEADS * HEAD_DIM

MLA_HEADS = 6
MLA_Q_RANK = 256
MLA_KV_RANK = 128
MLA_NOPE = 64
MLA_ROPE = 32
MLA_QK = MLA_NOPE + MLA_ROPE
MLA_V = 64
B_WIDTH = MLA_HEADS * MLA_V
ROPE_THETA = 10000.0

CONV_CH = 256
C_WIDTH = CONV_CH

N_EXPERTS = 16
EXPERT_FF = 1024
CAPACITY_FACTOR = 2

LANES = 128
HEAD_PAD = 128
QB_WIDTH = MLA_HEADS * HEAD_PAD
IN_COLS_PAD = 2432
TOK_TILE = 512
NA_ROWS = 8
MOE_CHUNK = 256
MOE_COLS = 256
GATHER_COLS = 512
SCATTER_CHUNKS = 2
TOPK_ROWS = 128
MLA_VROWS = MLA_V + 16
LOG2_E = 1.4426950408889634
MLA_UNROLL = 16
NEG_BIG = -1e30
VMEM_LIMIT = 56 * 1024 * 1024

F32 = jnp.float32
BF16 = jnp.bfloat16


def _dot(a, b):
    return jnp.dot(a, b, preferred_element_type=F32)


def _dot_nt(a, b):
    return lax.dot_general(a, b, (((1,), (1,)), ((), ())), preferred_element_type=F32)


def _rms(x, g):
    return x * lax.rsqrt(jnp.mean(x * x, axis=-1, keepdims=True) + EPS) * g


def _group_sumsq(y, e_ref):
    sq = (y * y).astype(BF16)
    width = y.shape[1]
    parts = []
    for c in range(0, width, 2 * LANES):
        n = min(2 * LANES, width - c)
        parts.append(_dot(sq[:, c:c + n], e_ref[0:n, 0:n]))
    return jnp.concatenate(parts, axis=1)


def _moe_tile_to_rows(moe_ref, n_chunks):
    rows = []
    for ch in range(n_chunks):
        cols = [moe_ref[0, cq, ch].T for cq in range(D_MODEL // MOE_COLS)]
        rows.append(jnp.concatenate(cols, axis=1))
    return jnp.concatenate(rows, axis=0)


def _proj_kernel(*refs, has_moe):
    if has_moe:
        x_ref, moe_ref = refs[0], refs[1]
        refs = refs[2:]
    else:
        x_ref = refs[0]
        refs = refs[1:]
    (nmix_ref, win_ref, gqa_ref, gka_ref, e384_ref, cqn_ref, wuq_ref, ckvn_ref, wkk_ref,
     wuv_ref, gqb_ref, gkb_ref, e768_ref, cos_ref, sin_ref) = refs[:15]
    outs = refs[15:]
    if has_moe:
        qa_o, ka_o, va_o, qb_o, kb_o, vbt_o, u_o, bc_o, xs_o = outs
    else:
        qa_o, ka_o, va_o, qb_o, kb_o, vbt_o, u_o, bc_o = outs

    x = x_ref[0]
    if has_moe:
        x = x + _moe_tile_to_rows(moe_ref, TOK_TILE // MOE_CHUNK)
        xs_o[0] = x
    xb = _rms(x, nmix_ref[...]).astype(BF16)

    pa = _dot(xb, win_ref[:, 0:3 * A_WIDTH])
    qa = pa[:, 0:A_WIDTH]
    ka = pa[:, A_WIDTH:2 * A_WIDTH]
    qa_o[0] = (qa * lax.rsqrt(_group_sumsq(qa, e384_ref) * (1.0 / HEAD_DIM) + EPS) * gqa_ref[...]).astype(BF16)
    ka_o[0] = (ka * lax.rsqrt(_group_sumsq(ka, e384_ref) * (1.0 / HEAD_DIM) + EPS) * gka_ref[...]).astype(BF16)
    va_o[0] = pa[:, 2 * A_WIDTH:3 * A_WIDTH].astype(BF16)

    c0 = 3 * A_WIDTH
    pb = _dot(xb, win_ref[:, c0:c0 + 512])
    cq = _rms(pb[:, 0:MLA_Q_RANK], cqn_ref[...]).astype(BF16)
    ckv = _rms(pb[:, MLA_Q_RANK:MLA_Q_RANK + MLA_KV_RANK], ckvn_ref[...]).astype(BF16)
    kr = pb[:, MLA_Q_RANK + MLA_KV_RANK:512].astype(BF16)
    qb_raw = _dot(cq, wuq_ref[...])
    kb_raw = _dot(jnp.concatenate([ckv, kr], axis=1), wkk_ref[...])
    vb = _dot(ckv, wuv_ref[...])

    cos = jnp.concatenate([cos_ref[...]] * MLA_HEADS, axis=1)
    sin = jnp.concatenate([sin_ref[...]] * MLA_HEADS, axis=1)
    lane = lax.broadcasted_iota(jnp.int32, (TOK_TILE, QB_WIDTH), 1) & (HEAD_PAD - 1)
    first_half = (lane >= MLA_NOPE) & (lane < MLA_NOPE + MLA_ROPE // 2)

    def norm_rope(y, g_ref):
        y = y * lax.rsqrt(_group_sumsq(y, e768_ref) * (1.0 / MLA_QK) + EPS) * g_ref[...]
        rot = jnp.where(first_half,
                        pltpu.roll(y, QB_WIDTH - MLA_ROPE // 2, 1),
                        pltpu.roll(y, MLA_ROPE // 2, 1))
        return y * cos + rot * sin

    qb_o[0] = norm_rope(qb_raw, gqb_ref).T.astype(BF16)
    kb_o[0] = norm_rope(kb_raw, gkb_ref).astype(BF16)
    vbt = vb.T
    for h in range(MLA_HEADS):
        vbt_o[0, h, 0, 0:MLA_V, :] = vbt[h * MLA_V:(h + 1) * MLA_V, :].astype(BF16)
        vbt_o[0, h, 0, MLA_V:MLA_VROWS, :] = jnp.ones((MLA_VROWS - MLA_V, TOK_TILE), BF16)

    c1 = c0 + 512
    pc = _dot(xb, win_ref[:, c1:c1 + 3 * CONV_CH])
    u_o[0] = pc[:, 2 * CONV_CH:3 * CONV_CH] * pc[:, 0:CONV_CH]
    bc_o[0] = pc[:, CONV_CH:2 * CONV_CH]


def _proj_call(x, moe, lw, tabs):
    b, s, d = x.shape
    nt = s // TOK_TILE
    has_moe = moe is not None

    def const(shape):
        return pl.BlockSpec(shape, lambda bi, ti: (0,) * len(shape))

    in_specs = [pl.BlockSpec((1, TOK_TILE, d), lambda bi, ti: (bi, ti, 0))]
    args = [x]
    if has_moe:
        ncq = D_MODEL // MOE_COLS
        in_specs.append(pl.BlockSpec((1, ncq, TOK_TILE // MOE_CHUNK, MOE_COLS, MOE_CHUNK),
                                     lambda bi, ti: (bi, 0, ti, 0, 0)))
        args.append(moe)
    consts = [lw['nmix'], lw['w_in'], lw['gqa'], lw['gka'], tabs['e384'], lw['cqn'], lw['w_uq'],
              lw['ckvn'], lw['w_kk'], lw['w_uv'], lw['gqb'], lw['gkb'], tabs['e768']]
    for c in consts:
        in_specs.append(const(c.shape))
        args.append(c)
    in_specs += [pl.BlockSpec((TOK_TILE, HEAD_PAD), lambda bi, ti: (ti, 0))] * 2
    args += [tabs['cos'], tabs['sin']]

    def tok(width, dtype):
        return (jax.ShapeDtypeStruct((b, s, width), dtype),
                pl.BlockSpec((1, TOK_TILE, width), lambda bi, ti: (bi, ti, 0)))

    outs = [tok(A_WIDTH, BF16), tok(A_WIDTH, BF16), tok(A_WIDTH, BF16),
            (jax.ShapeDtypeStruct((b, QB_WIDTH, s), BF16),
             pl.BlockSpec((1, QB_WIDTH, TOK_TILE), lambda bi, ti: (bi, 0, ti))),
            tok(QB_WIDTH, BF16),
            (jax.ShapeDtypeStruct((b, MLA_HEADS, nt, MLA_VROWS, TOK_TILE), BF16),
             pl.BlockSpec((1, MLA_HEADS, 1, MLA_VROWS, TOK_TILE), lambda bi, ti: (bi, 0, ti, 0, 0))),
            tok(CONV_CH, F32), tok(CONV_CH, F32)]
    if has_moe:
        outs.append(tok(d, F32))
    return pl.pallas_call(
        functools.partial(_proj_kernel, has_moe=has_moe),
        grid=(b, nt),
        in_specs=in_specs,
        out_specs=[o[1] for o in outs],
        out_shape=[o[0] for o in outs],
        compiler_params=pltpu.CompilerParams(
            dimension_semantics=("arbitrary", "arbitrary"), vmem_limit_bytes=VMEM_LIMIT),
        name="proj_moe" if has_moe else "proj",
    )(*args)


def _bias_kernel(rpb_ref, o_ref):
    h = pl.program_id(0)
    delta = pl.program_id(1)
    nw = 2 * NA_WIN_W - 1
    shape = (GRID_W, NA_WIN_H * GRID_W)
    c = lax.broadcasted_iota(jnp.int32, shape, 0)
    col = lax.broadcasted_iota(jnp.int32, shape, 1)
    wrow = lax.shift_right_logical(col, 6)
    kc = col & (GRID_W - 1)
    c_lo = jnp.clip(c - NA_WIN_W // 2, 0, GRID_W - NA_WIN_W)
    inside = (kc >= c_lo) & (kc < c_lo + NA_WIN_W)
    d = kc - c + (NA_WIN_W - 1)
    acc = jnp.zeros(shape, F32)
    base = h * ((2 * NA_WIN_H - 1) * nw)
    for i in range(NA_WIN_H):
        ri = i - delta + (NA_WIN_H - 1)
        for dd in range(nw):
            val = rpb_ref[base + ri * nw + dd]
            acc = jnp.where((wrow == i) & (d == dd), val, acc)
    o_ref[0, 0] = jnp.where(inside, acc, NEG_BIG)


def _bias_call(rpb_l):
    flat = rpb_l.reshape(-1)
    return pl.pallas_call(
        _bias_kernel,
        grid=(NA_HEADS, NA_WIN_H),
        in_specs=[pl.BlockSpec(memory_space=pltpu.SMEM)],
        out_specs=pl.BlockSpec((1, 1, GRID_W, NA_WIN_H * GRID_W), lambda h, dl: (h, dl, 0, 0)),
        out_shape=jax.ShapeDtypeStruct((NA_HEADS, NA_WIN_H, GRID_W, NA_WIN_H * GRID_W), F32),
        name="na_bias",
    )(flat)


def _na_kernel(q_ref, kp_ref, kc_ref, kn_ref, vp_ref, vc_ref, vn_ref, bias_ref, o_ref,
               k_scr, v_scr, *, rows):
    j = pl.program_id(1)
    blk = NA_ROWS * GRID_W
    k_scr[0:blk] = kp_ref[0]
    k_scr[blk:2 * blk] = kc_ref[0]
    k_scr[2 * blk:3 * blk] = kn_ref[0]
    v_scr[0:blk] = vp_ref[0]
    v_scr[blk:2 * blk] = vc_ref[0]
    v_scr[2 * blk:3 * blk] = vn_ref[0]
    lane = lax.broadcasted_iota(jnp.int32, (GRID_W, LANES), 1)
    low = lane < HEAD_DIM
    win = NA_WIN_H * GRID_W

    def window_start(i):
        r = NA_ROWS * j + i
        r0 = jnp.clip(r - NA_WIN_H // 2, 0, rows - NA_WIN_H)
        return r - r0, pl.multiple_of((r0 - NA_ROWS * j + NA_ROWS) * GRID_W, GRID_W)

    tiles = []
    for i in range(NA_ROWS):
        delta, start = window_start(i)
        for p in range(NA_HEADS // 2):
            lsl = slice(p * LANES, (p + 1) * LANES)
            qp = q_ref[0, i * GRID_W:(i + 1) * GRID_W, lsl]
            kw = k_scr[pl.ds(start, win), lsl]
            for a in range(2):
                qm = jnp.where(low if a == 0 else jnp.logical_not(low), qp, jnp.zeros_like(qp))
                tiles.append(_dot_nt(qm, kw) + bias_ref[2 * p + a, delta])
    sc = jnp.concatenate(tiles, axis=0)
    e = jnp.exp(sc - jnp.max(sc, axis=-1, keepdims=True))
    inv = 1.0 / jnp.sum(e, axis=-1, keepdims=True)
    eb = e.astype(BF16)
    t = 0
    for i in range(NA_ROWS):
        _, start = window_start(i)
        for p in range(NA_HEADS // 2):
            lsl = slice(p * LANES, (p + 1) * LANES)
            vw = v_scr[pl.ds(start, win), lsl]
            res = []
            for a in range(2):
                rsl = slice(t * GRID_W, (t + 1) * GRID_W)
                res.append(_dot(eb[rsl], vw) * inv[rsl])
                t += 1
            o_ref[0, i * GRID_W:(i + 1) * GRID_W, lsl] = jnp.where(low, res[0], res[1])


def _na_call(qa, ka, va, bias_tab):
    b, s, w = qa.shape
    rows = s // GRID_W
    nb = rows // NA_ROWS
    blk = NA_ROWS * GRID_W
    cur = pl.BlockSpec((1, blk, w), lambda bi, j: (bi, j, 0))
    prev = pl.BlockSpec((1, blk, w), lambda bi, j: (bi, jnp.maximum(j - 1, 0), 0))
    nxt = pl.BlockSpec((1, blk, w), lambda bi, j: (bi, jnp.minimum(j + 1, nb - 1), 0))
    return pl.pallas_call(
        functools.partial(_na_kernel, rows=rows),
        grid=(b, nb),
        in_specs=[cur, prev, cur, nxt, prev, cur, nxt,
                  pl.BlockSpec(bias_tab.shape, lambda bi, j: (0, 0, 0, 0))],
        out_specs=pl.BlockSpec((1, blk, w), lambda bi, j: (bi, j, 0)),
        out_shape=jax.ShapeDtypeStruct((b, s, w), F32),
        scratch_shapes=[pltpu.VMEM((3 * blk, w), BF16), pltpu.VMEM((3 * blk, w), BF16)],
        compiler_params=pltpu.CompilerParams(
            dimension_semantics=("arbitrary", "arbitrary"), vmem_limit_bytes=VMEM_LIMIT),
        name="na_attn",
    )(qa, ka, ka, ka, va, va, va, bias_tab)


def _mla_kernel(q_ref, k_ref, vt_ref, o_ref, s_scr, acc_scr, m_scr, cmax_scr, *, n_chunks):
    qt = q_ref[0]

    def scores(c, slot):
        kb = k_ref[0, pl.ds(pl.multiple_of(c * TOK_TILE, TOK_TILE), TOK_TILE), :]
        st = _dot(kb, qt)
        s_scr[slot] = st
        cmax_scr[slot] = jnp.max(st, axis=0, keepdims=True)

    def update(slot, c):
        m = m_scr[...]
        m_new = jnp.maximum(m, cmax_scr[slot])
        alpha = jnp.exp2(m - m_new)
        p = jnp.exp2(s_scr[slot] - m_new).astype(BF16)
        acc_scr[...] = alpha * acc_scr[...] + _dot(vt_ref[0, 0, c], p)
        m_scr[...] = m_new

    m_scr[...] = jnp.full(m_scr.shape, NEG_BIG, F32)
    acc_scr[...] = jnp.zeros(acc_scr.shape, F32)
    scores(0, 0)

    def body(i, carry):
        for j in range(MLA_UNROLL):
            c = MLA_UNROLL * i + j
            scores(jnp.minimum(c + 1, n_chunks - 1), (j + 1) % 2)
            update(j % 2, c)
        return carry

    lax.fori_loop(0, n_chunks // MLA_UNROLL, body, 0)
    acc = acc_scr[...]
    o_ref[0] = acc[0:MLA_V, :] / acc[MLA_V:MLA_V + 1, :]


def _mla_call(qb, kb, vbt):
    b, s, _ = kb.shape
    nt = s // TOK_TILE
    assert nt % MLA_UNROLL == 0 and MLA_UNROLL % 2 == 0
    return pl.pallas_call(
        functools.partial(_mla_kernel, n_chunks=nt),
        grid=(b, MLA_HEADS, nt),
        in_specs=[pl.BlockSpec((1, HEAD_PAD, TOK_TILE), lambda bi, h, qi: (bi, h, qi)),
                  pl.BlockSpec((1, s, HEAD_PAD), lambda bi, h, qi: (bi, 0, h)),
                  pl.BlockSpec((1, 1, nt, MLA_VROWS, TOK_TILE), lambda bi, h, qi: (bi, h, 0, 0, 0))],
        out_specs=pl.BlockSpec((1, MLA_V, TOK_TILE), lambda bi, h, qi: (bi, h, qi)),
        out_shape=jax.ShapeDtypeStruct((b, B_WIDTH, s), F32),
        scratch_shapes=[pltpu.VMEM((2, TOK_TILE, TOK_TILE), F32), pltpu.VMEM((MLA_VROWS, TOK_TILE), F32),
                        pltpu.VMEM((1, TOK_TILE), F32), pltpu.VMEM((2, 1, TOK_TILE), F32)],
        compiler_params=pltpu.CompilerParams(
            dimension_semantics=("arbitrary", "arbitrary", "arbitrary"), vmem_limit_bytes=VMEM_LIMIT),
        name="mla_attn",
    )(qb, kb, vbt)


def _out_kernel(x_ref, oa_ref, obt_ref, u_ref, up_ref, un_ref, bc_ref, ga_ref, gb_ref, gc_ref,
                woa_ref, wob_ref, woc_ref, cw_ref, nffn_ref, wrh_ref, wrl_ref, br_ref,
                x1_o, xn_o, aff_o):
    ti = pl.program_id(1)
    nt = pl.num_programs(1)
    t = TOK_TILE

    na = _rms(oa_ref[0], ga_ref[...]).astype(BF16)
    acc = _dot(na, woa_ref[...])

    obt = obt_ref[0]
    nbt = obt * lax.rsqrt(jnp.mean(obt * obt, axis=0, keepdims=True) + EPS) * gb_ref[...]
    acc = acc + _dot(nbt.T.astype(BF16), wob_ref[...])

    u = u_ref[0]
    row = lax.broadcasted_iota(jnp.int32, (t, CONV_CH), 0)
    prev_row = jnp.where(ti > 0, up_ref[0, 7:8, :], 0.0)
    next_row = jnp.where(ti < nt - 1, un_ref[0, 0:1, :], 0.0)
    u_m1 = jnp.where(row == 0, prev_row, pltpu.roll(u, 1, 0))
    u_p1 = jnp.where(row == t - 1, next_row, pltpu.roll(u, t - 1, 0))
    y = cw_ref[0:1, :] * u_m1 + cw_ref[1:2, :] * u + cw_ref[2:3, :] * u_p1
    oc = _rms(bc_ref[0] * y, gc_ref[...]).astype(BF16)
    acc = acc + _dot(oc, woc_ref[...])

    x1 = x_ref[0] + acc
    x1_o[0] = x1
    xn = _rms(x1, nffn_ref[...])
    hi = xn.astype(BF16)
    lo = (xn - hi.astype(F32)).astype(BF16)
    xn_o[0] = hi
    logits = _dot(hi, wrh_ref[...]) + _dot(lo, wrh_ref[...]) + _dot(hi, wrl_ref[...]) + br_ref[...]
    m = jnp.max(logits, axis=-1, keepdims=True)
    e = jnp.exp(logits - m)
    aff = e / jnp.sum(e, axis=-1, keepdims=True)
    aff_o[0] = aff.T[0:N_EXPERTS, :]


def _out_call(x, out_a, out_bt, u, bc, lw):
    b, s, d = x.shape
    nt = s // TOK_TILE
    h8 = TOK_TILE // 8

    def const(a):
        return pl.BlockSpec(a.shape, lambda bi, ti: (0,) * a.ndim)

    def tok(width):
        return pl.BlockSpec((1, TOK_TILE, width), lambda bi, ti: (bi, ti, 0))

    consts = [lw['ga'], lw['gb_col'], lw['gc'], lw['wo_a'], lw['wo_b'], lw['wo_c'], lw['conv_w'],
              lw['nffn'], lw['wr_hi'], lw['wr_lo'], lw['b_r']]
    in_specs = [tok(d), tok(A_WIDTH),
                pl.BlockSpec((1, B_WIDTH, TOK_TILE), lambda bi, ti: (bi, 0, ti)),
                tok(CONV_CH),
                pl.BlockSpec((1, 8, CONV_CH), lambda bi, ti: (bi, jnp.maximum(ti * h8 - 1, 0), 0)),
                pl.BlockSpec((1, 8, CONV_CH), lambda bi, ti: (bi, jnp.minimum((ti + 1) * h8, s // 8 - 1), 0)),
                tok(CONV_CH)] + [const(c) for c in consts]
    return pl.pallas_call(
        _out_kernel,
        grid=(b, nt),
        in_specs=in_specs,
        out_specs=[tok(d), tok(d), pl.BlockSpec((1, N_EXPERTS, TOK_TILE), lambda bi, ti: (bi, 0, ti))],
        out_shape=[jax.ShapeDtypeStruct((b, s, d), F32), jax.ShapeDtypeStruct((b, s, d), BF16),
                   jax.ShapeDtypeStruct((b, N_EXPERTS, s), F32)],
        compiler_params=pltpu.CompilerParams(
            dimension_semantics=("arbitrary", "arbitrary"), vmem_limit_bytes=VMEM_LIMIT),
        name="out_proj",
    )(x, out_a, out_bt, u, u, u, bc, *consts)


def _topk_kernel(a_ref, tri_ref, o_slot, o_gate, o_off, *, cap):
    a = a_ref[0, 0]
    bits = pltpu.bitcast(a, jnp.int32)

    def total(mask):
        v = jnp.where(mask, 1.0, 0.0)
        return jnp.sum(jnp.sum(v, axis=0, keepdims=True), axis=1, keepdims=True)

    thr = jnp.zeros((1, 1), jnp.int32)
    for bit in range(30, -1, -1):
        cand = thr | (1 << bit)
        thr = jnp.where(total(bits >= cand) >= cap, cand, thr)
    gt = bits > thr
    eq = bits == thr
    need = cap - total(gt)

    upper = tri_ref[0]
    lower_strict = tri_ref[1]
    upper_strict = tri_ref[2]

    def excl_rank(mask):
        f = jnp.where(mask, 1.0, 0.0)
        fb = f.astype(BF16)
        in_row = _dot(fb, upper)
        before_rows = jnp.sum(_dot(lower_strict, fb), axis=1, keepdims=True)
        return before_rows + in_row - f, fb

    eq_rank, _ = excl_rank(eq)
    sel = gt | (eq & (eq_rank < need))
    pos, sel_b = excl_rank(sel)
    o_slot[0, 0] = jnp.where(sel, pos, -1.0).astype(jnp.int32)
    o_gate[0, 0] = jnp.where(sel, a, 0.0)
    row_tot = _dot_nt(jnp.ones((8, LANES), BF16), sel_b)
    o_off[0, 0] = _dot(row_tot.astype(BF16), upper_strict)


def _topk_call(aff_t, tri, cap):
    b, e, s = aff_t.shape
    pad = TOPK_ROWS * LANES - s
    a = aff_t if pad == 0 else jnp.pad(aff_t, ((0, 0), (0, 0), (0, pad)), constant_values=-1.0)
    a = a.reshape(b, e, TOPK_ROWS, LANES)
    blk = pl.BlockSpec((1, 1, TOPK_ROWS, LANES), lambda bi, ei: (bi, ei, 0, 0))
    return pl.pallas_call(
        functools.partial(_topk_kernel, cap=cap),
        grid=(b, e),
        in_specs=[blk, pl.BlockSpec(tri.shape, lambda bi, ei: (0, 0, 0))],
        out_specs=[blk, blk, pl.BlockSpec((1, 1, 8, LANES), lambda bi, ei: (bi, ei, 0, 0))],
        out_shape=[jax.ShapeDtypeStruct((b, e, TOPK_ROWS, LANES), jnp.int32),
                   jax.ShapeDtypeStruct((b, e, TOPK_ROWS, LANES), F32),
                   jax.ShapeDtypeStruct((b, e, 8, LANES), F32)],
        name="topk_select",
    )(a, tri)


def _window_onehot(slot_row, start):
    i = lax.broadcasted_iota(jnp.int32, (2 * MOE_CHUNK, MOE_CHUNK), 0)
    return jnp.where(i == slot_row - start, 1.0, 0.0).astype(BF16)


def _window_tile(off, tiles):
    return jnp.minimum(lax.shift_right_logical(off, 8), tiles - 1)


def _gather_kernel(coff_ref, x_ref, slot_ref, o_ref, acc_ref, *, n_chunks, cap):
    bi = pl.program_id(0)
    ei = pl.program_id(2)
    base = (bi * N_EXPERTS + ei) * (n_chunks + 1)
    acc_ref[...] = jnp.zeros_like(acc_ref)

    def body(k, carry):
        start = pl.multiple_of(_window_tile(coff_ref[base + k], cap // MOE_CHUNK) * MOE_CHUNK, MOE_CHUNK)
        g = _window_onehot(slot_ref[0, 0, k], start)
        xk = x_ref[0, pl.ds(pl.multiple_of(k * MOE_CHUNK, MOE_CHUNK), MOE_CHUNK), :]
        acc_ref[pl.ds(start, 2 * MOE_CHUNK), :] += _dot(g, xk)
        return carry

    lax.fori_loop(0, n_chunks, body, 0, unroll=2)
    o_ref[0, 0] = acc_ref[0:cap, :].astype(BF16)


def _gather_call(coff, xn, slot5, cap):
    b, s, d = xn.shape
    n_chunks = s // MOE_CHUNK
    ncq = d // GATHER_COLS
    grid_spec = pltpu.PrefetchScalarGridSpec(
        num_scalar_prefetch=1,
        grid=(b, ncq, N_EXPERTS),
        in_specs=[pl.BlockSpec((1, s, GATHER_COLS), lambda bi, cq, ei, co: (bi, 0, cq)),
                  pl.BlockSpec((1, 1, n_chunks, 1, MOE_CHUNK), lambda bi, cq, ei, co: (bi, ei, 0, 0, 0))],
        out_specs=pl.BlockSpec((1, 1, cap, GATHER_COLS), lambda bi, cq, ei, co: (bi, ei, 0, cq)),
        scratch_shapes=[pltpu.VMEM((cap + MOE_CHUNK, GATHER_COLS), F32)],
    )
    return pl.pallas_call(
        functools.partial(_gather_kernel, n_chunks=n_chunks, cap=cap),
        grid_spec=grid_spec,
        out_shape=jax.ShapeDtypeStruct((b, N_EXPERTS, cap, d), BF16),
        compiler_params=pltpu.CompilerParams(
            dimension_semantics=("arbitrary", "arbitrary", "arbitrary"), vmem_limit_bytes=VMEM_LIMIT),
        name="moe_gather",
    )(coff, xn, slot5)


def _ffn_kernel(xc_ref, wg_ref, wu_ref, wdt_ref, o_ref, *, tiles):
    xc = xc_ref[0, 0]
    g = _dot(xc, wg_ref[0])
    u = _dot(xc, wu_ref[0])
    h = (g / (1.0 + jnp.exp(-g)) * u).astype(BF16)
    yt = _dot_nt(wdt_ref[0], h)
    for r in range(tiles):
        o_ref[0, 0, r] = yt[:, r * MOE_CHUNK:(r + 1) * MOE_CHUNK].astype(BF16)


def _ffn_call(xc, wg, wu, wdt):
    b, e, cap, d = xc.shape
    ff = wg.shape[-1]
    rt = min(cap, 1024)
    tiles = rt // MOE_CHUNK
    return pl.pallas_call(
        functools.partial(_ffn_kernel, tiles=tiles),
        grid=(e, b, cap // rt),
        in_specs=[pl.BlockSpec((1, 1, rt, d), lambda ei, bi, ri: (bi, ei, ri, 0)),
                  pl.BlockSpec((1, d, ff), lambda ei, bi, ri: (ei, 0, 0)),
                  pl.BlockSpec((1, d, ff), lambda ei, bi, ri: (ei, 0, 0)),
                  pl.BlockSpec((1, d, ff), lambda ei, bi, ri: (ei, 0, 0))],
        out_specs=pl.BlockSpec((1, 1, tiles, d, MOE_CHUNK), lambda ei, bi, ri: (bi, ei, ri, 0, 0)),
        out_shape=jax.ShapeDtypeStruct((b, e, cap // MOE_CHUNK, d, MOE_CHUNK), BF16),
        compiler_params=pltpu.CompilerParams(
            dimension_semantics=("arbitrary", "arbitrary", "arbitrary"), vmem_limit_bytes=VMEM_LIMIT),
        name="moe_ffn",
    )(xc, wg, wu, wdt)


def _scatter_kernel(coff_ref, y_ref, slot_ref, gate_ref, o_ref, *, n_chunks, tiles):
    bi = pl.program_id(0)
    kg = pl.program_id(2)
    for j in range(SCATTER_CHUNKS):
        k = kg * SCATTER_CHUNKS + j
        acc = jnp.zeros((MOE_COLS, MOE_CHUNK), F32)
        for e in range(N_EXPERTS):
            r0 = _window_tile(coff_ref[(bi * N_EXPERTS + e) * (n_chunks + 1) + k], tiles)
            r1 = jnp.minimum(r0 + 1, tiles - 1)
            g = _window_onehot(slot_ref[0, e, j], r0 * MOE_CHUNK)
            yw = jnp.concatenate([y_ref[0, e, r0], y_ref[0, e, r1]], axis=1)
            acc = acc + _dot(yw, g) * gate_ref[0, e, j]
        o_ref[0, 0, j] = acc


def _scatter_call(coff, y5, slot5, gate5):
    b, e, tiles, d, _ = y5.shape
    n_chunks = slot5.shape[2]
    ncq = d // MOE_COLS
    small = pl.BlockSpec((1, e, SCATTER_CHUNKS, 1, MOE_CHUNK), lambda bi, cq, kg, co: (bi, 0, kg, 0, 0))
    grid_spec = pltpu.PrefetchScalarGridSpec(
        num_scalar_prefetch=1,
        grid=(b, ncq, n_chunks // SCATTER_CHUNKS),
        in_specs=[pl.BlockSpec((1, e, tiles, MOE_COLS, MOE_CHUNK), lambda bi, cq, kg, co: (bi, 0, 0, cq, 0)),
                  small, small],
        out_specs=pl.BlockSpec((1, 1, SCATTER_CHUNKS, MOE_COLS, MOE_CHUNK),
                               lambda bi, cq, kg, co: (bi, cq, kg, 0, 0)),
    )
    return pl.pallas_call(
        functools.partial(_scatter_kernel, n_chunks=n_chunks, tiles=tiles),
        grid_spec=grid_spec,
        out_shape=jax.ShapeDtypeStruct((b, ncq, n_chunks, MOE_COLS, MOE_CHUNK), F32),
        compiler_params=pltpu.CompilerParams(
            dimension_semantics=("arbitrary", "arbitrary", "arbitrary"), vmem_limit_bytes=VMEM_LIMIT),
        name="moe_scatter",
    )(coff, y5, slot5, gate5)


def _final_kernel(x_ref, moe_ref, o_ref):
    o_ref[0] = x_ref[0] + _moe_tile_to_rows(moe_ref, TOK_TILE // MOE_CHUNK)


def _final_call(x1, moe):
    b, s, d = x1.shape
    ncq = d // MOE_COLS
    tok = pl.BlockSpec((1, TOK_TILE, d), lambda bi, ti: (bi, ti, 0))
    return pl.pallas_call(
        _final_kernel,
        grid=(b, s // TOK_TILE),
        in_specs=[tok, pl.BlockSpec((1, ncq, TOK_TILE // MOE_CHUNK, MOE_COLS, MOE_CHUNK),
                                    lambda bi, ti: (bi, 0, ti, 0, 0))],
        out_specs=tok,
        out_shape=jax.ShapeDtypeStruct((b, s, d), F32),
        compiler_params=pltpu.CompilerParams(
            dimension_semantics=("arbitrary", "arbitrary"), vmem_limit_bytes=VMEM_LIMIT),
        name="final_add",
    )(x1, moe)


def _block_ones(n, group):
    i = np.arange(n)
    return jnp.asarray((i[:, None] // group) == (i[None, :] // group), BF16)


def _tables(s):
    half = MLA_ROPE // 2
    inv = ROPE_THETA ** (-jnp.arange(half, dtype=F32) / half)
    ang = jnp.arange(s, dtype=jnp.int32).astype(F32)[:, None] * inv[None, :]
    cos, sin = jnp.cos(ang), jnp.sin(ang)
    ones = jnp.ones((s, MLA_NOPE), F32)
    tail = HEAD_PAD - MLA_QK
    cos_t = jnp.concatenate([ones, cos, cos, jnp.ones((s, tail), F32)], axis=1)
    sin_t = jnp.concatenate([0 * ones, -sin, sin, jnp.zeros((s, tail), F32)], axis=1)
    i = np.arange(LANES)
    tri = np.stack([i[:, None] <= i[None, :], i[None, :] < i[:, None], i[:, None] < i[None, :]])
    return {'cos': cos_t, 'sin': sin_t, 'e384': _block_ones(2 * LANES, HEAD_DIM),
            'e768': _block_ones(2 * LANES, HEAD_PAD), 'tri': jnp.asarray(tri, BF16)}


def _pad_heads(v, scale=1.0):
    g = jnp.pad(v.astype(F32) * scale, (0, HEAD_PAD - MLA_QK))
    return jnp.tile(g, MLA_HEADS)[None, :]


def _layer_weights(l, p):
    w_in = p['w_in'][l]
    kr_end = 3 * A_WIDTH + MLA_Q_RANK + MLA_KV_RANK + MLA_ROPE
    w_in_r = jnp.concatenate(
        [w_in[:, :kr_end], jnp.zeros((D_MODEL, LANES - MLA_ROPE), F32), w_in[:, kr_end:]], axis=1)
    w_uq = p['w_uq'][l].reshape(MLA_Q_RANK, MLA_HEADS, MLA_QK)
    w_uq = jnp.pad(w_uq, ((0, 0), (0, 0), (0, HEAD_PAD - MLA_QK))).reshape(MLA_Q_RANK, QB_WIDTH)
    w_ukv = p['w_ukv'][l].reshape(MLA_KV_RANK, MLA_HEADS, MLA_NOPE + MLA_V)
    w_uk = jnp.pad(w_ukv[:, :, :MLA_NOPE], ((0, 0), (0, 0), (0, HEAD_PAD - MLA_NOPE)))
    w_uk = w_uk.reshape(MLA_KV_RANK, QB_WIDTH)
    w_uv = w_ukv[:, :, MLA_NOPE:].reshape(MLA_KV_RANK, B_WIDTH)
    place = np.zeros((LANES, QB_WIDTH), np.float32)
    for h in range(MLA_HEADS):
        place[np.arange(MLA_ROPE), h * HEAD_PAD + MLA_NOPE + np.arange(MLA_ROPE)] = 1.0
    w_kk = jnp.concatenate([w_uk, jnp.asarray(place)], axis=0)
    out_norm = p['out_norm'][l]
    w_out = p['w_out'][l]
    w_r = jnp.pad(p['w_router'][l], ((0, 0), (0, LANES - N_EXPERTS)))
    wr_hi = w_r.astype(BF16)
    b_r = jnp.concatenate([p['b_router'][l].astype(F32), jnp.full((LANES - N_EXPERTS,), NEG_BIG, F32)])
    return {
        'nmix': p['norm_mix'][l][None, :], 'w_in': w_in_r.astype(BF16),
        'gqa': jnp.tile(p['q_norm_a'][l] * HEAD_DIM ** -0.5, NA_HEADS)[None, :],
        'gka': jnp.tile(p['k_norm_a'][l], NA_HEADS)[None, :],
        'cqn': p['cq_norm'][l][None, :], 'w_uq': w_uq.astype(BF16),
        'ckvn': p['ckv_norm'][l][None, :], 'w_kk': w_kk.astype(BF16), 'w_uv': w_uv.astype(BF16),
        'gqb': _pad_heads(p['q_norm_b'][l], MLA_QK ** -0.5 * LOG2_E), 'gkb': _pad_heads(p['k_norm_b'][l]),
        'ga': out_norm[None, :A_WIDTH], 'gb_col': out_norm[A_WIDTH:A_WIDTH + B_WIDTH, None],
        'gc': out_norm[None, A_WIDTH + B_WIDTH:],
        'wo_a': w_out[:A_WIDTH].astype(BF16), 'wo_b': w_out[A_WIDTH:A_WIDTH + B_WIDTH].astype(BF16),
        'wo_c': w_out[A_WIDTH + B_WIDTH:].astype(BF16),
        'conv_w': p['conv_w'][l], 'nffn': p['norm_ffn'][l][None, :],
        'wr_hi': wr_hi, 'wr_lo': (w_r - wr_hi.astype(F32)).astype(BF16), 'b_r': b_r[None, :],
        'w_gate': p['w_gate'][l].astype(BF16), 'w_up': p['w_up'][l].astype(BF16),
        'w_down_t': jnp.swapaxes(p['w_down'][l], 1, 2).astype(BF16),
    }


def _layer(x, moe, lw, tabs, rpb_l):
    b, s, d = x.shape
    cap = CAPACITY_FACTOR * s // N_EXPERTS
    n_chunks = s // MOE_CHUNK

    outs = _proj_call(x, moe, lw, tabs)
    if moe is not None:
        qa, ka, va, qb, kb, vbt, u, bc, x = outs
    else:
        qa, ka, va, qb, kb, vbt, u, bc = outs
    out_a = _na_call(qa, ka, va, _bias_call(rpb_l))
    out_bt = _mla_call(qb, kb, vbt)
    x1, xn, aff_t = _out_call(x, out_a, out_bt, u, bc, lw)

    slot, gate, off = _topk_call(aff_t, tabs['tri'], cap)
    rows_per_chunk = MOE_CHUNK // LANES
    coff = off[:, :, 0, :s // LANES:rows_per_chunk].astype(jnp.int32)
    coff = jnp.concatenate([coff, jnp.full((b, N_EXPERTS, 1), cap, jnp.int32)], axis=-1).reshape(-1)
    slot5 = slot.reshape(b, N_EXPERTS, -1)[:, :, :s].reshape(b, N_EXPERTS, n_chunks, 1, MOE_CHUNK)
    gate5 = gate.reshape(b, N_EXPERTS, -1)[:, :, :s].reshape(b, N_EXPERTS, n_chunks, 1, MOE_CHUNK)

    xc = _gather_call(coff, xn, slot5, cap)
    y5 = _ffn_call(xc, lw['w_gate'], lw['w_up'], lw['w_down_t'])
    moe_out = _scatter_call(coff, y5, slot5, gate5)
    return x1, moe_out


def kernel(x, norm_mix, w_in, q_norm_a, k_norm_a, rpb, cq_norm, w_uq, ckv_norm, w_ukv, q_norm_b,
           k_norm_b, conv_w, out_norm, w_out, norm_ffn, w_router, b_router, w_gate, w_up, w_down):
    p = dict(norm_mix=norm_mix, w_in=w_in, q_norm_a=q_norm_a, k_norm_a=k_norm_a, cq_norm=cq_norm,
             w_uq=w_uq, ckv_norm=ckv_norm, w_ukv=w_ukv, q_norm_b=q_norm_b, k_norm_b=k_norm_b,
             conv_w=conv_w, out_norm=out_norm, w_out=w_out, norm_ffn=norm_ffn, w_router=w_router,
             b_router=b_router, w_gate=w_gate, w_up=w_up, w_down=w_down)
    b, s, d = x.shape
    assert d == D_MODEL and (CAPACITY_FACTOR * s // N_EXPERTS) % MOE_CHUNK == 0
    assert s % TOK_TILE == 0 and (s // GRID_W) % NA_ROWS == 0 and s <= TOPK_ROWS * LANES
    tabs = _tables(s)
    moe = None
    for l in range(norm_mix.shape[0]):
        x, moe = _layer(x, moe, _layer_weights(l, p), tabs, rpb[l])
    return _final_call(x, moe)
```

```python
import functools

import numpy as np
import jax
import jax.numpy as jnp
from jax import lax
from jax.experimental import pallas as pl
from jax.experimental.pallas import tpu as pltpu

D_MODEL = 1024
GRID_W = 64
HEAD_DIM = 64
EPS = 1e-6

NA_HEADS = 6
NA_WIN_H = 8
NA_WIN_W = 16
A_WIDTH = NA_HEADS * HEAD_DIM

MLA_HEADS = 6
MLA_Q_RANK = 256
MLA_KV_RANK = 128
MLA_NOPE = 64
MLA_ROPE = 32
MLA_QK = MLA_NOPE + MLA_ROPE
MLA_V = 64
B_WIDTH = MLA_HEADS * MLA_V
ROPE_THETA = 10000.0

CONV_CH = 256
C_WIDTH = CONV_CH

N_EXPERTS = 16
EXPERT_FF = 1024
CAPACITY_FACTOR = 2

LANES = 128
HEAD_PAD = 128
QB_WIDTH = MLA_HEADS * HEAD_PAD
IN_COLS_PAD = 2432
TOK_TILE = 512
NA_ROWS = 8
MOE_CHUNK = 256
MOE_COLS = 256
GATHER_COLS = 512
SCATTER_CHUNKS = 2
GATHER_WIN = MOE_CHUNK + 16
TOPK_ROWS = 128
MLA_VROWS = MLA_V + 16
LOG2_E = 1.4426950408889634
MLA_UNROLL = 16
NEG_BIG = -1e30
VMEM_LIMIT = 56 * 1024 * 1024

F32 = jnp.float32
BF16 = jnp.bfloat16


def _dot(a, b):
    return jnp.dot(a, b, preferred_element_type=F32)


def _dot_nt(a, b):
    return lax.dot_general(a, b, (((1,), (1,)), ((), ())), preferred_element_type=F32)


def _rms(x, g):
    return x * lax.rsqrt(jnp.mean(x * x, axis=-1, keepdims=True) + EPS) * g


def _group_sumsq(y, e_ref):
    sq = (y * y).astype(BF16)
    width = y.shape[1]
    parts = []
    for c in range(0, width, 2 * LANES):
        n = min(2 * LANES, width - c)
        parts.append(_dot(sq[:, c:c + n], e_ref[0:n, 0:n]))
    return jnp.concatenate(parts, axis=1)


def _moe_tile_to_rows(moe_ref, n_chunks):
    rows = []
    for ch in range(n_chunks):
        cols = [moe_ref[0, cq, ch].T for cq in range(D_MODEL // MOE_COLS)]
        rows.append(jnp.concatenate(cols, axis=1))
    return jnp.concatenate(rows, axis=0)


def _proj_kernel(*refs, has_moe):
    if has_moe:
        x_ref, moe_ref = refs[0], refs[1]
        refs = refs[2:]
    else:
        x_ref = refs[0]
        refs = refs[1:]
    (nmix_ref, win_ref, gqa_ref, gka_ref, e384_ref, cqn_ref, wuq_ref, ckvn_ref, wkk_ref,
     wuv_ref, gqb_ref, gkb_ref, e768_ref, cos_ref, sin_ref) = refs[:15]
    outs = refs[15:]
    if has_moe:
        qa_o, ka_o, va_o, qb_o, kb_o, vbt_o, u_o, bc_o, xs_o = outs
    else:
        qa_o, ka_o, va_o, qb_o, kb_o, vbt_o, u_o, bc_o = outs

    x = x_ref[0]
    if has_moe:
        x = x + _moe_tile_to_rows(moe_ref, TOK_TILE // MOE_CHUNK)
        xs_o[0] = x
    xb = _rms(x, nmix_ref[...]).astype(BF16)

    pa = _dot(xb, win_ref[:, 0:3 * A_WIDTH])
    qa = pa[:, 0:A_WIDTH]
    ka = pa[:, A_WIDTH:2 * A_WIDTH]
    qa_o[0] = (qa * lax.rsqrt(_group_sumsq(qa, e384_ref) * (1.0 / HEAD_DIM) + EPS) * gqa_ref[...]).astype(BF16)
    ka_o[0] = (ka * lax.rsqrt(_group_sumsq(ka, e384_ref) * (1.0 / HEAD_DIM) + EPS) * gka_ref[...]).astype(BF16)
    va_o[0] = pa[:, 2 * A_WIDTH:3 * A_WIDTH].astype(BF16)

    c0 = 3 * A_WIDTH
    pb = _dot(xb, win_ref[:, c0:c0 + 512])
    cq = _rms(pb[:, 0:MLA_Q_RANK], cqn_ref[...]).astype(BF16)
    ckv = _rms(pb[:, MLA_Q_RANK:MLA_Q_RANK + MLA_KV_RANK], ckvn_ref[...]).astype(BF16)
    kr = pb[:, MLA_Q_RANK + MLA_KV_RANK:512].astype(BF16)
    qb_raw = _dot(cq, wuq_ref[...])
    kb_raw = _dot(jnp.concatenate([ckv, kr], axis=1), wkk_ref[...])
    vb = _dot(ckv, wuv_ref[...])

    cos = jnp.concatenate([cos_ref[...]] * MLA_HEADS, axis=1)
    sin = jnp.concatenate([sin_ref[...]] * MLA_HEADS, axis=1)
    lane = lax.broadcasted_iota(jnp.int32, (TOK_TILE, QB_WIDTH), 1) & (HEAD_PAD - 1)
    first_half = (lane >= MLA_NOPE) & (lane < MLA_NOPE + MLA_ROPE // 2)

    def norm_rope(y, g_ref):
        y = y * lax.rsqrt(_group_sumsq(y, e768_ref) * (1.0 / MLA_QK) + EPS) * g_ref[...]
        rot = jnp.where(first_half,
                        pltpu.roll(y, QB_WIDTH - MLA_ROPE // 2, 1),
                        pltpu.roll(y, MLA_ROPE // 2, 1))
        return y * cos + rot * sin

    qb_o[0] = norm_rope(qb_raw, gqb_ref).T.astype(BF16)
    kb_o[0] = norm_rope(kb_raw, gkb_ref).astype(BF16)
    vbt = vb.T
    for h in range(MLA_HEADS):
        vbt_o[0, h, 0, 0:MLA_V, :] = vbt[h * MLA_V:(h + 1) * MLA_V, :].astype(BF16)
        vbt_o[0, h, 0, MLA_V:MLA_VROWS, :] = jnp.ones((MLA_VROWS - MLA_V, TOK_TILE), BF16)

    c1 = c0 + 512
    pc = _dot(xb, win_ref[:, c1:c1 + 3 * CONV_CH])
    u_o[0] = pc[:, 2 * CONV_CH:3 * CONV_CH] * pc[:, 0:CONV_CH]
    bc_o[0] = pc[:, CONV_CH:2 * CONV_CH]


def _proj_call(x, moe, lw, tabs):
    b, s, d = x.shape
    nt = s // TOK_TILE
    has_moe = moe is not None

    def const(shape):
        return pl.BlockSpec(shape, lambda bi, ti: (0,) * len(shape))

    in_specs = [pl.BlockSpec((1, TOK_TILE, d), lambda bi, ti: (bi, ti, 0))]
    args = [x]
    if has_moe:
        ncq = D_MODEL // MOE_COLS
        in_specs.append(pl.BlockSpec((1, ncq, TOK_TILE // MOE_CHUNK, MOE_COLS, MOE_CHUNK),
                                     lambda bi, ti: (bi, 0, ti, 0, 0)))
        args.append(moe)
    consts = [lw['nmix'], lw['w_in'], lw['gqa'], lw['gka'], tabs['e384'], lw['cqn'], lw['w_uq'],
              lw['ckvn'], lw['w_kk'], lw['w_uv'], lw['gqb'], lw['gkb'], tabs['e768']]
    for c in consts:
        in_specs.append(const(c.shape))
        args.append(c)
    in_specs += [pl.BlockSpec((TOK_TILE, HEAD_PAD), lambda bi, ti: (ti, 0))] * 2
    args += [tabs['cos'], tabs['sin']]

    def tok(width, dtype):
        return (jax.ShapeDtypeStruct((b, s, width), dtype),
                pl.BlockSpec((1, TOK_TILE, width), lambda bi, ti: (bi, ti, 0)))

    outs = [tok(A_WIDTH, BF16), tok(A_WIDTH, BF16), tok(A_WIDTH, BF16),
            (jax.ShapeDtypeStruct((b, QB_WIDTH, s), BF16),
             pl.BlockSpec((1, QB_WIDTH, TOK_TILE), lambda bi, ti: (bi, 0, ti))),
            tok(QB_WIDTH, BF16),
            (jax.ShapeDtypeStruct((b, MLA_HEADS, nt, MLA_VROWS, TOK_TILE), BF16),
             pl.BlockSpec((1, MLA_HEADS, 1, MLA_VROWS, TOK_TILE), lambda bi, ti: (bi, 0, ti, 0, 0))),
            tok(CONV_CH, F32), tok(CONV_CH, F32)]
    if has_moe:
        outs.append(tok(d, F32))
    return pl.pallas_call(
        functools.partial(_proj_kernel, has_moe=has_moe),
        grid=(b, nt),
        in_specs=in_specs,
        out_specs=[o[1] for o in outs],
        out_shape=[o[0] for o in outs],
        compiler_params=pltpu.CompilerParams(
            dimension_semantics=("arbitrary", "arbitrary"), vmem_limit_bytes=VMEM_LIMIT),
        name="proj_moe" if has_moe else "proj",
    )(*args)


def _bias_kernel(rpb_ref, o_ref):
    h = pl.program_id(0)
    delta = pl.program_id(1)
    nw = 2 * NA_WIN_W - 1
    shape = (GRID_W, NA_WIN_H * GRID_W)
    c = lax.broadcasted_iota(jnp.int32, shape, 0)
    col = lax.broadcasted_iota(jnp.int32, shape, 1)
    wrow = lax.shift_right_logical(col, 6)
    kc = col & (GRID_W - 1)
    c_lo = jnp.clip(c - NA_WIN_W // 2, 0, GRID_W - NA_WIN_W)
    inside = (kc >= c_lo) & (kc < c_lo + NA_WIN_W)
    d = kc - c + (NA_WIN_W - 1)
    acc = jnp.zeros(shape, F32)
    base = h * ((2 * NA_WIN_H - 1) * nw)
    for i in range(NA_WIN_H):
        ri = i - delta + (NA_WIN_H - 1)
        for dd in range(nw):
            val = rpb_ref[base + ri * nw + dd]
            acc = jnp.where((wrow == i) & (d == dd), val, acc)
    o_ref[0, 0] = jnp.where(inside, acc, NEG_BIG)


def _bias_call(rpb_l):
    flat = rpb_l.reshape(-1)
    return pl.pallas_call(
        _bias_kernel,
        grid=(NA_HEADS, NA_WIN_H),
        in_specs=[pl.BlockSpec(memory_space=pltpu.SMEM)],
        out_specs=pl.BlockSpec((1, 1, GRID_W, NA_WIN_H * GRID_W), lambda h, dl: (h, dl, 0, 0)),
        out_shape=jax.ShapeDtypeStruct((NA_HEADS, NA_WIN_H, GRID_W, NA_WIN_H * GRID_W), F32),
        name="na_bias",
    )(flat)


def _na_kernel(q_ref, kp_ref, kc_ref, kn_ref, vp_ref, vc_ref, vn_ref, bias_ref, o_ref,
               k_scr, v_scr, *, rows):
    j = pl.program_id(1)
    blk = NA_ROWS * GRID_W
    k_scr[0:blk] = kp_ref[0]
    k_scr[blk:2 * blk] = kc_ref[0]
    k_scr[2 * blk:3 * blk] = kn_ref[0]
    v_scr[0:blk] = vp_ref[0]
    v_scr[blk:2 * blk] = vc_ref[0]
    v_scr[2 * blk:3 * blk] = vn_ref[0]
    lane = lax.broadcasted_iota(jnp.int32, (GRID_W, LANES), 1)
    low = lane < HEAD_DIM
    win = NA_WIN_H * GRID_W

    def window_start(i):
        r = NA_ROWS * j + i
        r0 = jnp.clip(r - NA_WIN_H // 2, 0, rows - NA_WIN_H)
        return r - r0, pl.multiple_of((r0 - NA_ROWS * j + NA_ROWS) * GRID_W, GRID_W)

    tiles = []
    for i in range(NA_ROWS):
        delta, start = window_start(i)
        for p in range(NA_HEADS // 2):
            lsl = slice(p * LANES, (p + 1) * LANES)
            qp = q_ref[0, i * GRID_W:(i + 1) * GRID_W, lsl]
            kw = k_scr[pl.ds(start, win), lsl]
            for a in range(2):
                qm = jnp.where(low if a == 0 else jnp.logical_not(low), qp, jnp.zeros_like(qp))
                tiles.append(_dot_nt(qm, kw) + bias_ref[2 * p + a, delta])
    sc = jnp.concatenate(tiles, axis=0)
    e = jnp.exp(sc - jnp.max(sc, axis=-1, keepdims=True))
    inv = 1.0 / jnp.sum(e, axis=-1, keepdims=True)
    eb = e.astype(BF16)
    t = 0
    for i in range(NA_ROWS):
        _, start = window_start(i)
        for p in range(NA_HEADS // 2):
            lsl = slice(p * LANES, (p + 1) * LANES)
            vw = v_scr[pl.ds(start, win), lsl]
            res = []
            for a in range(2):
                rsl = slice(t * GRID_W, (t + 1) * GRID_W)
                res.append(_dot(eb[rsl], vw) * inv[rsl])
                t += 1
            o_ref[0, i * GRID_W:(i + 1) * GRID_W, lsl] = jnp.where(low, res[0], res[1])


def _na_call(qa, ka, va, bias_tab):
    b, s, w = qa.shape
    rows = s // GRID_W
    nb = rows // NA_ROWS
    blk = NA_ROWS * GRID_W
    cur = pl.BlockSpec((1, blk, w), lambda bi, j: (bi, j, 0))
    prev = pl.BlockSpec((1, blk, w), lambda bi, j: (bi, jnp.maximum(j - 1, 0), 0))
    nxt = pl.BlockSpec((1, blk, w), lambda bi, j: (bi, jnp.minimum(j + 1, nb - 1), 0))
    return pl.pallas_call(
        functools.partial(_na_kernel, rows=rows),
        grid=(b, nb),
        in_specs=[cur, prev, cur, nxt, prev, cur, nxt,
                  pl.BlockSpec(bias_tab.shape, lambda bi, j: (0, 0, 0, 0))],
        out_specs=pl.BlockSpec((1, blk, w), lambda bi, j: (bi, j, 0)),
        out_shape=jax.ShapeDtypeStruct((b, s, w), F32),
        scratch_shapes=[pltpu.VMEM((3 * blk, w), BF16), pltpu.VMEM((3 * blk, w), BF16)],
        compiler_params=pltpu.CompilerParams(
            dimension_semantics=("arbitrary", "arbitrary"), vmem_limit_bytes=VMEM_LIMIT),
        name="na_attn",
    )(qa, ka, ka, ka, va, va, va, bias_tab)


def _mla_kernel(q_ref, k_ref, vt_ref, o_ref, s_scr, acc_scr, m_scr, *, n_chunks):
    qt = q_ref[0]

    def scores(c):
        kb = k_ref[0, pl.ds(pl.multiple_of(c * TOK_TILE, TOK_TILE), TOK_TILE), :]
        return _dot(kb, qt)

    def update(st, c):
        m = m_scr[...]
        m_new = jnp.maximum(m, jnp.max(st, axis=0, keepdims=True))
        alpha = jnp.exp2(m - m_new)
        p = jnp.exp2(st - m_new).astype(BF16)
        acc_scr[...] = alpha * acc_scr[...] + _dot(vt_ref[0, 0, c], p)
        m_scr[...] = m_new

    m_scr[...] = jnp.full(m_scr.shape, NEG_BIG, F32)
    acc_scr[...] = jnp.zeros(acc_scr.shape, F32)
    s_scr[0] = scores(0)

    def body(i, carry):
        for j in range(MLA_UNROLL):
            c = MLA_UNROLL * i + j
            s_scr[(j + 1) % 2] = scores(jnp.minimum(c + 1, n_chunks - 1))
            update(s_scr[j % 2], c)
        return carry

    lax.fori_loop(0, n_chunks // MLA_UNROLL, body, 0)
    acc = acc_scr[...]
    o_ref[0] = acc[0:MLA_V, :] / acc[MLA_V:MLA_V + 1, :]


def _mla_call(qb, kb, vbt):
    b, s, _ = kb.shape
    nt = s // TOK_TILE
    assert nt % MLA_UNROLL == 0 and MLA_UNROLL % 2 == 0
    return pl.pallas_call(
        functools.partial(_mla_kernel, n_chunks=nt),
        grid=(b, MLA_HEADS, nt),
        in_specs=[pl.BlockSpec((1, HEAD_PAD, TOK_TILE), lambda bi, h, qi: (bi, h, qi)),
                  pl.BlockSpec((1, s, HEAD_PAD), lambda bi, h, qi: (bi, 0, h)),
                  pl.BlockSpec((1, 1, nt, MLA_VROWS, TOK_TILE), lambda bi, h, qi: (bi, h, 0, 0, 0))],
        out_specs=pl.BlockSpec((1, MLA_V, TOK_TILE), lambda bi, h, qi: (bi, h, qi)),
        out_shape=jax.ShapeDtypeStruct((b, B_WIDTH, s), F32),
        scratch_shapes=[pltpu.VMEM((2, TOK_TILE, TOK_TILE), F32), pltpu.VMEM((MLA_VROWS, TOK_TILE), F32),
                        pltpu.VMEM((1, TOK_TILE), F32)],
        compiler_params=pltpu.CompilerParams(
            dimension_semantics=("arbitrary", "arbitrary", "arbitrary"), vmem_limit_bytes=VMEM_LIMIT),
        name="mla_attn",
    )(qb, kb, vbt)


def _out_kernel(x_ref, oa_ref, obt_ref, u_ref, up_ref, un_ref, bc_ref, ga_ref, gb_ref, gc_ref,
                woa_ref, wob_ref, woc_ref, cw_ref, nffn_ref, wrh_ref, wrl_ref, br_ref,
                x1_o, xn_o, aff_o):
    ti = pl.program_id(1)
    nt = pl.num_programs(1)
    t = TOK_TILE

    na = _rms(oa_ref[0], ga_ref[...]).astype(BF16)
    acc = _dot(na, woa_ref[...])

    obt = obt_ref[0]
    nbt = obt * lax.rsqrt(jnp.mean(obt * obt, axis=0, keepdims=True) + EPS) * gb_ref[...]
    acc = acc + _dot(nbt.T.astype(BF16), wob_ref[...])

    u = u_ref[0]
    row = lax.broadcasted_iota(jnp.int32, (t, CONV_CH), 0)
    prev_row = jnp.where(ti > 0, up_ref[0, 7:8, :], 0.0)
    next_row = jnp.where(ti < nt - 1, un_ref[0, 0:1, :], 0.0)
    u_m1 = jnp.where(row == 0, prev_row, pltpu.roll(u, 1, 0))
    u_p1 = jnp.where(row == t - 1, next_row, pltpu.roll(u, t - 1, 0))
    y = cw_ref[0:1, :] * u_m1 + cw_ref[1:2, :] * u + cw_ref[2:3, :] * u_p1
    oc = _rms(bc_ref[0] * y, gc_ref[...]).astype(BF16)
    acc = acc + _dot(oc, woc_ref[...])

    x1 = x_ref[0] + acc
    x1_o[0] = x1
    xn = _rms(x1, nffn_ref[...])
    hi = xn.astype(BF16)
    lo = (xn - hi.astype(F32)).astype(BF16)
    xn_o[0] = hi
    logits = _dot(hi, wrh_ref[...]) + _dot(lo, wrh_ref[...]) + _dot(hi, wrl_ref[...]) + br_ref[...]
    m = jnp.max(logits, axis=-1, keepdims=True)
    e = jnp.exp(logits - m)
    aff = e / jnp.sum(e, axis=-1, keepdims=True)
    aff_o[0] = aff.T[0:N_EXPERTS, :]


def _out_call(x, out_a, out_bt, u, bc, lw):
    b, s, d = x.shape
    nt = s // TOK_TILE
    h8 = TOK_TILE // 8

    def const(a):
        return pl.BlockSpec(a.shape, lambda bi, ti: (0,) * a.ndim)

    def tok(width):
        return pl.BlockSpec((1, TOK_TILE, width), lambda bi, ti: (bi, ti, 0))

    consts = [lw['ga'], lw['gb_col'], lw['gc'], lw['wo_a'], lw['wo_b'], lw['wo_c'], lw['conv_w'],
              lw['nffn'], lw['wr_hi'], lw['wr_lo'], lw['b_r']]
    in_specs = [tok(d), tok(A_WIDTH),
                pl.BlockSpec((1, B_WIDTH, TOK_TILE), lambda bi, ti: (bi, 0, ti)),
                tok(CONV_CH),
                pl.BlockSpec((1, 8, CONV_CH), lambda bi, ti: (bi, jnp.maximum(ti * h8 - 1, 0), 0)),
                pl.BlockSpec((1, 8, CONV_CH), lambda bi, ti: (bi, jnp.minimum((ti + 1) * h8, s // 8 - 1), 0)),
                tok(CONV_CH)] + [const(c) for c in consts]
    return pl.pallas_call(
        _out_kernel,
        grid=(b, nt),
        in_specs=in_specs,
        out_specs=[tok(d), tok(d), pl.BlockSpec((1, N_EXPERTS, TOK_TILE), lambda bi, ti: (bi, 0, ti))],
        out_shape=[jax.ShapeDtypeStruct((b, s, d), F32), jax.ShapeDtypeStruct((b, s, d), BF16),
                   jax.ShapeDtypeStruct((b, N_EXPERTS, s), F32)],
        compiler_params=pltpu.CompilerParams(
            dimension_semantics=("arbitrary", "arbitrary"), vmem_limit_bytes=VMEM_LIMIT),
        name="out_proj",
    )(x, out_a, out_bt, u, u, u, bc, *consts)


def _topk_kernel(a_ref, tri_ref, o_slot, o_gate, o_off, *, cap):
    a = a_ref[0, 0]
    bits = pltpu.bitcast(a, jnp.int32)

    def total(mask):
        v = jnp.where(mask, 1.0, 0.0)
        return jnp.sum(jnp.sum(v, axis=0, keepdims=True), axis=1, keepdims=True)

    thr = jnp.zeros((1, 1), jnp.int32)
    for bit in range(30, -1, -1):
        cand = thr | (1 << bit)
        thr = jnp.where(total(bits >= cand) >= cap, cand, thr)
    gt = bits > thr
    eq = bits == thr
    need = cap - total(gt)

    upper = tri_ref[0]
    lower_strict = tri_ref[1]
    upper_strict = tri_ref[2]

    def excl_rank(mask):
        f = jnp.where(mask, 1.0, 0.0)
        fb = f.astype(BF16)
        in_row = _dot(fb, upper)
        before_rows = jnp.sum(_dot(lower_strict, fb), axis=1, keepdims=True)
        return before_rows + in_row - f, fb

    eq_rank, _ = excl_rank(eq)
    sel = gt | (eq & (eq_rank < need))
    pos, sel_b = excl_rank(sel)
    o_slot[0, 0] = jnp.where(sel, pos, -1.0).astype(jnp.int32)
    o_gate[0, 0] = jnp.where(sel, a, 0.0)
    row_tot = _dot_nt(jnp.ones((8, LANES), BF16), sel_b)
    o_off[0, 0] = _dot(row_tot.astype(BF16), upper_strict)


def _topk_call(aff_t, tri, cap):
    b, e, s = aff_t.shape
    pad = TOPK_ROWS * LANES - s
    a = aff_t if pad == 0 else jnp.pad(aff_t, ((0, 0), (0, 0), (0, pad)), constant_values=-1.0)
    a = a.reshape(b, e, TOPK_ROWS, LANES)
    blk = pl.BlockSpec((1, 1, TOPK_ROWS, LANES), lambda bi, ei: (bi, ei, 0, 0))
    return pl.pallas_call(
        functools.partial(_topk_kernel, cap=cap),
        grid=(b, e),
        in_specs=[blk, pl.BlockSpec(tri.shape, lambda bi, ei: (0, 0, 0))],
        out_specs=[blk, blk, pl.BlockSpec((1, 1, 8, LANES), lambda bi, ei: (bi, ei, 0, 0))],
        out_shape=[jax.ShapeDtypeStruct((b, e, TOPK_ROWS, LANES), jnp.int32),
                   jax.ShapeDtypeStruct((b, e, TOPK_ROWS, LANES), F32),
                   jax.ShapeDtypeStruct((b, e, 8, LANES), F32)],
        name="topk_select",
    )(a, tri)


def _window_onehot(slot_row, start):
    i = lax.broadcasted_iota(jnp.int32, (2 * MOE_CHUNK, MOE_CHUNK), 0)
    return jnp.where(i == slot_row - start, 1.0, 0.0).astype(BF16)


def _window_tile(off, tiles):
    return jnp.minimum(lax.shift_right_logical(off, 8), tiles - 1)


def _gather_kernel(coff_ref, x_ref, slot_ref, o_ref, acc_ref, *, n_chunks, cap):
    bi = pl.program_id(0)
    ei = pl.program_id(2)
    base = (bi * N_EXPERTS + ei) * (n_chunks + 1)
    acc_ref[...] = jnp.zeros_like(acc_ref)

    row = lax.broadcasted_iota(jnp.int32, (GATHER_WIN, MOE_CHUNK), 0)

    def body(k, carry):
        off = coff_ref[base + k]
        start = pl.multiple_of(jnp.minimum(lax.shift_left(lax.shift_right_logical(off, 3), 3), cap), 8)
        g = jnp.where(row == slot_ref[0, 0, k] - start, 1.0, 0.0).astype(BF16)
        xk = x_ref[0, pl.ds(pl.multiple_of(k * MOE_CHUNK, MOE_CHUNK), MOE_CHUNK), :]
        acc_ref[pl.ds(start, GATHER_WIN), :] += _dot(g, xk)
        return carry

    lax.fori_loop(0, n_chunks, body, 0, unroll=2)
    o_ref[0, 0] = acc_ref[0:cap, :].astype(BF16)


def _gather_call(coff, xn, slot5, cap):
    b, s, d = xn.shape
    n_chunks = s // MOE_CHUNK
    ncq = d // GATHER_COLS
    grid_spec = pltpu.PrefetchScalarGridSpec(
        num_scalar_prefetch=1,
        grid=(b, ncq, N_EXPERTS),
        in_specs=[pl.BlockSpec((1, s, GATHER_COLS), lambda bi, cq, ei, co: (bi, 0, cq)),
                  pl.BlockSpec((1, 1, n_chunks, 1, MOE_CHUNK), lambda bi, cq, ei, co: (bi, ei, 0, 0, 0))],
        out_specs=pl.BlockSpec((1, 1, cap, GATHER_COLS), lambda bi, cq, ei, co: (bi, ei, 0, cq)),
        scratch_shapes=[pltpu.VMEM((cap + GATHER_WIN, GATHER_COLS), F32)],
    )
    return pl.pallas_call(
        functools.partial(_gather_kernel, n_chunks=n_chunks, cap=cap),
        grid_spec=grid_spec,
        out_shape=jax.ShapeDtypeStruct((b, N_EXPERTS, cap, d), BF16),
        compiler_params=pltpu.CompilerParams(
            dimension_semantics=("arbitrary", "arbitrary", "arbitrary"), vmem_limit_bytes=VMEM_LIMIT),
        name="moe_gather",
    )(coff, xn, slot5)


def _ffn_kernel(xc_ref, wg_ref, wu_ref, wdt_ref, o_ref, *, tiles):
    xc = xc_ref[0, 0]
    g = _dot(xc, wg_ref[0])
    u = _dot(xc, wu_ref[0])
    h = (g / (1.0 + jnp.exp(-g)) * u).astype(BF16)
    yt = _dot_nt(wdt_ref[0], h)
    for r in range(tiles):
        o_ref[0, 0, r] = yt[:, r * MOE_CHUNK:(r + 1) * MOE_CHUNK].astype(BF16)


def _ffn_call(xc, wg, wu, wdt):
    b, e, cap, d = xc.shape
    ff = wg.shape[-1]
    rt = min(cap, 1024)
    tiles = rt // MOE_CHUNK
    return pl.pallas_call(
        functools.partial(_ffn_kernel, tiles=tiles),
        grid=(e, b, cap // rt),
        in_specs=[pl.BlockSpec((1, 1, rt, d), lambda ei, bi, ri: (bi, ei, ri, 0)),
                  pl.BlockSpec((1, d, ff), lambda ei, bi, ri: (ei, 0, 0)),
                  pl.BlockSpec((1, d, ff), lambda ei, bi, ri: (ei, 0, 0)),
                  pl.BlockSpec((1, d, ff), lambda ei, bi, ri: (ei, 0, 0))],
        out_specs=pl.BlockSpec((1, 1, tiles, d, MOE_CHUNK), lambda ei, bi, ri: (bi, ei, ri, 0, 0)),
        out_shape=jax.ShapeDtypeStruct((b, e, cap // MOE_CHUNK, d, MOE_CHUNK), BF16),
        compiler_params=pltpu.CompilerParams(
            dimension_semantics=("arbitrary", "arbitrary", "arbitrary"), vmem_limit_bytes=VMEM_LIMIT),
        name="moe_ffn",
    )(xc, wg, wu, wdt)


def _scatter_kernel(coff_ref, y_ref, slot_ref, gate_ref, o_ref, *, n_chunks, tiles):
    bi = pl.program_id(0)
    kg = pl.program_id(2)
    for j in range(SCATTER_CHUNKS):
        k = kg * SCATTER_CHUNKS + j
        acc = jnp.zeros((MOE_COLS, MOE_CHUNK), F32)
        for e in range(N_EXPERTS):
            r0 = _window_tile(coff_ref[(bi * N_EXPERTS + e) * (n_chunks + 1) + k], tiles)
            r1 = jnp.minimum(r0 + 1, tiles - 1)
            g = _window_onehot(slot_ref[0, e, j], r0 * MOE_CHUNK)
            yw = jnp.concatenate([y_ref[0, e, r0], y_ref[0, e, r1]], axis=1)
            acc = acc + _dot(yw, g) * gate_ref[0, e, j]
        o_ref[0, 0, j] = acc


def _scatter_call(coff, y5, slot5, gate5):
    b, e, tiles, d, _ = y5.shape
    n_chunks = slot5.shape[2]
    ncq = d // MOE_COLS
    small = pl.BlockSpec((1, e, SCATTER_CHUNKS, 1, MOE_CHUNK), lambda bi, cq, kg, co: (bi, 0, kg, 0, 0))
    grid_spec = pltpu.PrefetchScalarGridSpec(
        num_scalar_prefetch=1,
        grid=(b, ncq, n_chunks // SCATTER_CHUNKS),
        in_specs=[pl.BlockSpec((1, e, tiles, MOE_COLS, MOE_CHUNK), lambda bi, cq, kg, co: (bi, 0, 0, cq, 0)),
                  small, small],
        out_specs=pl.BlockSpec((1, 1, SCATTER_CHUNKS, MOE_COLS, MOE_CHUNK),
                               lambda bi, cq, kg, co: (bi, cq, kg, 0, 0)),
    )
    return pl.pallas_call(
        functools.partial(_scatter_kernel, n_chunks=n_chunks, tiles=tiles),
        grid_spec=grid_spec,
        out_shape=jax.ShapeDtypeStruct((b, ncq, n_chunks, MOE_COLS, MOE_CHUNK), F32),
        compiler_params=pltpu.CompilerParams(
            dimension_semantics=("arbitrary", "arbitrary", "arbitrary"), vmem_limit_bytes=VMEM_LIMIT),
        name="moe_scatter",
    )(coff, y5, slot5, gate5)


def _final_kernel(x_ref, moe_ref, o_ref):
    o_ref[0] = x_ref[0] + _moe_tile_to_rows(moe_ref, TOK_TILE // MOE_CHUNK)


def _final_call(x1, moe):
    b, s, d = x1.shape
    ncq = d // MOE_COLS
    tok = pl.BlockSpec((1, TOK_TILE, d), lambda bi, ti: (bi, ti, 0))
    return pl.pallas_call(
        _final_kernel,
        grid=(b, s // TOK_TILE),
        in_specs=[tok, pl.BlockSpec((1, ncq, TOK_TILE // MOE_CHUNK, MOE_COLS, MOE_CHUNK),
                                    lambda bi, ti: (bi, 0, ti, 0, 0))],
        out_specs=tok,
        out_shape=jax.ShapeDtypeStruct((b, s, d), F32),
        compiler_params=pltpu.CompilerParams(
            dimension_semantics=("arbitrary", "arbitrary"), vmem_limit_bytes=VMEM_LIMIT),
        name="final_add",
    )(x1, moe)


def _block_ones(n, group):
    i = np.arange(n)
    return jnp.asarray((i[:, None] // group) == (i[None, :] // group), BF16)


def _tables(s):
    half = MLA_ROPE // 2
    inv = ROPE_THETA ** (-jnp.arange(half, dtype=F32) / half)
    ang = jnp.arange(s, dtype=jnp.int32).astype(F32)[:, None] * inv[None, :]
    cos, sin = jnp.cos(ang), jnp.sin(ang)
    ones = jnp.ones((s, MLA_NOPE), F32)
    tail = HEAD_PAD - MLA_QK
    cos_t = jnp.concatenate([ones, cos, cos, jnp.ones((s, tail), F32)], axis=1)
    sin_t = jnp.concatenate([0 * ones, -sin, sin, jnp.zeros((s, tail), F32)], axis=1)
    i = np.arange(LANES)
    tri = np.stack([i[:, None] <= i[None, :], i[None, :] < i[:, None], i[:, None] < i[None, :]])
    return {'cos': cos_t, 'sin': sin_t, 'e384': _block_ones(2 * LANES, HEAD_DIM),
            'e768': _block_ones(2 * LANES, HEAD_PAD), 'tri': jnp.asarray(tri, BF16)}


def _pad_heads(v, scale=1.0):
    g = jnp.pad(v.astype(F32) * scale, (0, HEAD_PAD - MLA_QK))
    return jnp.tile(g, MLA_HEADS)[None, :]


def _layer_weights(l, p):
    w_in = p['w_in'][l]
    kr_end = 3 * A_WIDTH + MLA_Q_RANK + MLA_KV_RANK + MLA_ROPE
    w_in_r = jnp.concatenate(
        [w_in[:, :kr_end], jnp.zeros((D_MODEL, LANES - MLA_ROPE), F32), w_in[:, kr_end:]], axis=1)
    w_uq = p['w_uq'][l].reshape(MLA_Q_RANK, MLA_HEADS, MLA_QK)
    w_uq = jnp.pad(w_uq, ((0, 0), (0, 0), (0, HEAD_PAD - MLA_QK))).reshape(MLA_Q_RANK, QB_WIDTH)
    w_ukv = p['w_ukv'][l].reshape(MLA_KV_RANK, MLA_HEADS, MLA_NOPE + MLA_V)
    w_uk = jnp.pad(w_ukv[:, :, :MLA_NOPE], ((0, 0), (0, 0), (0, HEAD_PAD - MLA_NOPE)))
    w_uk = w_uk.reshape(MLA_KV_RANK, QB_WIDTH)
    w_uv = w_ukv[:, :, MLA_NOPE:].reshape(MLA_KV_RANK, B_WIDTH)
    place = np.zeros((LANES, QB_WIDTH), np.float32)
    for h in range(MLA_HEADS):
        place[np.arange(MLA_ROPE), h * HEAD_PAD + MLA_NOPE + np.arange(MLA_ROPE)] = 1.0
    w_kk = jnp.concatenate([w_uk, jnp.asarray(place)], axis=0)
    out_norm = p['out_norm'][l]
    w_out = p['w_out'][l]
    w_r = jnp.pad(p['w_router'][l], ((0, 0), (0, LANES - N_EXPERTS)))
    wr_hi = w_r.astype(BF16)
    b_r = jnp.concatenate([p['b_router'][l].astype(F32), jnp.full((LANES - N_EXPERTS,), NEG_BIG, F32)])
    return {
        'nmix': p['norm_mix'][l][None, :], 'w_in': w_in_r.astype(BF16),
        'gqa': jnp.tile(p['q_norm_a'][l] * HEAD_DIM ** -0.5, NA_HEADS)[None, :],
        'gka': jnp.tile(p['k_norm_a'][l], NA_HEADS)[None, :],
        'cqn': p['cq_norm'][l][None, :], 'w_uq': w_uq.astype(BF16),
        'ckvn': p['ckv_norm'][l][None, :], 'w_kk': w_kk.astype(BF16), 'w_uv': w_uv.astype(BF16),
        'gqb': _pad_heads(p['q_norm_b'][l], MLA_QK ** -0.5 * LOG2_E), 'gkb': _pad_heads(p['k_norm_b'][l]),
        'ga': out_norm[None, :A_WIDTH], 'gb_col': out_norm[A_WIDTH:A_WIDTH + B_WIDTH, None],
        'gc': out_norm[None, A_WIDTH + B_WIDTH:],
        'wo_a': w_out[:A_WIDTH].astype(BF16), 'wo_b': w_out[A_WIDTH:A_WIDTH + B_WIDTH].astype(BF16),
        'wo_c': w_out[A_WIDTH + B_WIDTH:].astype(BF16),
        'conv_w': p['conv_w'][l], 'nffn': p['norm_ffn'][l][None, :],
        'wr_hi': wr_hi, 'wr_lo': (w_r - wr_hi.astype(F32)).astype(BF16), 'b_r': b_r[None, :],
        'w_gate': p['w_gate'][l].astype(BF16), 'w_up': p['w_up'][l].astype(BF16),
        'w_down_t': jnp.swapaxes(p['w_down'][l], 1, 2).astype(BF16),
    }


def _layer(x, moe, lw, tabs, rpb_l):
    b, s, d = x.shape
    cap = CAPACITY_FACTOR * s // N_EXPERTS
    n_chunks = s // MOE_CHUNK

    outs = _proj_call(x, moe, lw, tabs)
    if moe is not None:
        qa, ka, va, qb, kb, vbt, u, bc, x = outs
    else:
        qa, ka, va, qb, kb, vbt, u, bc = outs
    out_a = _na_call(qa, ka, va, _bias_call(rpb_l))
    out_bt = _mla_call(qb, kb, vbt)
    x1, xn, aff_t = _out_call(x, out_a, out_bt, u, bc, lw)

    slot, gate, off = _topk_call(aff_t, tabs['tri'], cap)
    rows_per_chunk = MOE_CHUNK // LANES
    coff = off[:, :, 0, :s // LANES:rows_per_chunk].astype(jnp.int32)
    coff = jnp.concatenate([coff, jnp.full((b, N_EXPERTS, 1), cap, jnp.int32)], axis=-1).reshape(-1)
    slot5 = slot.reshape(b, N_EXPERTS, -1)[:, :, :s].reshape(b, N_EXPERTS, n_chunks, 1, MOE_CHUNK)
    gate5 = gate.reshape(b, N_EXPERTS, -1)[:, :, :s].reshape(b, N_EXPERTS, n_chunks, 1, MOE_CHUNK)

    xc = _gather_call(coff, xn, slot5, cap)
    y5 = _ffn_call(xc, lw['w_gate'], lw['w_up'], lw['w_down_t'])
    moe_out = _scatter_call(coff, y5, slot5, gate5)
    return x1, moe_out


def kernel(x, norm_mix, w_in, q_norm_a, k_norm_a, rpb, cq_norm, w_uq, ckv_norm, w_ukv, q_norm_b,
           k_norm_b, conv_w, out_norm, w_out, norm_ffn, w_router, b_router, w_gate, w_up, w_down):
    p = dict(norm_mix=norm_mix, w_in=w_in, q_norm_a=q_norm_a, k_norm_a=k_norm_a, cq_norm=cq_norm,
             w_uq=w_uq, ckv_norm=ckv_norm, w_ukv=w_ukv, q_norm_b=q_norm_b, k_norm_b=k_norm_b,
             conv_w=conv_w, out_norm=out_norm, w_out=w_out, norm_ffn=norm_ffn, w_router=w_router,
             b_router=b_router, w_gate=w_gate, w_up=w_up, w_down=w_down)
    b, s, d = x.shape
    assert d == D_MODEL and (CAPACITY_FACTOR * s // N_EXPERTS) % MOE_CHUNK == 0
    assert s % TOK_TILE == 0 and (s // GRID_W) % NA_ROWS == 0 and s <= TOPK_ROWS * LANES
    tabs = _tables(s)
    moe = None
    for l in range(norm_mix.shape[0]):
        x, moe = _layer(x, moe, _layer_weights(l, p), tabs, rpb[l])
    return _final_call(x, moe)
```

```python
import functools

import numpy as np
import jax
import jax.numpy as jnp
from jax import lax
from jax.experimental import pallas as pl
from jax.experimental.pallas import tpu as pltpu

D_MODEL = 1024
GRID_W = 64
HEAD_DIM = 64
EPS = 1e-6

NA_HEADS = 6
NA_WIN_H = 8
NA_WIN_W = 16
A_WIDTH = NA_HEADS * HEAD_DIM

MLA_HEADS = 6
MLA_Q_RANK = 256
MLA_KV_RANK = 128
MLA_NOPE = 64
MLA_ROPE = 32
MLA_QK = MLA_NOPE + MLA_ROPE
MLA_V = 64
B_WIDTH = MLA_HEADS * MLA_V
ROPE_THETA = 10000.0

CONV_CH = 256
C_WIDTH = CONV_CH

N_EXPERTS = 16
EXPERT_FF = 1024
CAPACITY_FACTOR = 2

LANES = 128
HEAD_PAD = 128
QB_WIDTH = MLA_HEADS * HEAD_PAD
IN_COLS_PAD = 2432
TOK_TILE = 512
NA_ROWS = 8
MOE_CHUNK = 256
MOE_COLS = 256
GATHER_COLS = 512
SCATTER_CHUNKS = 2
TOPK_GROUP = 4
GATHER_WIN = MOE_CHUNK + 16
TOPK_ROWS = 128
MLA_VROWS = MLA_V + 16
LOG2_E = 1.4426950408889634
MLA_UNROLL = 16
NEG_BIG = -1e30
VMEM_LIMIT = 56 * 1024 * 1024

F32 = jnp.float32
BF16 = jnp.bfloat16


def _dot(a, b):
    return jnp.dot(a, b, preferred_element_type=F32)


def _dot_nt(a, b):
    return lax.dot_general(a, b, (((1,), (1,)), ((), ())), preferred_element_type=F32)


def _rms(x, g):
    return x * lax.rsqrt(jnp.mean(x * x, axis=-1, keepdims=True) + EPS) * g


def _group_sumsq(y, e_ref):
    sq = (y * y).astype(BF16)
    width = y.shape[1]
    parts = []
    for c in range(0, width, 2 * LANES):
        n = min(2 * LANES, width - c)
        parts.append(_dot(sq[:, c:c + n], e_ref[0:n, 0:n]))
    return jnp.concatenate(parts, axis=1)


def _moe_tile_to_rows(moe_ref, n_chunks):
    rows = []
    for ch in range(n_chunks):
        cols = [moe_ref[0, cq, ch].T for cq in range(D_MODEL // MOE_COLS)]
        rows.append(jnp.concatenate(cols, axis=1))
    return jnp.concatenate(rows, axis=0)


def _proj_kernel(*refs, has_moe):
    if has_moe:
        x_ref, moe_ref = refs[0], refs[1]
        refs = refs[2:]
    else:
        x_ref = refs[0]
        refs = refs[1:]
    (nmix_ref, win_ref, gqa_ref, gka_ref, e384_ref, cqn_ref, wuq_ref, ckvn_ref, wkk_ref,
     wuv_ref, gqb_ref, gkb_ref, e768_ref, cos_ref, sin_ref) = refs[:15]
    outs = refs[15:]
    if has_moe:
        qa_o, ka_o, va_o, qb_o, kb_o, vbt_o, u_o, bc_o, xs_o = outs
    else:
        qa_o, ka_o, va_o, qb_o, kb_o, vbt_o, u_o, bc_o = outs

    x = x_ref[0]
    if has_moe:
        x = x + _moe_tile_to_rows(moe_ref, TOK_TILE // MOE_CHUNK)
        xs_o[0] = x
    xb = _rms(x, nmix_ref[...]).astype(BF16)

    pa = _dot(xb, win_ref[:, 0:3 * A_WIDTH])
    qa = pa[:, 0:A_WIDTH]
    ka = pa[:, A_WIDTH:2 * A_WIDTH]
    qa_o[0] = (qa * lax.rsqrt(_group_sumsq(qa, e384_ref) * (1.0 / HEAD_DIM) + EPS) * gqa_ref[...]).astype(BF16)
    ka_o[0] = (ka * lax.rsqrt(_group_sumsq(ka, e384_ref) * (1.0 / HEAD_DIM) + EPS) * gka_ref[...]).astype(BF16)
    va_o[0] = pa[:, 2 * A_WIDTH:3 * A_WIDTH].astype(BF16)

    c0 = 3 * A_WIDTH
    pb = _dot(xb, win_ref[:, c0:c0 + 512])
    cq = _rms(pb[:, 0:MLA_Q_RANK], cqn_ref[...]).astype(BF16)
    ckv = _rms(pb[:, MLA_Q_RANK:MLA_Q_RANK + MLA_KV_RANK], ckvn_ref[...]).astype(BF16)
    kr = pb[:, MLA_Q_RANK + MLA_KV_RANK:512].astype(BF16)
    qb_raw = _dot(cq, wuq_ref[...])
    kb_raw = _dot(jnp.concatenate([ckv, kr], axis=1), wkk_ref[...])
    vb = _dot(ckv, wuv_ref[...])

    cos = jnp.concatenate([cos_ref[...]] * MLA_HEADS, axis=1)
    sin = jnp.concatenate([sin_ref[...]] * MLA_HEADS, axis=1)
    lane = lax.broadcasted_iota(jnp.int32, (TOK_TILE, QB_WIDTH), 1) & (HEAD_PAD - 1)
    first_half = (lane >= MLA_NOPE) & (lane < MLA_NOPE + MLA_ROPE // 2)

    def norm_rope(y, g_ref):
        y = y * lax.rsqrt(_group_sumsq(y, e768_ref) * (1.0 / MLA_QK) + EPS) * g_ref[...]
        rot = jnp.where(first_half,
                        pltpu.roll(y, QB_WIDTH - MLA_ROPE // 2, 1),
                        pltpu.roll(y, MLA_ROPE // 2, 1))
        return y * cos + rot * sin

    qb_o[0] = norm_rope(qb_raw, gqb_ref).T.astype(BF16)
    kb_o[0] = norm_rope(kb_raw, gkb_ref).astype(BF16)
    vbt = vb.T
    for h in range(MLA_HEADS):
        vbt_o[0, h, 0, 0:MLA_V, :] = vbt[h * MLA_V:(h + 1) * MLA_V, :].astype(BF16)
        vbt_o[0, h, 0, MLA_V:MLA_VROWS, :] = jnp.ones((MLA_VROWS - MLA_V, TOK_TILE), BF16)

    c1 = c0 + 512
    pc = _dot(xb, win_ref[:, c1:c1 + 3 * CONV_CH])
    u_o[0] = pc[:, 2 * CONV_CH:3 * CONV_CH] * pc[:, 0:CONV_CH]
    bc_o[0] = pc[:, CONV_CH:2 * CONV_CH]


def _proj_call(x, moe, lw, tabs):
    b, s, d = x.shape
    nt = s // TOK_TILE
    has_moe = moe is not None

    def const(shape):
        return pl.BlockSpec(shape, lambda bi, ti: (0,) * len(shape))

    in_specs = [pl.BlockSpec((1, TOK_TILE, d), lambda bi, ti: (bi, ti, 0))]
    args = [x]
    if has_moe:
        ncq = D_MODEL // MOE_COLS
        in_specs.append(pl.BlockSpec((1, ncq, TOK_TILE // MOE_CHUNK, MOE_COLS, MOE_CHUNK),
                                     lambda bi, ti: (bi, 0, ti, 0, 0)))
        args.append(moe)
    consts = [lw['nmix'], lw['w_in'], lw['gqa'], lw['gka'], tabs['e384'], lw['cqn'], lw['w_uq'],
              lw['ckvn'], lw['w_kk'], lw['w_uv'], lw['gqb'], lw['gkb'], tabs['e768']]
    for c in consts:
        in_specs.append(const(c.shape))
        args.append(c)
    in_specs += [pl.BlockSpec((TOK_TILE, HEAD_PAD), lambda bi, ti: (ti, 0))] * 2
    args += [tabs['cos'], tabs['sin']]

    def tok(width, dtype):
        return (jax.ShapeDtypeStruct((b, s, width), dtype),
                pl.BlockSpec((1, TOK_TILE, width), lambda bi, ti: (bi, ti, 0)))

    outs = [tok(A_WIDTH, BF16), tok(A_WIDTH, BF16), tok(A_WIDTH, BF16),
            (jax.ShapeDtypeStruct((b, QB_WIDTH, s), BF16),
             pl.BlockSpec((1, QB_WIDTH, TOK_TILE), lambda bi, ti: (bi, 0, ti))),
            tok(QB_WIDTH, BF16),
            (jax.ShapeDtypeStruct((b, MLA_HEADS, nt, MLA_VROWS, TOK_TILE), BF16),
             pl.BlockSpec((1, MLA_HEADS, 1, MLA_VROWS, TOK_TILE), lambda bi, ti: (bi, 0, ti, 0, 0))),
            tok(CONV_CH, F32), tok(CONV_CH, F32)]
    if has_moe:
        outs.append(tok(d, F32))
    return pl.pallas_call(
        functools.partial(_proj_kernel, has_moe=has_moe),
        grid=(b, nt),
        in_specs=in_specs,
        out_specs=[o[1] for o in outs],
        out_shape=[o[0] for o in outs],
        compiler_params=pltpu.CompilerParams(
            dimension_semantics=("arbitrary", "arbitrary"), vmem_limit_bytes=VMEM_LIMIT),
        name="proj_moe" if has_moe else "proj",
    )(*args)


def _bias_kernel(rpb_ref, o_ref):
    h = pl.program_id(0)
    delta = pl.program_id(1)
    nw = 2 * NA_WIN_W - 1
    shape = (GRID_W, NA_WIN_H * GRID_W)
    c = lax.broadcasted_iota(jnp.int32, shape, 0)
    col = lax.broadcasted_iota(jnp.int32, shape, 1)
    wrow = lax.shift_right_logical(col, 6)
    kc = col & (GRID_W - 1)
    c_lo = jnp.clip(c - NA_WIN_W // 2, 0, GRID_W - NA_WIN_W)
    inside = (kc >= c_lo) & (kc < c_lo + NA_WIN_W)
    d = kc - c + (NA_WIN_W - 1)
    acc = jnp.zeros(shape, F32)
    base = h * ((2 * NA_WIN_H - 1) * nw)
    for i in range(NA_WIN_H):
        ri = i - delta + (NA_WIN_H - 1)
        for dd in range(nw):
            val = rpb_ref[base + ri * nw + dd]
            acc = jnp.where((wrow == i) & (d == dd), val, acc)
    o_ref[0, 0] = jnp.where(inside, acc, NEG_BIG)


def _bias_call(rpb_l):
    flat = rpb_l.reshape(-1)
    return pl.pallas_call(
        _bias_kernel,
        grid=(NA_HEADS, NA_WIN_H),
        in_specs=[pl.BlockSpec(memory_space=pltpu.SMEM)],
        out_specs=pl.BlockSpec((1, 1, GRID_W, NA_WIN_H * GRID_W), lambda h, dl: (h, dl, 0, 0)),
        out_shape=jax.ShapeDtypeStruct((NA_HEADS, NA_WIN_H, GRID_W, NA_WIN_H * GRID_W), F32),
        name="na_bias",
    )(flat)


def _na_kernel(q_ref, kp_ref, kc_ref, kn_ref, vp_ref, vc_ref, vn_ref, bias_ref, o_ref,
               k_scr, v_scr, *, rows):
    j = pl.program_id(1)
    blk = NA_ROWS * GRID_W
    k_scr[0:blk] = kp_ref[0]
    k_scr[blk:2 * blk] = kc_ref[0]
    k_scr[2 * blk:3 * blk] = kn_ref[0]
    v_scr[0:blk] = vp_ref[0]
    v_scr[blk:2 * blk] = vc_ref[0]
    v_scr[2 * blk:3 * blk] = vn_ref[0]
    lane = lax.broadcasted_iota(jnp.int32, (GRID_W, LANES), 1)
    low = lane < HEAD_DIM
    win = NA_WIN_H * GRID_W

    def window_start(i):
        r = NA_ROWS * j + i
        r0 = jnp.clip(r - NA_WIN_H // 2, 0, rows - NA_WIN_H)
        return r - r0, pl.multiple_of((r0 - NA_ROWS * j + NA_ROWS) * GRID_W, GRID_W)

    tiles = []
    for i in range(NA_ROWS):
        delta, start = window_start(i)
        for p in range(NA_HEADS // 2):
            lsl = slice(p * LANES, (p + 1) * LANES)
            qp = q_ref[0, i * GRID_W:(i + 1) * GRID_W, lsl]
            kw = k_scr[pl.ds(start, win), lsl]
            for a in range(2):
                qm = jnp.where(low if a == 0 else jnp.logical_not(low), qp, jnp.zeros_like(qp))
                tiles.append(_dot_nt(qm, kw) + bias_ref[2 * p + a, delta])
    sc = jnp.concatenate(tiles, axis=0)
    e = jnp.exp(sc - jnp.max(sc, axis=-1, keepdims=True))
    inv = 1.0 / jnp.sum(e, axis=-1, keepdims=True)
    eb = e.astype(BF16)
    t = 0
    for i in range(NA_ROWS):
        _, start = window_start(i)
        for p in range(NA_HEADS // 2):
            lsl = slice(p * LANES, (p + 1) * LANES)
            vw = v_scr[pl.ds(start, win), lsl]
            res = []
            for a in range(2):
                rsl = slice(t * GRID_W, (t + 1) * GRID_W)
                res.append(_dot(eb[rsl], vw) * inv[rsl])
                t += 1
            o_ref[0, i * GRID_W:(i + 1) * GRID_W, lsl] = jnp.where(low, res[0], res[1])


def _na_call(qa, ka, va, bias_tab):
    b, s, w = qa.shape
    rows = s // GRID_W
    nb = rows // NA_ROWS
    blk = NA_ROWS * GRID_W
    cur = pl.BlockSpec((1, blk, w), lambda bi, j: (bi, j, 0))
    prev = pl.BlockSpec((1, blk, w), lambda bi, j: (bi, jnp.maximum(j - 1, 0), 0))
    nxt = pl.BlockSpec((1, blk, w), lambda bi, j: (bi, jnp.minimum(j + 1, nb - 1), 0))
    return pl.pallas_call(
        functools.partial(_na_kernel, rows=rows),
        grid=(b, nb),
        in_specs=[cur, prev, cur, nxt, prev, cur, nxt,
                  pl.BlockSpec(bias_tab.shape, lambda bi, j: (0, 0, 0, 0))],
        out_specs=pl.BlockSpec((1, blk, w), lambda bi, j: (bi, j, 0)),
        out_shape=jax.ShapeDtypeStruct((b, s, w), F32),
        scratch_shapes=[pltpu.VMEM((3 * blk, w), BF16), pltpu.VMEM((3 * blk, w), BF16)],
        compiler_params=pltpu.CompilerParams(
            dimension_semantics=("arbitrary", "arbitrary"), vmem_limit_bytes=VMEM_LIMIT),
        name="na_attn",
    )(qa, ka, ka, ka, va, va, va, bias_tab)


def _mla_kernel(q_ref, k_ref, vt_ref, o_ref, s_scr, acc_scr, m_scr, *, n_chunks):
    qt = q_ref[0]

    def scores(c):
        kb = k_ref[0, pl.ds(pl.multiple_of(c * TOK_TILE, TOK_TILE), TOK_TILE), :]
        return _dot(kb, qt)

    def update(st, c):
        m = m_scr[...]
        m_new = jnp.maximum(m, jnp.max(st, axis=0, keepdims=True))
        alpha = jnp.exp2(m - m_new)
        p = jnp.exp2(st - m_new).astype(BF16)
        acc_scr[...] = alpha * acc_scr[...] + _dot(vt_ref[0, 0, c], p)
        m_scr[...] = m_new

    m_scr[...] = jnp.full(m_scr.shape, NEG_BIG, F32)
    acc_scr[...] = jnp.zeros(acc_scr.shape, F32)
    s_scr[0] = scores(0)

    def body(i, carry):
        for j in range(MLA_UNROLL):
            c = MLA_UNROLL * i + j
            s_scr[(j + 1) % 2] = scores(jnp.minimum(c + 1, n_chunks - 1))
            update(s_scr[j % 2], c)
        return carry

    lax.fori_loop(0, n_chunks // MLA_UNROLL, body, 0)
    acc = acc_scr[...]
    o_ref[0] = acc[0:MLA_V, :] / acc[MLA_V:MLA_V + 1, :]


def _mla_call(qb, kb, vbt):
    b, s, _ = kb.shape
    nt = s // TOK_TILE
    assert nt % MLA_UNROLL == 0 and MLA_UNROLL % 2 == 0
    return pl.pallas_call(
        functools.partial(_mla_kernel, n_chunks=nt),
        grid=(b, MLA_HEADS, nt),
        in_specs=[pl.BlockSpec((1, HEAD_PAD, TOK_TILE), lambda bi, h, qi: (bi, h, qi)),
                  pl.BlockSpec((1, s, HEAD_PAD), lambda bi, h, qi: (bi, 0, h)),
                  pl.BlockSpec((1, 1, nt, MLA_VROWS, TOK_TILE), lambda bi, h, qi: (bi, h, 0, 0, 0))],
        out_specs=pl.BlockSpec((1, MLA_V, TOK_TILE), lambda bi, h, qi: (bi, h, qi)),
        out_shape=jax.ShapeDtypeStruct((b, B_WIDTH, s), F32),
        scratch_shapes=[pltpu.VMEM((2, TOK_TILE, TOK_TILE), F32), pltpu.VMEM((MLA_VROWS, TOK_TILE), F32),
                        pltpu.VMEM((1, TOK_TILE), F32)],
        compiler_params=pltpu.CompilerParams(
            dimension_semantics=("arbitrary", "arbitrary", "arbitrary"), vmem_limit_bytes=VMEM_LIMIT),
        name="mla_attn",
    )(qb, kb, vbt)


def _out_kernel(x_ref, oa_ref, obt_ref, u_ref, up_ref, un_ref, bc_ref, ga_ref, gb_ref, gc_ref,
                woa_ref, wob_ref, woc_ref, cw_ref, nffn_ref, wrh_ref, wrl_ref, br_ref,
                x1_o, xn_o, aff_o):
    ti = pl.program_id(1)
    nt = pl.num_programs(1)
    t = TOK_TILE

    na = _rms(oa_ref[0], ga_ref[...]).astype(BF16)
    acc = _dot(na, woa_ref[...])

    obt = obt_ref[0]
    nbt = obt * lax.rsqrt(jnp.mean(obt * obt, axis=0, keepdims=True) + EPS) * gb_ref[...]
    acc = acc + _dot(nbt.T.astype(BF16), wob_ref[...])

    u = u_ref[0]
    row = lax.broadcasted_iota(jnp.int32, (t, CONV_CH), 0)
    prev_row = jnp.where(ti > 0, up_ref[0, 7:8, :], 0.0)
    next_row = jnp.where(ti < nt - 1, un_ref[0, 0:1, :], 0.0)
    u_m1 = jnp.where(row == 0, prev_row, pltpu.roll(u, 1, 0))
    u_p1 = jnp.where(row == t - 1, next_row, pltpu.roll(u, t - 1, 0))
    y = cw_ref[0:1, :] * u_m1 + cw_ref[1:2, :] * u + cw_ref[2:3, :] * u_p1
    oc = _rms(bc_ref[0] * y, gc_ref[...]).astype(BF16)
    acc = acc + _dot(oc, woc_ref[...])

    x1 = x_ref[0] + acc
    x1_o[0] = x1
    xn = _rms(x1, nffn_ref[...])
    hi = xn.astype(BF16)
    lo = (xn - hi.astype(F32)).astype(BF16)
    xn_o[0] = hi
    logits = _dot(hi, wrh_ref[...]) + _dot(lo, wrh_ref[...]) + _dot(hi, wrl_ref[...]) + br_ref[...]
    m = jnp.max(logits, axis=-1, keepdims=True)
    e = jnp.exp(logits - m)
    aff = e / jnp.sum(e, axis=-1, keepdims=True)
    aff_o[0] = aff.T[0:N_EXPERTS, :]


def _out_call(x, out_a, out_bt, u, bc, lw):
    b, s, d = x.shape
    nt = s // TOK_TILE
    h8 = TOK_TILE // 8

    def const(a):
        return pl.BlockSpec(a.shape, lambda bi, ti: (0,) * a.ndim)

    def tok(width):
        return pl.BlockSpec((1, TOK_TILE, width), lambda bi, ti: (bi, ti, 0))

    consts = [lw['ga'], lw['gb_col'], lw['gc'], lw['wo_a'], lw['wo_b'], lw['wo_c'], lw['conv_w'],
              lw['nffn'], lw['wr_hi'], lw['wr_lo'], lw['b_r']]
    in_specs = [tok(d), tok(A_WIDTH),
                pl.BlockSpec((1, B_WIDTH, TOK_TILE), lambda bi, ti: (bi, 0, ti)),
                tok(CONV_CH),
                pl.BlockSpec((1, 8, CONV_CH), lambda bi, ti: (bi, jnp.maximum(ti * h8 - 1, 0), 0)),
                pl.BlockSpec((1, 8, CONV_CH), lambda bi, ti: (bi, jnp.minimum((ti + 1) * h8, s // 8 - 1), 0)),
                tok(CONV_CH)] + [const(c) for c in consts]
    return pl.pallas_call(
        _out_kernel,
        grid=(b, nt),
        in_specs=in_specs,
        out_specs=[tok(d), tok(d), pl.BlockSpec((1, N_EXPERTS, TOK_TILE), lambda bi, ti: (bi, 0, ti))],
        out_shape=[jax.ShapeDtypeStruct((b, s, d), F32), jax.ShapeDtypeStruct((b, s, d), BF16),
                   jax.ShapeDtypeStruct((b, N_EXPERTS, s), F32)],
        compiler_params=pltpu.CompilerParams(
            dimension_semantics=("arbitrary", "arbitrary"), vmem_limit_bytes=VMEM_LIMIT),
        name="out_proj",
    )(x, out_a, out_bt, u, u, u, bc, *consts)


def _topk_kernel(a_ref, tri_ref, o_slot, o_gate, o_off, *, cap):
    def total(mask):
        v = jnp.where(mask, 1.0, 0.0)
        return jnp.sum(jnp.sum(v, axis=0, keepdims=True), axis=1, keepdims=True)

    group = range(TOPK_GROUP)
    all_bits = [pltpu.bitcast(a_ref[0, t], jnp.int32) for t in group]
    thrs = [jnp.zeros((1, 1), jnp.int32) for _ in group]
    for bit in range(30, -1, -1):
        cands = [thrs[t] | (1 << bit) for t in group]
        thrs = [jnp.where(total(all_bits[t] >= cands[t]) >= cap, cands[t], thrs[t]) for t in group]
    for t in group:
        _topk_finish(a_ref[0, t], all_bits[t], thrs[t], total, tri_ref, o_slot, o_gate, o_off, t, cap)


def _topk_finish(a, bits, thr, total, tri_ref, o_slot, o_gate, o_off, t, cap):
    gt = bits > thr
    eq = bits == thr
    need = cap - total(gt)

    upper = tri_ref[0]
    lower_strict = tri_ref[1]
    upper_strict = tri_ref[2]

    def excl_rank(mask):
        f = jnp.where(mask, 1.0, 0.0)
        fb = f.astype(BF16)
        in_row = _dot(fb, upper)
        before_rows = jnp.sum(_dot(lower_strict, fb), axis=1, keepdims=True)
        return before_rows + in_row - f, fb

    eq_rank, _ = excl_rank(eq)
    sel = gt | (eq & (eq_rank < need))
    pos, sel_b = excl_rank(sel)
    o_slot[0, t] = jnp.where(sel, pos, -1.0).astype(jnp.int32)
    o_gate[0, t] = jnp.where(sel, a, 0.0)
    row_tot = _dot_nt(jnp.ones((8, LANES), BF16), sel_b)
    o_off[0, t] = _dot(row_tot.astype(BF16), upper_strict)


def _topk_call(aff_t, tri, cap):
    b, e, s = aff_t.shape
    pad = TOPK_ROWS * LANES - s
    a = aff_t if pad == 0 else jnp.pad(aff_t, ((0, 0), (0, 0), (0, pad)), constant_values=-1.0)
    a = a.reshape(b, e, TOPK_ROWS, LANES)
    blk = pl.BlockSpec((1, TOPK_GROUP, TOPK_ROWS, LANES), lambda bi, ei: (bi, ei, 0, 0))
    return pl.pallas_call(
        functools.partial(_topk_kernel, cap=cap),
        grid=(b, e // TOPK_GROUP),
        in_specs=[blk, pl.BlockSpec(tri.shape, lambda bi, ei: (0, 0, 0))],
        out_specs=[blk, blk, pl.BlockSpec((1, TOPK_GROUP, 8, LANES), lambda bi, ei: (bi, ei, 0, 0))],
        out_shape=[jax.ShapeDtypeStruct((b, e, TOPK_ROWS, LANES), jnp.int32),
                   jax.ShapeDtypeStruct((b, e, TOPK_ROWS, LANES), F32),
                   jax.ShapeDtypeStruct((b, e, 8, LANES), F32)],
        name="topk_select",
    )(a, tri)


def _window_onehot(slot_row, start):
    i = lax.broadcasted_iota(jnp.int32, (2 * MOE_CHUNK, MOE_CHUNK), 0)
    return jnp.where(i == slot_row - start, 1.0, 0.0).astype(BF16)


def _window_tile(off, tiles):
    return jnp.minimum(lax.shift_right_logical(off, 8), tiles - 1)


def _gather_kernel(coff_ref, x_ref, slot_ref, o_ref, acc_ref, *, n_chunks, cap):
    bi = pl.program_id(0)
    ei = pl.program_id(2)
    base = (bi * N_EXPERTS + ei) * (n_chunks + 1)
    acc_ref[...] = jnp.zeros_like(acc_ref)

    row = lax.broadcasted_iota(jnp.int32, (GATHER_WIN, MOE_CHUNK), 0)

    def body(k, carry):
        off = coff_ref[base + k]
        start = pl.multiple_of(jnp.minimum(lax.shift_left(lax.shift_right_logical(off, 3), 3), cap), 8)
        g = jnp.where(row == slot_ref[0, 0, k] - start, 1.0, 0.0).astype(BF16)
        xk = x_ref[0, pl.ds(pl.multiple_of(k * MOE_CHUNK, MOE_CHUNK), MOE_CHUNK), :]
        acc_ref[pl.ds(start, GATHER_WIN), :] += _dot(g, xk)
        return carry

    lax.fori_loop(0, n_chunks, body, 0, unroll=2)
    o_ref[0, 0] = acc_ref[0:cap, :].astype(BF16)


def _gather_call(coff, xn, slot5, cap):
    b, s, d = xn.shape
    n_chunks = s // MOE_CHUNK
    ncq = d // GATHER_COLS
    grid_spec = pltpu.PrefetchScalarGridSpec(
        num_scalar_prefetch=1,
        grid=(b, ncq, N_EXPERTS),
        in_specs=[pl.BlockSpec((1, s, GATHER_COLS), lambda bi, cq, ei, co: (bi, 0, cq)),
                  pl.BlockSpec((1, 1, n_chunks, 1, MOE_CHUNK), lambda bi, cq, ei, co: (bi, ei, 0, 0, 0))],
        out_specs=pl.BlockSpec((1, 1, cap, GATHER_COLS), lambda bi, cq, ei, co: (bi, ei, 0, cq)),
        scratch_shapes=[pltpu.VMEM((cap + GATHER_WIN, GATHER_COLS), F32)],
    )
    return pl.pallas_call(
        functools.partial(_gather_kernel, n_chunks=n_chunks, cap=cap),
        grid_spec=grid_spec,
        out_shape=jax.ShapeDtypeStruct((b, N_EXPERTS, cap, d), BF16),
        compiler_params=pltpu.CompilerParams(
            dimension_semantics=("arbitrary", "arbitrary", "arbitrary"), vmem_limit_bytes=VMEM_LIMIT),
        name="moe_gather",
    )(coff, xn, slot5)


def _ffn_kernel(xc_ref, wg_ref, wu_ref, wdt_ref, o_ref, *, tiles):
    xc = xc_ref[0, 0]
    g = _dot(xc, wg_ref[0])
    u = _dot(xc, wu_ref[0])
    h = (g / (1.0 + jnp.exp(-g)) * u).astype(BF16)
    yt = _dot_nt(wdt_ref[0], h)
    for r in range(tiles):
        o_ref[0, 0, r] = yt[:, r * MOE_CHUNK:(r + 1) * MOE_CHUNK].astype(BF16)


def _ffn_call(xc, wg, wu, wdt):
    b, e, cap, d = xc.shape
    ff = wg.shape[-1]
    rt = min(cap, 1024)
    tiles = rt // MOE_CHUNK
    return pl.pallas_call(
        functools.partial(_ffn_kernel, tiles=tiles),
        grid=(e, b, cap // rt),
        in_specs=[pl.BlockSpec((1, 1, rt, d), lambda ei, bi, ri: (bi, ei, ri, 0)),
                  pl.BlockSpec((1, d, ff), lambda ei, bi, ri: (ei, 0, 0)),
                  pl.BlockSpec((1, d, ff), lambda ei, bi, ri: (ei, 0, 0)),
                  pl.BlockSpec((1, d, ff), lambda ei, bi, ri: (ei, 0, 0))],
        out_specs=pl.BlockSpec((1, 1, tiles, d, MOE_CHUNK), lambda ei, bi, ri: (bi, ei, ri, 0, 0)),
        out_shape=jax.ShapeDtypeStruct((b, e, cap // MOE_CHUNK, d, MOE_CHUNK), BF16),
        compiler_params=pltpu.CompilerParams(
            dimension_semantics=("arbitrary", "arbitrary", "arbitrary"), vmem_limit_bytes=VMEM_LIMIT),
        name="moe_ffn",
    )(xc, wg, wu, wdt)


def _scatter_kernel(coff_ref, y_ref, slot_ref, gate_ref, o_ref, *, n_chunks, tiles):
    bi = pl.program_id(0)
    kg = pl.program_id(2)
    for j in range(SCATTER_CHUNKS):
        k = kg * SCATTER_CHUNKS + j
        acc = jnp.zeros((MOE_COLS, MOE_CHUNK), F32)
        for e in range(N_EXPERTS):
            r0 = _window_tile(coff_ref[(bi * N_EXPERTS + e) * (n_chunks + 1) + k], tiles)
            r1 = jnp.minimum(r0 + 1, tiles - 1)
            g = _window_onehot(slot_ref[0, e, j], r0 * MOE_CHUNK)
            yw = jnp.concatenate([y_ref[0, e, r0], y_ref[0, e, r1]], axis=1)
            acc = acc + _dot(yw, g) * gate_ref[0, e, j]
        o_ref[0, 0, j] = acc


def _scatter_call(coff, y5, slot5, gate5):
    b, e, tiles, d, _ = y5.shape
    n_chunks = slot5.shape[2]
    ncq = d // MOE_COLS
    small = pl.BlockSpec((1, e, SCATTER_CHUNKS, 1, MOE_CHUNK), lambda bi, cq, kg, co: (bi, 0, kg, 0, 0))
    grid_spec = pltpu.PrefetchScalarGridSpec(
        num_scalar_prefetch=1,
        grid=(b, ncq, n_chunks // SCATTER_CHUNKS),
        in_specs=[pl.BlockSpec((1, e, tiles, MOE_COLS, MOE_CHUNK), lambda bi, cq, kg, co: (bi, 0, 0, cq, 0)),
                  small, small],
        out_specs=pl.BlockSpec((1, 1, SCATTER_CHUNKS, MOE_COLS, MOE_CHUNK),
                               lambda bi, cq, kg, co: (bi, cq, kg, 0, 0)),
    )
    return pl.pallas_call(
        functools.partial(_scatter_kernel, n_chunks=n_chunks, tiles=tiles),
        grid_spec=grid_spec,
        out_shape=jax.ShapeDtypeStruct((b, ncq, n_chunks, MOE_COLS, MOE_CHUNK), F32),
        compiler_params=pltpu.CompilerParams(
            dimension_semantics=("arbitrary", "arbitrary", "arbitrary"), vmem_limit_bytes=VMEM_LIMIT),
        name="moe_scatter",
    )(coff, y5, slot5, gate5)


def _final_kernel(x_ref, moe_ref, o_ref):
    o_ref[0] = x_ref[0] + _moe_tile_to_rows(moe_ref, TOK_TILE // MOE_CHUNK)


def _final_call(x1, moe):
    b, s, d = x1.shape
    ncq = d // MOE_COLS
    tok = pl.BlockSpec((1, TOK_TILE, d), lambda bi, ti: (bi, ti, 0))
    return pl.pallas_call(
        _final_kernel,
        grid=(b, s // TOK_TILE),
        in_specs=[tok, pl.BlockSpec((1, ncq, TOK_TILE // MOE_CHUNK, MOE_COLS, MOE_CHUNK),
                                    lambda bi, ti: (bi, 0, ti, 0, 0))],
        out_specs=tok,
        out_shape=jax.ShapeDtypeStruct((b, s, d), F32),
        compiler_params=pltpu.CompilerParams(
            dimension_semantics=("arbitrary", "arbitrary"), vmem_limit_bytes=VMEM_LIMIT),
        name="final_add",
    )(x1, moe)


def _block_ones(n, group):
    i = np.arange(n)
    return jnp.asarray((i[:, None] // group) == (i[None, :] // group), BF16)


def _tables(s):
    half = MLA_ROPE // 2
    inv = ROPE_THETA ** (-jnp.arange(half, dtype=F32) / half)
    ang = jnp.arange(s, dtype=jnp.int32).astype(F32)[:, None] * inv[None, :]
    cos, sin = jnp.cos(ang), jnp.sin(ang)
    ones = jnp.ones((s, MLA_NOPE), F32)
    tail = HEAD_PAD - MLA_QK
    cos_t = jnp.concatenate([ones, cos, cos, jnp.ones((s, tail), F32)], axis=1)
    sin_t = jnp.concatenate([0 * ones, -sin, sin, jnp.zeros((s, tail), F32)], axis=1)
    i = np.arange(LANES)
    tri = np.stack([i[:, None] <= i[None, :], i[None, :] < i[:, None], i[:, None] < i[None, :]])
    return {'cos': cos_t, 'sin': sin_t, 'e384': _block_ones(2 * LANES, HEAD_DIM),
            'e768': _block_ones(2 * LANES, HEAD_PAD), 'tri': jnp.asarray(tri, BF16)}


def _pad_heads(v, scale=1.0):
    g = jnp.pad(v.astype(F32) * scale, (0, HEAD_PAD - MLA_QK))
    return jnp.tile(g, MLA_HEADS)[None, :]


def _layer_weights(l, p):
    w_in = p['w_in'][l]
    kr_end = 3 * A_WIDTH + MLA_Q_RANK + MLA_KV_RANK + MLA_ROPE
    w_in_r = jnp.concatenate(
        [w_in[:, :kr_end], jnp.zeros((D_MODEL, LANES - MLA_ROPE), F32), w_in[:, kr_end:]], axis=1)
    w_uq = p['w_uq'][l].reshape(MLA_Q_RANK, MLA_HEADS, MLA_QK)
    w_uq = jnp.pad(w_uq, ((0, 0), (0, 0), (0, HEAD_PAD - MLA_QK))).reshape(MLA_Q_RANK, QB_WIDTH)
    w_ukv = p['w_ukv'][l].reshape(MLA_KV_RANK, MLA_HEADS, MLA_NOPE + MLA_V)
    w_uk = jnp.pad(w_ukv[:, :, :MLA_NOPE], ((0, 0), (0, 0), (0, HEAD_PAD - MLA_NOPE)))
    w_uk = w_uk.reshape(MLA_KV_RANK, QB_WIDTH)
    w_uv = w_ukv[:, :, MLA_NOPE:].reshape(MLA_KV_RANK, B_WIDTH)
    place = np.zeros((LANES, QB_WIDTH), np.float32)
    for h in range(MLA_HEADS):
        place[np.arange(MLA_ROPE), h * HEAD_PAD + MLA_NOPE + np.arange(MLA_ROPE)] = 1.0
    w_kk = jnp.concatenate([w_uk, jnp.asarray(place)], axis=0)
    out_norm = p['out_norm'][l]
    w_out = p['w_out'][l]
    w_r = jnp.pad(p['w_router'][l], ((0, 0), (0, LANES - N_EXPERTS)))
    wr_hi = w_r.astype(BF16)
    b_r = jnp.concatenate([p['b_router'][l].astype(F32), jnp.full((LANES - N_EXPERTS,), NEG_BIG, F32)])
    return {
        'nmix': p['norm_mix'][l][None, :], 'w_in': w_in_r.astype(BF16),
        'gqa': jnp.tile(p['q_norm_a'][l] * HEAD_DIM ** -0.5, NA_HEADS)[None, :],
        'gka': jnp.tile(p['k_norm_a'][l], NA_HEADS)[None, :],
        'cqn': p['cq_norm'][l][None, :], 'w_uq': w_uq.astype(BF16),
        'ckvn': p['ckv_norm'][l][None, :], 'w_kk': w_kk.astype(BF16), 'w_uv': w_uv.astype(BF16),
        'gqb': _pad_heads(p['q_norm_b'][l], MLA_QK ** -0.5 * LOG2_E), 'gkb': _pad_heads(p['k_norm_b'][l]),
        'ga': out_norm[None, :A_WIDTH], 'gb_col': out_norm[A_WIDTH:A_WIDTH + B_WIDTH, None],
        'gc': out_norm[None, A_WIDTH + B_WIDTH:],
        'wo_a': w_out[:A_WIDTH].astype(BF16), 'wo_b': w_out[A_WIDTH:A_WIDTH + B_WIDTH].astype(BF16),
        'wo_c': w_out[A_WIDTH + B_WIDTH:].astype(BF16),
        'conv_w': p['conv_w'][l], 'nffn': p['norm_ffn'][l][None, :],
        'wr_hi': wr_hi, 'wr_lo': (w_r - wr_hi.astype(F32)).astype(BF16), 'b_r': b_r[None, :],
        'w_gate': p['w_gate'][l].astype(BF16), 'w_up': p['w_up'][l].astype(BF16),
        'w_down_t': jnp.swapaxes(p['w_down'][l], 1, 2).astype(BF16),
    }


def _layer(x, moe, lw, tabs, rpb_l):
    b, s, d = x.shape
    cap = CAPACITY_FACTOR * s // N_EXPERTS
    n_chunks = s // MOE_CHUNK

    outs = _proj_call(x, moe, lw, tabs)
    if moe is not None:
        qa, ka, va, qb, kb, vbt, u, bc, x = outs
    else:
        qa, ka, va, qb, kb, vbt, u, bc = outs
    out_a = _na_call(qa, ka, va, _bias_call(rpb_l))
    out_bt = _mla_call(qb, kb, vbt)
    x1, xn, aff_t = _out_call(x, out_a, out_bt, u, bc, lw)

    slot, gate, off = _topk_call(aff_t, tabs['tri'], cap)
    rows_per_chunk = MOE_CHUNK // LANES
    coff = off[:, :, 0, :s // LANES:rows_per_chunk].astype(jnp.int32)
    coff = jnp.concatenate([coff, jnp.full((b, N_EXPERTS, 1), cap, jnp.int32)], axis=-1).reshape(-1)
    slot5 = slot.reshape(b, N_EXPERTS, -1)[:, :, :s].reshape(b, N_EXPERTS, n_chunks, 1, MOE_CHUNK)
    gate5 = gate.reshape(b, N_EXPERTS, -1)[:, :, :s].reshape(b, N_EXPERTS, n_chunks, 1, MOE_CHUNK)

    xc = _gather_call(coff, xn, slot5, cap)
    y5 = _ffn_call(xc, lw['w_gate'], lw['w_up'], lw['w_down_t'])
    moe_out = _scatter_call(coff, y5, slot5, gate5)
    return x1, moe_out


def kernel(x, norm_mix, w_in, q_norm_a, k_norm_a, rpb, cq_norm, w_uq, ckv_norm, w_ukv, q_norm_b,
           k_norm_b, conv_w, out_norm, w_out, norm_ffn, w_router, b_router, w_gate, w_up, w_down):
    p = dict(norm_mix=norm_mix, w_in=w_in, q_norm_a=q_norm_a, k_norm_a=k_norm_a, cq_norm=cq_norm,
             w_uq=w_uq, ckv_norm=ckv_norm, w_ukv=w_ukv, q_norm_b=q_norm_b, k_norm_b=k_norm_b,
             conv_w=conv_w, out_norm=out_norm, w_out=w_out, norm_ffn=norm_ffn, w_router=w_router,
             b_router=b_router, w_gate=w_gate, w_up=w_up, w_down=w_down)
    b, s, d = x.shape
    assert d == D_MODEL and (CAPACITY_FACTOR * s // N_EXPERTS) % MOE_CHUNK == 0
    assert s % TOK_TILE == 0 and (s // GRID_W) % NA_ROWS == 0 and s <= TOPK_ROWS * LANES
    tabs = _tables(s)
    moe = None
    for l in range(norm_mix.shape[0]):
        x, moe = _layer(x, moe, _layer_weights(l, p), tabs, rpb[l])
    return _final_call(x, moe)
```

```python
import functools

import numpy as np
import jax
import jax.numpy as jnp
from jax import lax
from jax.experimental import pallas as pl
from jax.experimental.pallas import tpu as pltpu

D_MODEL = 1024
GRID_W = 64
HEAD_DIM = 64
EPS = 1e-6

NA_HEADS = 6
NA_WIN_H = 8
NA_WIN_W = 16
A_WIDTH = NA_HEADS * HEAD_DIM

MLA_HEADS = 6
MLA_Q_RANK = 256
MLA_KV_RANK = 128
MLA_NOPE = 64
MLA_ROPE = 32
MLA_QK = MLA_NOPE + MLA_ROPE
MLA_V = 64
B_WIDTH = MLA_HEADS * MLA_V
ROPE_THETA = 10000.0

CONV_CH = 256
C_WIDTH = CONV_CH

N_EXPERTS = 16
EXPERT_FF = 1024
CAPACITY_FACTOR = 2

LANES = 128
HEAD_PAD = 128
QB_WIDTH = MLA_HEADS * HEAD_PAD
IN_COLS_PAD = 2432
TOK_TILE = 512
NA_ROWS = 8
MOE_CHUNK = 256
MOE_COLS = 256
GATHER_COLS = 512
SCATTER_CHUNKS = 2
TOPK_GROUP = 4
GATHER_WIN = MOE_CHUNK + 16
TOPK_ROWS = 128
MLA_VROWS = MLA_V + 16
LOG2_E = 1.4426950408889634
MLA_UNROLL = 16
NEG_BIG = -1e30
VMEM_LIMIT = 56 * 1024 * 1024

F32 = jnp.float32
BF16 = jnp.bfloat16


def _dot(a, b):
    return jnp.dot(a, b, preferred_element_type=F32)


def _dot_nt(a, b):
    return lax.dot_general(a, b, (((1,), (1,)), ((), ())), preferred_element_type=F32)


def _rms(x, g):
    return x * lax.rsqrt(jnp.mean(x * x, axis=-1, keepdims=True) + EPS) * g


def _group_sumsq(y, e_ref):
    sq = (y * y).astype(BF16)
    width = y.shape[1]
    parts = []
    for c in range(0, width, 2 * LANES):
        n = min(2 * LANES, width - c)
        parts.append(_dot(sq[:, c:c + n], e_ref[0:n, 0:n]))
    return jnp.concatenate(parts, axis=1)


def _moe_tile_to_rows(moe_ref, n_chunks):
    rows = []
    for ch in range(n_chunks):
        cols = [moe_ref[0, cq, ch].T for cq in range(D_MODEL // MOE_COLS)]
        rows.append(jnp.concatenate(cols, axis=1))
    return jnp.concatenate(rows, axis=0)


def _proj_kernel(*refs, has_moe):
    if has_moe:
        x_ref, moe_ref = refs[0], refs[1]
        refs = refs[2:]
    else:
        x_ref = refs[0]
        refs = refs[1:]
    (nmix_ref, win_ref, gqa_ref, gka_ref, e384_ref, cqn_ref, wuq_ref, ckvn_ref, wkk_ref,
     wuv_ref, gqb_ref, gkb_ref, e768_ref, cos_ref, sin_ref) = refs[:15]
    outs = refs[15:]
    if has_moe:
        qa_o, ka_o, va_o, qb_o, kb_o, vbt_o, u_o, bc_o, xs_o = outs
    else:
        qa_o, ka_o, va_o, qb_o, kb_o, vbt_o, u_o, bc_o = outs

    x = x_ref[0]
    if has_moe:
        x = x + _moe_tile_to_rows(moe_ref, TOK_TILE // MOE_CHUNK)
        xs_o[0] = x
    xb = _rms(x, nmix_ref[...]).astype(BF16)

    pa = _dot(xb, win_ref[:, 0:3 * A_WIDTH])
    qa = pa[:, 0:A_WIDTH]
    ka = pa[:, A_WIDTH:2 * A_WIDTH]
    qa_o[0] = (qa * lax.rsqrt(_group_sumsq(qa, e384_ref) * (1.0 / HEAD_DIM) + EPS) * gqa_ref[...]).astype(BF16)
    ka_o[0] = (ka * lax.rsqrt(_group_sumsq(ka, e384_ref) * (1.0 / HEAD_DIM) + EPS) * gka_ref[...]).astype(BF16)
    va_o[0] = pa[:, 2 * A_WIDTH:3 * A_WIDTH].astype(BF16)

    c0 = 3 * A_WIDTH
    pb = _dot(xb, win_ref[:, c0:c0 + 512])
    cq = _rms(pb[:, 0:MLA_Q_RANK], cqn_ref[...]).astype(BF16)
    ckv = _rms(pb[:, MLA_Q_RANK:MLA_Q_RANK + MLA_KV_RANK], ckvn_ref[...]).astype(BF16)
    kr = pb[:, MLA_Q_RANK + MLA_KV_RANK:512].astype(BF16)
    qb_raw = _dot(cq, wuq_ref[...])
    kb_raw = _dot(jnp.concatenate([ckv, kr], axis=1), wkk_ref[...])
    vb = _dot(ckv, wuv_ref[...])

    cos = jnp.concatenate([cos_ref[...]] * MLA_HEADS, axis=1)
    sin = jnp.concatenate([sin_ref[...]] * MLA_HEADS, axis=1)
    lane = lax.broadcasted_iota(jnp.int32, (TOK_TILE, QB_WIDTH), 1) & (HEAD_PAD - 1)
    first_half = (lane >= MLA_NOPE) & (lane < MLA_NOPE + MLA_ROPE // 2)

    def norm_rope(y, g_ref):
        y = y * lax.rsqrt(_group_sumsq(y, e768_ref) * (1.0 / MLA_QK) + EPS) * g_ref[...]
        rot = jnp.where(first_half,
                        pltpu.roll(y, QB_WIDTH - MLA_ROPE // 2, 1),
                        pltpu.roll(y, MLA_ROPE // 2, 1))
        return y * cos + rot * sin

    qb_o[0] = norm_rope(qb_raw, gqb_ref).T.astype(BF16)
    kb_o[0] = norm_rope(kb_raw, gkb_ref).astype(BF16)
    vbt = vb.T
    for h in range(MLA_HEADS):
        vbt_o[0, h, 0, 0:MLA_V, :] = vbt[h * MLA_V:(h + 1) * MLA_V, :].astype(BF16)
        vbt_o[0, h, 0, MLA_V:MLA_VROWS, :] = jnp.ones((MLA_VROWS - MLA_V, TOK_TILE), BF16)

    c1 = c0 + 512
    pc = _dot(xb, win_ref[:, c1:c1 + 3 * CONV_CH])
    u_o[0] = pc[:, 2 * CONV_CH:3 * CONV_CH] * pc[:, 0:CONV_CH]
    bc_o[0] = pc[:, CONV_CH:2 * CONV_CH]


def _proj_call(x, moe, lw, tabs):
    b, s, d = x.shape
    nt = s // TOK_TILE
    has_moe = moe is not None

    def const(shape):
        return pl.BlockSpec(shape, lambda bi, ti: (0,) * len(shape))

    in_specs = [pl.BlockSpec((1, TOK_TILE, d), lambda bi, ti: (bi, ti, 0))]
    args = [x]
    if has_moe:
        ncq = D_MODEL // MOE_COLS
        in_specs.append(pl.BlockSpec((1, ncq, TOK_TILE // MOE_CHUNK, MOE_COLS, MOE_CHUNK),
                                     lambda bi, ti: (bi, 0, ti, 0, 0)))
        args.append(moe)
    consts = [lw['nmix'], lw['w_in'], lw['gqa'], lw['gka'], tabs['e384'], lw['cqn'], lw['w_uq'],
              lw['ckvn'], lw['w_kk'], lw['w_uv'], lw['gqb'], lw['gkb'], tabs['e768']]
    for c in consts:
        in_specs.append(const(c.shape))
        args.append(c)
    in_specs += [pl.BlockSpec((TOK_TILE, HEAD_PAD), lambda bi, ti: (ti, 0))] * 2
    args += [tabs['cos'], tabs['sin']]

    def tok(width, dtype):
        return (jax.ShapeDtypeStruct((b, s, width), dtype),
                pl.BlockSpec((1, TOK_TILE, width), lambda bi, ti: (bi, ti, 0)))

    outs = [tok(A_WIDTH, BF16), tok(A_WIDTH, BF16), tok(A_WIDTH, BF16),
            (jax.ShapeDtypeStruct((b, QB_WIDTH, s), BF16),
             pl.BlockSpec((1, QB_WIDTH, TOK_TILE), lambda bi, ti: (bi, 0, ti))),
            tok(QB_WIDTH, BF16),
            (jax.ShapeDtypeStruct((b, MLA_HEADS, nt, MLA_VROWS, TOK_TILE), BF16),
             pl.BlockSpec((1, MLA_HEADS, 1, MLA_VROWS, TOK_TILE), lambda bi, ti: (bi, 0, ti, 0, 0))),
            tok(CONV_CH, F32), tok(CONV_CH, F32)]
    if has_moe:
        outs.append(tok(d, F32))
    return pl.pallas_call(
        functools.partial(_proj_kernel, has_moe=has_moe),
        grid=(b, nt),
        in_specs=in_specs,
        out_specs=[o[1] for o in outs],
        out_shape=[o[0] for o in outs],
        compiler_params=pltpu.CompilerParams(
            dimension_semantics=("arbitrary", "arbitrary"), vmem_limit_bytes=VMEM_LIMIT),
        name="proj_moe" if has_moe else "proj",
    )(*args)


def _bias_kernel(rpb_ref, o_ref):
    h = pl.program_id(0)
    delta = pl.program_id(1)
    nw = 2 * NA_WIN_W - 1
    shape = (GRID_W, NA_WIN_H * GRID_W)
    c = lax.broadcasted_iota(jnp.int32, shape, 0)
    col = lax.broadcasted_iota(jnp.int32, shape, 1)
    wrow = lax.shift_right_logical(col, 6)
    kc = col & (GRID_W - 1)
    c_lo = jnp.clip(c - NA_WIN_W // 2, 0, GRID_W - NA_WIN_W)
    inside = (kc >= c_lo) & (kc < c_lo + NA_WIN_W)
    d = kc - c + (NA_WIN_W - 1)
    acc = jnp.zeros(shape, F32)
    base = h * ((2 * NA_WIN_H - 1) * nw)
    for i in range(NA_WIN_H):
        ri = i - delta + (NA_WIN_H - 1)
        for dd in range(nw):
            val = rpb_ref[base + ri * nw + dd]
            acc = jnp.where((wrow == i) & (d == dd), val, acc)
    o_ref[0, 0] = jnp.where(inside, acc, NEG_BIG)


def _bias_call(rpb_l):
    flat = rpb_l.reshape(-1)
    return pl.pallas_call(
        _bias_kernel,
        grid=(NA_HEADS, NA_WIN_H),
        in_specs=[pl.BlockSpec(memory_space=pltpu.SMEM)],
        out_specs=pl.BlockSpec((1, 1, GRID_W, NA_WIN_H * GRID_W), lambda h, dl: (h, dl, 0, 0)),
        out_shape=jax.ShapeDtypeStruct((NA_HEADS, NA_WIN_H, GRID_W, NA_WIN_H * GRID_W), F32),
        name="na_bias",
    )(flat)


def _na_kernel(q_ref, kp_ref, kc_ref, kn_ref, vp_ref, vc_ref, vn_ref, bias_ref, o_ref,
               k_scr, v_scr, *, rows):
    j = pl.program_id(1)
    blk = NA_ROWS * GRID_W
    k_scr[0:blk] = kp_ref[0]
    k_scr[blk:2 * blk] = kc_ref[0]
    k_scr[2 * blk:3 * blk] = kn_ref[0]
    v_scr[0:blk] = vp_ref[0]
    v_scr[blk:2 * blk] = vc_ref[0]
    v_scr[2 * blk:3 * blk] = vn_ref[0]
    lane = lax.broadcasted_iota(jnp.int32, (GRID_W, LANES), 1)
    low = lane < HEAD_DIM
    win = NA_WIN_H * GRID_W

    def window_start(i):
        r = NA_ROWS * j + i
        r0 = jnp.clip(r - NA_WIN_H // 2, 0, rows - NA_WIN_H)
        return r - r0, pl.multiple_of((r0 - NA_ROWS * j + NA_ROWS) * GRID_W, GRID_W)

    tiles = []
    for i in range(NA_ROWS):
        delta, start = window_start(i)
        for p in range(NA_HEADS // 2):
            lsl = slice(p * LANES, (p + 1) * LANES)
            qp = q_ref[0, i * GRID_W:(i + 1) * GRID_W, lsl]
            kw = k_scr[pl.ds(start, win), lsl]
            for a in range(2):
                qm = jnp.where(low if a == 0 else jnp.logical_not(low), qp, jnp.zeros_like(qp))
                tiles.append(_dot_nt(qm, kw) + bias_ref[2 * p + a, delta])
    sc = jnp.concatenate(tiles, axis=0)
    e = jnp.exp(sc - jnp.max(sc, axis=-1, keepdims=True))
    inv = 1.0 / jnp.sum(e, axis=-1, keepdims=True)
    eb = e.astype(BF16)
    t = 0
    for i in range(NA_ROWS):
        _, start = window_start(i)
        for p in range(NA_HEADS // 2):
            lsl = slice(p * LANES, (p + 1) * LANES)
            vw = v_scr[pl.ds(start, win), lsl]
            res = []
            for a in range(2):
                rsl = slice(t * GRID_W, (t + 1) * GRID_W)
                res.append(_dot(eb[rsl], vw) * inv[rsl])
                t += 1
            o_ref[0, i * GRID_W:(i + 1) * GRID_W, lsl] = jnp.where(low, res[0], res[1])


def _na_call(qa, ka, va, bias_tab):
    b, s, w = qa.shape
    rows = s // GRID_W
    nb = rows // NA_ROWS
    blk = NA_ROWS * GRID_W
    cur = pl.BlockSpec((1, blk, w), lambda bi, j: (bi, j, 0))
    prev = pl.BlockSpec((1, blk, w), lambda bi, j: (bi, jnp.maximum(j - 1, 0), 0))
    nxt = pl.BlockSpec((1, blk, w), lambda bi, j: (bi, jnp.minimum(j + 1, nb - 1), 0))
    return pl.pallas_call(
        functools.partial(_na_kernel, rows=rows),
        grid=(b, nb),
        in_specs=[cur, prev, cur, nxt, prev, cur, nxt,
                  pl.BlockSpec(bias_tab.shape, lambda bi, j: (0, 0, 0, 0))],
        out_specs=pl.BlockSpec((1, blk, w), lambda bi, j: (bi, j, 0)),
        out_shape=jax.ShapeDtypeStruct((b, s, w), F32),
        scratch_shapes=[pltpu.VMEM((3 * blk, w), BF16), pltpu.VMEM((3 * blk, w), BF16)],
        compiler_params=pltpu.CompilerParams(
            dimension_semantics=("arbitrary", "arbitrary"), vmem_limit_bytes=VMEM_LIMIT),
        name="na_attn",
    )(qa, ka, ka, ka, va, va, va, bias_tab)


def _mla_kernel(q_ref, k_ref, vt_ref, o_ref, s_scr, acc_scr, m_scr, *, n_chunks):
    qt = q_ref[0]

    def scores(c):
        kb = k_ref[0, pl.ds(pl.multiple_of(c * TOK_TILE, TOK_TILE), TOK_TILE), :]
        return _dot(kb, qt)

    def update(st, c):
        m = m_scr[...]
        m_new = jnp.maximum(m, jnp.max(st, axis=0, keepdims=True))
        alpha = jnp.exp2(m - m_new)
        p = jnp.exp2(st - m_new).astype(BF16)
        acc_scr[...] = alpha * acc_scr[...] + _dot(vt_ref[0, 0, c], p)
        m_scr[...] = m_new

    m_scr[...] = jnp.full(m_scr.shape, NEG_BIG, F32)
    acc_scr[...] = jnp.zeros(acc_scr.shape, F32)
    s_scr[0] = scores(0)

    def body(i, carry):
        for j in range(MLA_UNROLL):
            c = MLA_UNROLL * i + j
            s_scr[(j + 1) % 2] = scores(jnp.minimum(c + 1, n_chunks - 1))
            update(s_scr[j % 2], c)
        return carry

    lax.fori_loop(0, n_chunks // MLA_UNROLL, body, 0)
    acc = acc_scr[...]
    o_ref[0] = acc[0:MLA_V, :] / acc[MLA_V:MLA_V + 1, :]


def _mla_call(qb, kb, vbt):
    b, s, _ = kb.shape
    nt = s // TOK_TILE
    assert nt % MLA_UNROLL == 0 and MLA_UNROLL % 2 == 0
    return pl.pallas_call(
        functools.partial(_mla_kernel, n_chunks=nt),
        grid=(b, MLA_HEADS, nt),
        in_specs=[pl.BlockSpec((1, HEAD_PAD, TOK_TILE), lambda bi, h, qi: (bi, h, qi)),
                  pl.BlockSpec((1, s, HEAD_PAD), lambda bi, h, qi: (bi, 0, h)),
                  pl.BlockSpec((1, 1, nt, MLA_VROWS, TOK_TILE), lambda bi, h, qi: (bi, h, 0, 0, 0))],
        out_specs=pl.BlockSpec((1, MLA_V, TOK_TILE), lambda bi, h, qi: (bi, h, qi)),
        out_shape=jax.ShapeDtypeStruct((b, B_WIDTH, s), F32),
        scratch_shapes=[pltpu.VMEM((2, TOK_TILE, TOK_TILE), F32), pltpu.VMEM((MLA_VROWS, TOK_TILE), F32),
                        pltpu.VMEM((1, TOK_TILE), F32)],
        compiler_params=pltpu.CompilerParams(
            dimension_semantics=("arbitrary", "arbitrary", "arbitrary"), vmem_limit_bytes=VMEM_LIMIT),
        name="mla_attn",
    )(qb, kb, vbt)


def _out_kernel(x_ref, oa_ref, obt_ref, u_ref, up_ref, un_ref, bc_ref, ga_ref, gb_ref, gc_ref,
                woa_ref, wob_ref, woc_ref, cw_ref, nffn_ref, wrh_ref, wrl_ref, br_ref,
                x1_o, xn_o, aff_o):
    ti = pl.program_id(1)
    nt = pl.num_programs(1)
    t = TOK_TILE

    na = _rms(oa_ref[0], ga_ref[...]).astype(BF16)
    acc = _dot(na, woa_ref[...])

    obt = obt_ref[0]
    nbt = obt * lax.rsqrt(jnp.mean(obt * obt, axis=0, keepdims=True) + EPS) * gb_ref[...]
    acc = acc + _dot(nbt.T.astype(BF16), wob_ref[...])

    u = u_ref[0]
    row = lax.broadcasted_iota(jnp.int32, (t, CONV_CH), 0)
    prev_row = jnp.where(ti > 0, up_ref[0, 7:8, :], 0.0)
    next_row = jnp.where(ti < nt - 1, un_ref[0, 0:1, :], 0.0)
    u_m1 = jnp.where(row == 0, prev_row, pltpu.roll(u, 1, 0))
    u_p1 = jnp.where(row == t - 1, next_row, pltpu.roll(u, t - 1, 0))
    y = cw_ref[0:1, :] * u_m1 + cw_ref[1:2, :] * u + cw_ref[2:3, :] * u_p1
    oc = _rms(bc_ref[0] * y, gc_ref[...]).astype(BF16)
    acc = acc + _dot(oc, woc_ref[...])

    x1 = x_ref[0] + acc
    x1_o[0] = x1
    xn = _rms(x1, nffn_ref[...])
    hi = xn.astype(BF16)
    lo = (xn - hi.astype(F32)).astype(BF16)
    xn_o[0] = hi
    logits = _dot(hi, wrh_ref[...]) + _dot(lo, wrh_ref[...]) + _dot(hi, wrl_ref[...]) + br_ref[...]
    m = jnp.max(logits, axis=-1, keepdims=True)
    e = jnp.exp(logits - m)
    aff = e / jnp.sum(e, axis=-1, keepdims=True)
    aff_o[0] = aff.T[0:N_EXPERTS, :]


def _out_call(x, out_a, out_bt, u, bc, lw):
    b, s, d = x.shape
    nt = s // TOK_TILE
    h8 = TOK_TILE // 8

    def const(a):
        return pl.BlockSpec(a.shape, lambda bi, ti: (0,) * a.ndim)

    def tok(width):
        return pl.BlockSpec((1, TOK_TILE, width), lambda bi, ti: (bi, ti, 0))

    consts = [lw['ga'], lw['gb_col'], lw['gc'], lw['wo_a'], lw['wo_b'], lw['wo_c'], lw['conv_w'],
              lw['nffn'], lw['wr_hi'], lw['wr_lo'], lw['b_r']]
    in_specs = [tok(d), tok(A_WIDTH),
                pl.BlockSpec((1, B_WIDTH, TOK_TILE), lambda bi, ti: (bi, 0, ti)),
                tok(CONV_CH),
                pl.BlockSpec((1, 8, CONV_CH), lambda bi, ti: (bi, jnp.maximum(ti * h8 - 1, 0), 0)),
                pl.BlockSpec((1, 8, CONV_CH), lambda bi, ti: (bi, jnp.minimum((ti + 1) * h8, s // 8 - 1), 0)),
                tok(CONV_CH)] + [const(c) for c in consts]
    return pl.pallas_call(
        _out_kernel,
        grid=(b, nt),
        in_specs=in_specs,
        out_specs=[tok(d), tok(d), pl.BlockSpec((1, N_EXPERTS, TOK_TILE), lambda bi, ti: (bi, 0, ti))],
        out_shape=[jax.ShapeDtypeStruct((b, s, d), F32), jax.ShapeDtypeStruct((b, s, d), BF16),
                   jax.ShapeDtypeStruct((b, N_EXPERTS, s), F32)],
        compiler_params=pltpu.CompilerParams(
            dimension_semantics=("arbitrary", "arbitrary"), vmem_limit_bytes=VMEM_LIMIT),
        name="out_proj",
    )(x, out_a, out_bt, u, u, u, bc, *consts)


def _topk_kernel(a_ref, tri_ref, o_slot, o_gate, o_off, *, cap):
    def total(mask):
        v = jnp.where(mask, 1.0, 0.0)
        return jnp.sum(jnp.sum(v, axis=0, keepdims=True), axis=1, keepdims=True)

    group = range(TOPK_GROUP)
    all_bits = [pltpu.bitcast(a_ref[0, t], jnp.int32) for t in group]
    thrs = [jnp.zeros((1, 1), jnp.int32) for _ in group]
    for bit in range(30, -1, -1):
        cands = [thrs[t] | (1 << bit) for t in group]
        thrs = [jnp.where(total(all_bits[t] >= cands[t]) >= cap, cands[t], thrs[t]) for t in group]
    for t in group:
        _topk_finish(a_ref[0, t], all_bits[t], thrs[t], total, tri_ref, o_slot, o_gate, o_off, t, cap)


def _topk_finish(a, bits, thr, total, tri_ref, o_slot, o_gate, o_off, t, cap):
    gt = bits > thr
    eq = bits == thr
    need = cap - total(gt)

    upper = tri_ref[0]
    lower_strict = tri_ref[1]
    upper_strict = tri_ref[2]

    def excl_rank(mask):
        f = jnp.where(mask, 1.0, 0.0)
        fb = f.astype(BF16)
        in_row = _dot(fb, upper)
        before_rows = jnp.sum(_dot(lower_strict, fb), axis=1, keepdims=True)
        return before_rows + in_row - f, fb

    eq_rank, _ = excl_rank(eq)
    sel = gt | (eq & (eq_rank < need))
    pos, sel_b = excl_rank(sel)
    o_slot[0, t] = jnp.where(sel, pos, -1.0).astype(jnp.int32)
    o_gate[0, t] = jnp.where(sel, a, 0.0)
    row_tot = _dot_nt(jnp.ones((8, LANES), BF16), sel_b)
    o_off[0, t] = _dot(row_tot.astype(BF16), upper_strict)


def _topk_call(aff_t, tri, cap):
    b, e, s = aff_t.shape
    pad = TOPK_ROWS * LANES - s
    a = aff_t if pad == 0 else jnp.pad(aff_t, ((0, 0), (0, 0), (0, pad)), constant_values=-1.0)
    a = a.reshape(b, e, TOPK_ROWS, LANES)
    blk = pl.BlockSpec((1, TOPK_GROUP, TOPK_ROWS, LANES), lambda bi, ei: (bi, ei, 0, 0))
    return pl.pallas_call(
        functools.partial(_topk_kernel, cap=cap),
        grid=(b, e // TOPK_GROUP),
        in_specs=[blk, pl.BlockSpec(tri.shape, lambda bi, ei: (0, 0, 0))],
        out_specs=[blk, blk, pl.BlockSpec((1, TOPK_GROUP, 8, LANES), lambda bi, ei: (bi, ei, 0, 0))],
        out_shape=[jax.ShapeDtypeStruct((b, e, TOPK_ROWS, LANES), jnp.int32),
                   jax.ShapeDtypeStruct((b, e, TOPK_ROWS, LANES), F32),
                   jax.ShapeDtypeStruct((b, e, 8, LANES), F32)],
        name="topk_select",
    )(a, tri)


def _window_onehot(slot_row, start):
    i = lax.broadcasted_iota(jnp.int32, (2 * MOE_CHUNK, MOE_CHUNK), 0)
    return jnp.where(i == slot_row - start, 1.0, 0.0).astype(BF16)


def _window_tile(off, tiles):
    return jnp.minimum(lax.shift_right_logical(off, 8), tiles - 1)


def _gather_kernel(coff_ref, x_ref, slot_ref, o_ref, acc_ref, *, n_chunks, cap):
    bi = pl.program_id(0)
    ei = pl.program_id(2)
    base = (bi * N_EXPERTS + ei) * (n_chunks + 1)
    acc_ref[...] = jnp.zeros_like(acc_ref)

    row = lax.broadcasted_iota(jnp.int32, (GATHER_WIN, MOE_CHUNK), 0)

    def body(k, carry):
        off = coff_ref[base + k]
        start = pl.multiple_of(jnp.minimum(lax.shift_left(lax.shift_right_logical(off, 3), 3), cap), 8)
        g = jnp.where(row == slot_ref[0, 0, k] - start, 1.0, 0.0).astype(BF16)
        xk = x_ref[0, pl.ds(pl.multiple_of(k * MOE_CHUNK, MOE_CHUNK), MOE_CHUNK), :]
        acc_ref[pl.ds(start, GATHER_WIN), :] += _dot(g, xk)
        return carry

    lax.fori_loop(0, n_chunks, body, 0, unroll=8)
    o_ref[0, 0] = acc_ref[0:cap, :].astype(BF16)


def _gather_call(coff, xn, slot5, cap):
    b, s, d = xn.shape
    n_chunks = s // MOE_CHUNK
    ncq = d // GATHER_COLS
    grid_spec = pltpu.PrefetchScalarGridSpec(
        num_scalar_prefetch=1,
        grid=(b, ncq, N_EXPERTS),
        in_specs=[pl.BlockSpec((1, s, GATHER_COLS), lambda bi, cq, ei, co: (bi, 0, cq)),
                  pl.BlockSpec((1, 1, n_chunks, 1, MOE_CHUNK), lambda bi, cq, ei, co: (bi, ei, 0, 0, 0))],
        out_specs=pl.BlockSpec((1, 1, cap, GATHER_COLS), lambda bi, cq, ei, co: (bi, ei, 0, cq)),
        scratch_shapes=[pltpu.VMEM((cap + GATHER_WIN, GATHER_COLS), F32)],
    )
    return pl.pallas_call(
        functools.partial(_gather_kernel, n_chunks=n_chunks, cap=cap),
        grid_spec=grid_spec,
        out_shape=jax.ShapeDtypeStruct((b, N_EXPERTS, cap, d), BF16),
        compiler_params=pltpu.CompilerParams(
            dimension_semantics=("arbitrary", "arbitrary", "arbitrary"), vmem_limit_bytes=VMEM_LIMIT),
        name="moe_gather",
    )(coff, xn, slot5)


def _ffn_kernel(xc_ref, wg_ref, wu_ref, wdt_ref, o_ref, *, tiles):
    xc = xc_ref[0, 0]
    g = _dot(xc, wg_ref[0])
    u = _dot(xc, wu_ref[0])
    h = (g / (1.0 + jnp.exp(-g)) * u).astype(BF16)
    yt = _dot_nt(wdt_ref[0], h)
    for r in range(tiles):
        o_ref[0, 0, r] = yt[:, r * MOE_CHUNK:(r + 1) * MOE_CHUNK].astype(BF16)


def _ffn_call(xc, wg, wu, wdt):
    b, e, cap, d = xc.shape
    ff = wg.shape[-1]
    rt = min(cap, 1024)
    tiles = rt // MOE_CHUNK
    return pl.pallas_call(
        functools.partial(_ffn_kernel, tiles=tiles),
        grid=(e, b, cap // rt),
        in_specs=[pl.BlockSpec((1, 1, rt, d), lambda ei, bi, ri: (bi, ei, ri, 0)),
                  pl.BlockSpec((1, d, ff), lambda ei, bi, ri: (ei, 0, 0)),
                  pl.BlockSpec((1, d, ff), lambda ei, bi, ri: (ei, 0, 0)),
                  pl.BlockSpec((1, d, ff), lambda ei, bi, ri: (ei, 0, 0))],
        out_specs=pl.BlockSpec((1, 1, tiles, d, MOE_CHUNK), lambda ei, bi, ri: (bi, ei, ri, 0, 0)),
        out_shape=jax.ShapeDtypeStruct((b, e, cap // MOE_CHUNK, d, MOE_CHUNK), BF16),
        compiler_params=pltpu.CompilerParams(
            dimension_semantics=("arbitrary", "arbitrary", "arbitrary"), vmem_limit_bytes=VMEM_LIMIT),
        name="moe_ffn",
    )(xc, wg, wu, wdt)


def _scatter_kernel(coff_ref, y_ref, slot_ref, gate_ref, o_ref, *, n_chunks, tiles):
    bi = pl.program_id(0)
    kg = pl.program_id(2)
    for j in range(SCATTER_CHUNKS):
        k = kg * SCATTER_CHUNKS + j
        acc = jnp.zeros((MOE_COLS, MOE_CHUNK), F32)
        for e in range(N_EXPERTS):
            r0 = _window_tile(coff_ref[(bi * N_EXPERTS + e) * (n_chunks + 1) + k], tiles)
            r1 = jnp.minimum(r0 + 1, tiles - 1)
            g = _window_onehot(slot_ref[0, e, j], r0 * MOE_CHUNK)
            yw = jnp.concatenate([y_ref[0, e, r0], y_ref[0, e, r1]], axis=1)
            acc = acc + _dot(yw, g) * gate_ref[0, e, j]
        o_ref[0, 0, j] = acc


def _scatter_call(coff, y5, slot5, gate5):
    b, e, tiles, d, _ = y5.shape
    n_chunks = slot5.shape[2]
    ncq = d // MOE_COLS
    small = pl.BlockSpec((1, e, SCATTER_CHUNKS, 1, MOE_CHUNK), lambda bi, cq, kg, co: (bi, 0, kg, 0, 0))
    grid_spec = pltpu.PrefetchScalarGridSpec(
        num_scalar_prefetch=1,
        grid=(b, ncq, n_chunks // SCATTER_CHUNKS),
        in_specs=[pl.BlockSpec((1, e, tiles, MOE_COLS, MOE_CHUNK), lambda bi, cq, kg, co: (bi, 0, 0, cq, 0)),
                  small, small],
        out_specs=pl.BlockSpec((1, 1, SCATTER_CHUNKS, MOE_COLS, MOE_CHUNK),
                               lambda bi, cq, kg, co: (bi, cq, kg, 0, 0)),
    )
    return pl.pallas_call(
        functools.partial(_scatter_kernel, n_chunks=n_chunks, tiles=tiles),
        grid_spec=grid_spec,
        out_shape=jax.ShapeDtypeStruct((b, ncq, n_chunks, MOE_COLS, MOE_CHUNK), F32),
        compiler_params=pltpu.CompilerParams(
            dimension_semantics=("arbitrary", "arbitrary", "arbitrary"), vmem_limit_bytes=VMEM_LIMIT),
        name="moe_scatter",
    )(coff, y5, slot5, gate5)


def _final_kernel(x_ref, moe_ref, o_ref):
    o_ref[0] = x_ref[0] + _moe_tile_to_rows(moe_ref, TOK_TILE // MOE_CHUNK)


def _final_call(x1, moe):
    b, s, d = x1.shape
    ncq = d // MOE_COLS
    tok = pl.BlockSpec((1, TOK_TILE, d), lambda bi, ti: (bi, ti, 0))
    return pl.pallas_call(
        _final_kernel,
        grid=(b, s // TOK_TILE),
        in_specs=[tok, pl.BlockSpec((1, ncq, TOK_TILE // MOE_CHUNK, MOE_COLS, MOE_CHUNK),
                                    lambda bi, ti: (bi, 0, ti, 0, 0))],
        out_specs=tok,
        out_shape=jax.ShapeDtypeStruct((b, s, d), F32),
        compiler_params=pltpu.CompilerParams(
            dimension_semantics=("arbitrary", "arbitrary"), vmem_limit_bytes=VMEM_LIMIT),
        name="final_add",
    )(x1, moe)


def _block_ones(n, group):
    i = np.arange(n)
    return jnp.asarray((i[:, None] // group) == (i[None, :] // group), BF16)


def _tables(s):
    half = MLA_ROPE // 2
    inv = ROPE_THETA ** (-jnp.arange(half, dtype=F32) / half)
    ang = jnp.arange(s, dtype=jnp.int32).astype(F32)[:, None] * inv[None, :]
    cos, sin = jnp.cos(ang), jnp.sin(ang)
    ones = jnp.ones((s, MLA_NOPE), F32)
    tail = HEAD_PAD - MLA_QK
    cos_t = jnp.concatenate([ones, cos, cos, jnp.ones((s, tail), F32)], axis=1)
    sin_t = jnp.concatenate([0 * ones, -sin, sin, jnp.zeros((s, tail), F32)], axis=1)
    i = np.arange(LANES)
    tri = np.stack([i[:, None] <= i[None, :], i[None, :] < i[:, None], i[:, None] < i[None, :]])
    return {'cos': cos_t, 'sin': sin_t, 'e384': _block_ones(2 * LANES, HEAD_DIM),
            'e768': _block_ones(2 * LANES, HEAD_PAD), 'tri': jnp.asarray(tri, BF16)}


def _pad_heads(v, scale=1.0):
    g = jnp.pad(v.astype(F32) * scale, (0, HEAD_PAD - MLA_QK))
    return jnp.tile(g, MLA_HEADS)[None, :]


def _layer_weights(l, p):
    w_in = p['w_in'][l]
    kr_end = 3 * A_WIDTH + MLA_Q_RANK + MLA_KV_RANK + MLA_ROPE
    w_in_r = jnp.concatenate(
        [w_in[:, :kr_end], jnp.zeros((D_MODEL, LANES - MLA_ROPE), F32), w_in[:, kr_end:]], axis=1)
    w_uq = p['w_uq'][l].reshape(MLA_Q_RANK, MLA_HEADS, MLA_QK)
    w_uq = jnp.pad(w_uq, ((0, 0), (0, 0), (0, HEAD_PAD - MLA_QK))).reshape(MLA_Q_RANK, QB_WIDTH)
    w_ukv = p['w_ukv'][l].reshape(MLA_KV_RANK, MLA_HEADS, MLA_NOPE + MLA_V)
    w_uk = jnp.pad(w_ukv[:, :, :MLA_NOPE], ((0, 0), (0, 0), (0, HEAD_PAD - MLA_NOPE)))
    w_uk = w_uk.reshape(MLA_KV_RANK, QB_WIDTH)
    w_uv = w_ukv[:, :, MLA_NOPE:].reshape(MLA_KV_RANK, B_WIDTH)
    place = np.zeros((LANES, QB_WIDTH), np.float32)
    for h in range(MLA_HEADS):
        place[np.arange(MLA_ROPE), h * HEAD_PAD + MLA_NOPE + np.arange(MLA_ROPE)] = 1.0
    w_kk = jnp.concatenate([w_uk, jnp.asarray(place)], axis=0)
    out_norm = p['out_norm'][l]
    w_out = p['w_out'][l]
    w_r = jnp.pad(p['w_router'][l], ((0, 0), (0, LANES - N_EXPERTS)))
    wr_hi = w_r.astype(BF16)
    b_r = jnp.concatenate([p['b_router'][l].astype(F32), jnp.full((LANES - N_EXPERTS,), NEG_BIG, F32)])
    return {
        'nmix': p['norm_mix'][l][None, :], 'w_in': w_in_r.astype(BF16),
        'gqa': jnp.tile(p['q_norm_a'][l] * HEAD_DIM ** -0.5, NA_HEADS)[None, :],
        'gka': jnp.tile(p['k_norm_a'][l], NA_HEADS)[None, :],
        'cqn': p['cq_norm'][l][None, :], 'w_uq': w_uq.astype(BF16),
        'ckvn': p['ckv_norm'][l][None, :], 'w_kk': w_kk.astype(BF16), 'w_uv': w_uv.astype(BF16),
        'gqb': _pad_heads(p['q_norm_b'][l], MLA_QK ** -0.5 * LOG2_E), 'gkb': _pad_heads(p['k_norm_b'][l]),
        'ga': out_norm[None, :A_WIDTH], 'gb_col': out_norm[A_WIDTH:A_WIDTH + B_WIDTH, None],
        'gc': out_norm[None, A_WIDTH + B_WIDTH:],
        'wo_a': w_out[:A_WIDTH].astype(BF16), 'wo_b': w_out[A_WIDTH:A_WIDTH + B_WIDTH].astype(BF16),
        'wo_c': w_out[A_WIDTH + B_WIDTH:].astype(BF16),
        'conv_w': p['conv_w'][l], 'nffn': p['norm_ffn'][l][None, :],
        'wr_hi': wr_hi, 'wr_lo': (w_r - wr_hi.astype(F32)).astype(BF16), 'b_r': b_r[None, :],
        'w_gate': p['w_gate'][l].astype(BF16), 'w_up': p['w_up'][l].astype(BF16),
        'w_down_t': jnp.swapaxes(p['w_down'][l], 1, 2).astype(BF16),
    }


def _layer(x, moe, lw, tabs, rpb_l):
    b, s, d = x.shape
    cap = CAPACITY_FACTOR * s // N_EXPERTS
    n_chunks = s // MOE_CHUNK

    outs = _proj_call(x, moe, lw, tabs)
    if moe is not None:
        qa, ka, va, qb, kb, vbt, u, bc, x = outs
    else:
        qa, ka, va, qb, kb, vbt, u, bc = outs
    out_a = _na_call(qa, ka, va, _bias_call(rpb_l))
    out_bt = _mla_call(qb, kb, vbt)
    x1, xn, aff_t = _out_call(x, out_a, out_bt, u, bc, lw)

    slot, gate, off = _topk_call(aff_t, tabs['tri'], cap)
    rows_per_chunk = MOE_CHUNK // LANES
    coff = off[:, :, 0, :s // LANES:rows_per_chunk].astype(jnp.int32)
    coff = jnp.concatenate([coff, jnp.full((b, N_EXPERTS, 1), cap, jnp.int32)], axis=-1).reshape(-1)
    slot5 = slot.reshape(b, N_EXPERTS, -1)[:, :, :s].reshape(b, N_EXPERTS, n_chunks, 1, MOE_CHUNK)
    gate5 = gate.reshape(b, N_EXPERTS, -1)[:, :, :s].reshape(b, N_EXPERTS, n_chunks, 1, MOE_CHUNK)

    xc = _gather_call(coff, xn, slot5, cap)
    y5 = _ffn_call(xc, lw['w_gate'], lw['w_up'], lw['w_down_t'])
    moe_out = _scatter_call(coff, y5, slot5, gate5)
    return x1, moe_out


def kernel(x, norm_mix, w_in, q_norm_a, k_norm_a, rpb, cq_norm, w_uq, ckv_norm, w_ukv, q_norm_b,
           k_norm_b, conv_w, out_norm, w_out, norm_ffn, w_router, b_router, w_gate, w_up, w_down):
    p = dict(norm_mix=norm_mix, w_in=w_in, q_norm_a=q_norm_a, k_norm_a=k_norm_a, cq_norm=cq_norm,
             w_uq=w_uq, ckv_norm=ckv_norm, w_ukv=w_ukv, q_norm_b=q_norm_b, k_norm_b=k_norm_b,
             conv_w=conv_w, out_norm=out_norm, w_out=w_out, norm_ffn=norm_ffn, w_router=w_router,
             b_router=b_router, w_gate=w_gate, w_up=w_up, w_down=w_down)
    b, s, d = x.shape
    assert d == D_MODEL and (CAPACITY_FACTOR * s // N_EXPERTS) % MOE_CHUNK == 0
    assert s % TOK_TILE == 0 and (s // GRID_W) % NA_ROWS == 0 and s <= TOPK_ROWS * LANES
    tabs = _tables(s)
    moe = None
    for l in range(norm_mix.shape[0]):
        x, moe = _layer(x, moe, _layer_weights(l, p), tabs, rpb[l])
    return _final_call(x, moe)
```

```python
import functools

import numpy as np
import jax
import jax.numpy as jnp
from jax import lax
from jax.experimental import pallas as pl
from jax.experimental.pallas import tpu as pltpu

D_MODEL = 1024
GRID_W = 64
HEAD_DIM = 64
EPS = 1e-6

NA_HEADS = 6
NA_WIN_H = 8
NA_WIN_W = 16
A_WIDTH = NA_HEADS * HEAD_DIM

MLA_HEADS = 6
MLA_Q_RANK = 256
MLA_KV_RANK = 128
MLA_NOPE = 64
MLA_ROPE = 32
MLA_QK = MLA_NOPE + MLA_ROPE
MLA_V = 64
B_WIDTH = MLA_HEADS * MLA_V
ROPE_THETA = 10000.0

CONV_CH = 256
C_WIDTH = CONV_CH

N_EXPERTS = 16
EXPERT_FF = 1024
CAPACITY_FACTOR = 2

LANES = 128
HEAD_PAD = 128
QB_WIDTH = MLA_HEADS * HEAD_PAD
IN_COLS_PAD = 2432
TOK_TILE = 512
NA_ROWS = 8
MOE_CHUNK = 256
MOE_COLS = 256
GATHER_COLS = 512
SCATTER_CHUNKS = 4
TOPK_GROUP = 4
GATHER_WIN = MOE_CHUNK + 16
TOPK_ROWS = 128
MLA_VROWS = MLA_V + 16
LOG2_E = 1.4426950408889634
MLA_UNROLL = 16
NEG_BIG = -1e30
VMEM_LIMIT = 56 * 1024 * 1024

F32 = jnp.float32
BF16 = jnp.bfloat16


def _dot(a, b):
    return jnp.dot(a, b, preferred_element_type=F32)


def _dot_nt(a, b):
    return lax.dot_general(a, b, (((1,), (1,)), ((), ())), preferred_element_type=F32)


def _rms(x, g):
    return x * lax.rsqrt(jnp.mean(x * x, axis=-1, keepdims=True) + EPS) * g


def _group_sumsq(y, e_ref):
    sq = (y * y).astype(BF16)
    width = y.shape[1]
    parts = []
    for c in range(0, width, 2 * LANES):
        n = min(2 * LANES, width - c)
        parts.append(_dot(sq[:, c:c + n], e_ref[0:n, 0:n]))
    return jnp.concatenate(parts, axis=1)


def _moe_tile_to_rows(moe_ref, n_chunks):
    rows = []
    for ch in range(n_chunks):
        cols = [moe_ref[0, cq, ch].T for cq in range(D_MODEL // MOE_COLS)]
        rows.append(jnp.concatenate(cols, axis=1))
    return jnp.concatenate(rows, axis=0)


def _proj_kernel(*refs, has_moe):
    if has_moe:
        x_ref, moe_ref = refs[0], refs[1]
        refs = refs[2:]
    else:
        x_ref = refs[0]
        refs = refs[1:]
    (nmix_ref, win_ref, gqa_ref, gka_ref, e384_ref, cqn_ref, wuq_ref, ckvn_ref, wkk_ref,
     wuv_ref, gqb_ref, gkb_ref, e768_ref, cos_ref, sin_ref) = refs[:15]
    outs = refs[15:]
    if has_moe:
        qa_o, ka_o, va_o, qb_o, kb_o, vbt_o, u_o, bc_o, xs_o = outs
    else:
        qa_o, ka_o, va_o, qb_o, kb_o, vbt_o, u_o, bc_o = outs

    x = x_ref[0]
    if has_moe:
        x = x + _moe_tile_to_rows(moe_ref, TOK_TILE // MOE_CHUNK)
        xs_o[0] = x
    xb = _rms(x, nmix_ref[...]).astype(BF16)

    pa = _dot(xb, win_ref[:, 0:3 * A_WIDTH])
    qa = pa[:, 0:A_WIDTH]
    ka = pa[:, A_WIDTH:2 * A_WIDTH]
    qa_o[0] = (qa * lax.rsqrt(_group_sumsq(qa, e384_ref) * (1.0 / HEAD_DIM) + EPS) * gqa_ref[...]).astype(BF16)
    ka_o[0] = (ka * lax.rsqrt(_group_sumsq(ka, e384_ref) * (1.0 / HEAD_DIM) + EPS) * gka_ref[...]).astype(BF16)
    va_o[0] = pa[:, 2 * A_WIDTH:3 * A_WIDTH].astype(BF16)

    c0 = 3 * A_WIDTH
    pb = _dot(xb, win_ref[:, c0:c0 + 512])
    cq = _rms(pb[:, 0:MLA_Q_RANK], cqn_ref[...]).astype(BF16)
    ckv = _rms(pb[:, MLA_Q_RANK:MLA_Q_RANK + MLA_KV_RANK], ckvn_ref[...]).astype(BF16)
    kr = pb[:, MLA_Q_RANK + MLA_KV_RANK:512].astype(BF16)
    qb_raw = _dot(cq, wuq_ref[...])
    kb_raw = _dot(jnp.concatenate([ckv, kr], axis=1), wkk_ref[...])
    vb = _dot(ckv, wuv_ref[...])

    cos = jnp.concatenate([cos_ref[...]] * MLA_HEADS, axis=1)
    sin = jnp.concatenate([sin_ref[...]] * MLA_HEADS, axis=1)
    lane = lax.broadcasted_iota(jnp.int32, (TOK_TILE, QB_WIDTH), 1) & (HEAD_PAD - 1)
    first_half = (lane >= MLA_NOPE) & (lane < MLA_NOPE + MLA_ROPE // 2)

    def norm_rope(y, g_ref):
        y = y * lax.rsqrt(_group_sumsq(y, e768_ref) * (1.0 / MLA_QK) + EPS) * g_ref[...]
        rot = jnp.where(first_half,
                        pltpu.roll(y, QB_WIDTH - MLA_ROPE // 2, 1),
                        pltpu.roll(y, MLA_ROPE // 2, 1))
        return y * cos + rot * sin

    qb_o[0] = norm_rope(qb_raw, gqb_ref).T.astype(BF16)
    kb_o[0] = norm_rope(kb_raw, gkb_ref).astype(BF16)
    vbt = vb.T
    for h in range(MLA_HEADS):
        vbt_o[0, h, 0, 0:MLA_V, :] = vbt[h * MLA_V:(h + 1) * MLA_V, :].astype(BF16)
        vbt_o[0, h, 0, MLA_V:MLA_VROWS, :] = jnp.ones((MLA_VROWS - MLA_V, TOK_TILE), BF16)

    c1 = c0 + 512
    pc = _dot(xb, win_ref[:, c1:c1 + 3 * CONV_CH])
    u_o[0] = pc[:, 2 * CONV_CH:3 * CONV_CH] * pc[:, 0:CONV_CH]
    bc_o[0] = pc[:, CONV_CH:2 * CONV_CH]


def _proj_call(x, moe, lw, tabs):
    b, s, d = x.shape
    nt = s // TOK_TILE
    has_moe = moe is not None

    def const(shape):
        return pl.BlockSpec(shape, lambda bi, ti: (0,) * len(shape))

    in_specs = [pl.BlockSpec((1, TOK_TILE, d), lambda bi, ti: (bi, ti, 0))]
    args = [x]
    if has_moe:
        ncq = D_MODEL // MOE_COLS
        in_specs.append(pl.BlockSpec((1, ncq, TOK_TILE // MOE_CHUNK, MOE_COLS, MOE_CHUNK),
                                     lambda bi, ti: (bi, 0, ti, 0, 0)))
        args.append(moe)
    consts = [lw['nmix'], lw['w_in'], lw['gqa'], lw['gka'], tabs['e384'], lw['cqn'], lw['w_uq'],
              lw['ckvn'], lw['w_kk'], lw['w_uv'], lw['gqb'], lw['gkb'], tabs['e768']]
    for c in consts:
        in_specs.append(const(c.shape))
        args.append(c)
    in_specs += [pl.BlockSpec((TOK_TILE, HEAD_PAD), lambda bi, ti: (ti, 0))] * 2
    args += [tabs['cos'], tabs['sin']]

    def tok(width, dtype):
        return (jax.ShapeDtypeStruct((b, s, width), dtype),
                pl.BlockSpec((1, TOK_TILE, width), lambda bi, ti: (bi, ti, 0)))

    outs = [tok(A_WIDTH, BF16), tok(A_WIDTH, BF16), tok(A_WIDTH, BF16),
            (jax.ShapeDtypeStruct((b, QB_WIDTH, s), BF16),
             pl.BlockSpec((1, QB_WIDTH, TOK_TILE), lambda bi, ti: (bi, 0, ti))),
            tok(QB_WIDTH, BF16),
            (jax.ShapeDtypeStruct((b, MLA_HEADS, nt, MLA_VROWS, TOK_TILE), BF16),
             pl.BlockSpec((1, MLA_HEADS, 1, MLA_VROWS, TOK_TILE), lambda bi, ti: (bi, 0, ti, 0, 0))),
            tok(CONV_CH, F32), tok(CONV_CH, F32)]
    if has_moe:
        outs.append(tok(d, F32))
    return pl.pallas_call(
        functools.partial(_proj_kernel, has_moe=has_moe),
        grid=(b, nt),
        in_specs=in_specs,
        out_specs=[o[1] for o in outs],
        out_shape=[o[0] for o in outs],
        compiler_params=pltpu.CompilerParams(
            dimension_semantics=("arbitrary", "arbitrary"), vmem_limit_bytes=VMEM_LIMIT),
        name="proj_moe" if has_moe else "proj",
    )(*args)


def _bias_kernel(rpb_ref, o_ref):
    h = pl.program_id(0)
    delta = pl.program_id(1)
    nw = 2 * NA_WIN_W - 1
    shape = (GRID_W, NA_WIN_H * GRID_W)
    c = lax.broadcasted_iota(jnp.int32, shape, 0)
    col = lax.broadcasted_iota(jnp.int32, shape, 1)
    wrow = lax.shift_right_logical(col, 6)
    kc = col & (GRID_W - 1)
    c_lo = jnp.clip(c - NA_WIN_W // 2, 0, GRID_W - NA_WIN_W)
    inside = (kc >= c_lo) & (kc < c_lo + NA_WIN_W)
    d = kc - c + (NA_WIN_W - 1)
    acc = jnp.zeros(shape, F32)
    base = h * ((2 * NA_WIN_H - 1) * nw)
    for i in range(NA_WIN_H):
        ri = i - delta + (NA_WIN_H - 1)
        for dd in range(nw):
            val = rpb_ref[base + ri * nw + dd]
            acc = jnp.where((wrow == i) & (d == dd), val, acc)
    o_ref[0, 0] = jnp.where(inside, acc, NEG_BIG)


def _bias_call(rpb_l):
    flat = rpb_l.reshape(-1)
    return pl.pallas_call(
        _bias_kernel,
        grid=(NA_HEADS, NA_WIN_H),
        in_specs=[pl.BlockSpec(memory_space=pltpu.SMEM)],
        out_specs=pl.BlockSpec((1, 1, GRID_W, NA_WIN_H * GRID_W), lambda h, dl: (h, dl, 0, 0)),
        out_shape=jax.ShapeDtypeStruct((NA_HEADS, NA_WIN_H, GRID_W, NA_WIN_H * GRID_W), F32),
        name="na_bias",
    )(flat)


def _na_kernel(q_ref, kp_ref, kc_ref, kn_ref, vp_ref, vc_ref, vn_ref, bias_ref, o_ref,
               k_scr, v_scr, *, rows):
    j = pl.program_id(1)
    blk = NA_ROWS * GRID_W
    k_scr[0:blk] = kp_ref[0]
    k_scr[blk:2 * blk] = kc_ref[0]
    k_scr[2 * blk:3 * blk] = kn_ref[0]
    v_scr[0:blk] = vp_ref[0]
    v_scr[blk:2 * blk] = vc_ref[0]
    v_scr[2 * blk:3 * blk] = vn_ref[0]
    lane = lax.broadcasted_iota(jnp.int32, (GRID_W, LANES), 1)
    low = lane < HEAD_DIM
    win = NA_WIN_H * GRID_W

    def window_start(i):
        r = NA_ROWS * j + i
        r0 = jnp.clip(r - NA_WIN_H // 2, 0, rows - NA_WIN_H)
        return r - r0, pl.multiple_of((r0 - NA_ROWS * j + NA_ROWS) * GRID_W, GRID_W)

    tiles = []
    for i in range(NA_ROWS):
        delta, start = window_start(i)
        for p in range(NA_HEADS // 2):
            lsl = slice(p * LANES, (p + 1) * LANES)
            qp = q_ref[0, i * GRID_W:(i + 1) * GRID_W, lsl]
            kw = k_scr[pl.ds(start, win), lsl]
            for a in range(2):
                qm = jnp.where(low if a == 0 else jnp.logical_not(low), qp, jnp.zeros_like(qp))
                tiles.append(_dot_nt(qm, kw) + bias_ref[2 * p + a, delta])
    sc = jnp.concatenate(tiles, axis=0)
    e = jnp.exp(sc - jnp.max(sc, axis=-1, keepdims=True))
    inv = 1.0 / jnp.sum(e, axis=-1, keepdims=True)
    eb = e.astype(BF16)
    t = 0
    for i in range(NA_ROWS):
        _, start = window_start(i)
        for p in range(NA_HEADS // 2):
            lsl = slice(p * LANES, (p + 1) * LANES)
            vw = v_scr[pl.ds(start, win), lsl]
            res = []
            for a in range(2):
                rsl = slice(t * GRID_W, (t + 1) * GRID_W)
                res.append(_dot(eb[rsl], vw) * inv[rsl])
                t += 1
            o_ref[0, i * GRID_W:(i + 1) * GRID_W, lsl] = jnp.where(low, res[0], res[1])


def _na_call(qa, ka, va, bias_tab):
    b, s, w = qa.shape
    rows = s // GRID_W
    nb = rows // NA_ROWS
    blk = NA_ROWS * GRID_W
    cur = pl.BlockSpec((1, blk, w), lambda bi, j: (bi, j, 0))
    prev = pl.BlockSpec((1, blk, w), lambda bi, j: (bi, jnp.maximum(j - 1, 0), 0))
    nxt = pl.BlockSpec((1, blk, w), lambda bi, j: (bi, jnp.minimum(j + 1, nb - 1), 0))
    return pl.pallas_call(
        functools.partial(_na_kernel, rows=rows),
        grid=(b, nb),
        in_specs=[cur, prev, cur, nxt, prev, cur, nxt,
                  pl.BlockSpec(bias_tab.shape, lambda bi, j: (0, 0, 0, 0))],
        out_specs=pl.BlockSpec((1, blk, w), lambda bi, j: (bi, j, 0)),
        out_shape=jax.ShapeDtypeStruct((b, s, w), F32),
        scratch_shapes=[pltpu.VMEM((3 * blk, w), BF16), pltpu.VMEM((3 * blk, w), BF16)],
        compiler_params=pltpu.CompilerParams(
            dimension_semantics=("arbitrary", "arbitrary"), vmem_limit_bytes=VMEM_LIMIT),
        name="na_attn",
    )(qa, ka, ka, ka, va, va, va, bias_tab)


def _mla_kernel(q_ref, k_ref, vt_ref, o_ref, s_scr, acc_scr, m_scr, *, n_chunks):
    qt = q_ref[0]

    def scores(c):
        kb = k_ref[0, pl.ds(pl.multiple_of(c * TOK_TILE, TOK_TILE), TOK_TILE), :]
        return _dot(kb, qt)

    def update(st, c):
        m = m_scr[...]
        m_new = jnp.maximum(m, jnp.max(st, axis=0, keepdims=True))
        alpha = jnp.exp2(m - m_new)
        p = jnp.exp2(st - m_new).astype(BF16)
        acc_scr[...] = alpha * acc_scr[...] + _dot(vt_ref[0, 0, c], p)
        m_scr[...] = m_new

    m_scr[...] = jnp.full(m_scr.shape, NEG_BIG, F32)
    acc_scr[...] = jnp.zeros(acc_scr.shape, F32)
    s_scr[0] = scores(0)

    def body(i, carry):
        for j in range(MLA_UNROLL):
            c = MLA_UNROLL * i + j
            s_scr[(j + 1) % 2] = scores(jnp.minimum(c + 1, n_chunks - 1))
            update(s_scr[j % 2], c)
        return carry

    lax.fori_loop(0, n_chunks // MLA_UNROLL, body, 0)
    acc = acc_scr[...]
    o_ref[0] = acc[0:MLA_V, :] / acc[MLA_V:MLA_V + 1, :]


def _mla_call(qb, kb, vbt):
    b, s, _ = kb.shape
    nt = s // TOK_TILE
    assert nt % MLA_UNROLL == 0 and MLA_UNROLL % 2 == 0
    return pl.pallas_call(
        functools.partial(_mla_kernel, n_chunks=nt),
        grid=(b, MLA_HEADS, nt),
        in_specs=[pl.BlockSpec((1, HEAD_PAD, TOK_TILE), lambda bi, h, qi: (bi, h, qi)),
                  pl.BlockSpec((1, s, HEAD_PAD), lambda bi, h, qi: (bi, 0, h)),
                  pl.BlockSpec((1, 1, nt, MLA_VROWS, TOK_TILE), lambda bi, h, qi: (bi, h, 0, 0, 0))],
        out_specs=pl.BlockSpec((1, MLA_V, TOK_TILE), lambda bi, h, qi: (bi, h, qi)),
        out_shape=jax.ShapeDtypeStruct((b, B_WIDTH, s), F32),
        scratch_shapes=[pltpu.VMEM((2, TOK_TILE, TOK_TILE), F32), pltpu.VMEM((MLA_VROWS, TOK_TILE), F32),
                        pltpu.VMEM((1, TOK_TILE), F32)],
        compiler_params=pltpu.CompilerParams(
            dimension_semantics=("arbitrary", "arbitrary", "arbitrary"), vmem_limit_bytes=VMEM_LIMIT),
        name="mla_attn",
    )(qb, kb, vbt)


def _out_kernel(x_ref, oa_ref, obt_ref, u_ref, up_ref, un_ref, bc_ref, ga_ref, gb_ref, gc_ref,
                woa_ref, wob_ref, woc_ref, cw_ref, nffn_ref, wrh_ref, wrl_ref, br_ref,
                x1_o, xn_o, aff_o):
    ti = pl.program_id(1)
    nt = pl.num_programs(1)
    t = TOK_TILE

    na = _rms(oa_ref[0], ga_ref[...]).astype(BF16)
    acc = _dot(na, woa_ref[...])

    obt = obt_ref[0]
    nbt = obt * lax.rsqrt(jnp.mean(obt * obt, axis=0, keepdims=True) + EPS) * gb_ref[...]
    acc = acc + _dot(nbt.T.astype(BF16), wob_ref[...])

    u = u_ref[0]
    row = lax.broadcasted_iota(jnp.int32, (t, CONV_CH), 0)
    prev_row = jnp.where(ti > 0, up_ref[0, 7:8, :], 0.0)
    next_row = jnp.where(ti < nt - 1, un_ref[0, 0:1, :], 0.0)
    u_m1 = jnp.where(row == 0, prev_row, pltpu.roll(u, 1, 0))
    u_p1 = jnp.where(row == t - 1, next_row, pltpu.roll(u, t - 1, 0))
    y = cw_ref[0:1, :] * u_m1 + cw_ref[1:2, :] * u + cw_ref[2:3, :] * u_p1
    oc = _rms(bc_ref[0] * y, gc_ref[...]).astype(BF16)
    acc = acc + _dot(oc, woc_ref[...])

    x1 = x_ref[0] + acc
    x1_o[0] = x1
    xn = _rms(x1, nffn_ref[...])
    hi = xn.astype(BF16)
    lo = (xn - hi.astype(F32)).astype(BF16)
    xn_o[0] = hi
    logits = _dot(hi, wrh_ref[...]) + _dot(lo, wrh_ref[...]) + _dot(hi, wrl_ref[...]) + br_ref[...]
    m = jnp.max(logits, axis=-1, keepdims=True)
    e = jnp.exp(logits - m)
    aff = e / jnp.sum(e, axis=-1, keepdims=True)
    aff_o[0] = aff.T[0:N_EXPERTS, :]


def _out_call(x, out_a, out_bt, u, bc, lw):
    b, s, d = x.shape
    nt = s // TOK_TILE
    h8 = TOK_TILE // 8

    def const(a):
        return pl.BlockSpec(a.shape, lambda bi, ti: (0,) * a.ndim)

    def tok(width):
        return pl.BlockSpec((1, TOK_TILE, width), lambda bi, ti: (bi, ti, 0))

    consts = [lw['ga'], lw['gb_col'], lw['gc'], lw['wo_a'], lw['wo_b'], lw['wo_c'], lw['conv_w'],
              lw['nffn'], lw['wr_hi'], lw['wr_lo'], lw['b_r']]
    in_specs = [tok(d), tok(A_WIDTH),
                pl.BlockSpec((1, B_WIDTH, TOK_TILE), lambda bi, ti: (bi, 0, ti)),
                tok(CONV_CH),
                pl.BlockSpec((1, 8, CONV_CH), lambda bi, ti: (bi, jnp.maximum(ti * h8 - 1, 0), 0)),
                pl.BlockSpec((1, 8, CONV_CH), lambda bi, ti: (bi, jnp.minimum((ti + 1) * h8, s // 8 - 1), 0)),
                tok(CONV_CH)] + [const(c) for c in consts]
    return pl.pallas_call(
        _out_kernel,
        grid=(b, nt),
        in_specs=in_specs,
        out_specs=[tok(d), tok(d), pl.BlockSpec((1, N_EXPERTS, TOK_TILE), lambda bi, ti: (bi, 0, ti))],
        out_shape=[jax.ShapeDtypeStruct((b, s, d), F32), jax.ShapeDtypeStruct((b, s, d), BF16),
                   jax.ShapeDtypeStruct((b, N_EXPERTS, s), F32)],
        compiler_params=pltpu.CompilerParams(
            dimension_semantics=("arbitrary", "arbitrary"), vmem_limit_bytes=VMEM_LIMIT),
        name="out_proj",
    )(x, out_a, out_bt, u, u, u, bc, *consts)


def _topk_kernel(a_ref, tri_ref, o_slot, o_gate, o_off, *, cap):
    def total(mask):
        v = jnp.where(mask, 1.0, 0.0)
        return jnp.sum(jnp.sum(v, axis=0, keepdims=True), axis=1, keepdims=True)

    group = range(TOPK_GROUP)
    all_bits = [pltpu.bitcast(a_ref[0, t], jnp.int32) for t in group]
    thrs = [jnp.zeros((1, 1), jnp.int32) for _ in group]
    for bit in range(30, -1, -1):
        cands = [thrs[t] | (1 << bit) for t in group]
        thrs = [jnp.where(total(all_bits[t] >= cands[t]) >= cap, cands[t], thrs[t]) for t in group]
    for t in group:
        _topk_finish(a_ref[0, t], all_bits[t], thrs[t], total, tri_ref, o_slot, o_gate, o_off, t, cap)


def _topk_finish(a, bits, thr, total, tri_ref, o_slot, o_gate, o_off, t, cap):
    gt = bits > thr
    eq = bits == thr
    need = cap - total(gt)

    upper = tri_ref[0]
    lower_strict = tri_ref[1]
    upper_strict = tri_ref[2]

    def excl_rank(mask):
        f = jnp.where(mask, 1.0, 0.0)
        fb = f.astype(BF16)
        in_row = _dot(fb, upper)
        before_rows = jnp.sum(_dot(lower_strict, fb), axis=1, keepdims=True)
        return before_rows + in_row - f, fb

    eq_rank, _ = excl_rank(eq)
    sel = gt | (eq & (eq_rank < need))
    pos, sel_b = excl_rank(sel)
    o_slot[0, t] = jnp.where(sel, pos, -1.0).astype(jnp.int32)
    o_gate[0, t] = jnp.where(sel, a, 0.0)
    row_tot = _dot_nt(jnp.ones((8, LANES), BF16), sel_b)
    o_off[0, t] = _dot(row_tot.astype(BF16), upper_strict)


def _topk_call(aff_t, tri, cap):
    b, e, s = aff_t.shape
    pad = TOPK_ROWS * LANES - s
    a = aff_t if pad == 0 else jnp.pad(aff_t, ((0, 0), (0, 0), (0, pad)), constant_values=-1.0)
    a = a.reshape(b, e, TOPK_ROWS, LANES)
    blk = pl.BlockSpec((1, TOPK_GROUP, TOPK_ROWS, LANES), lambda bi, ei: (bi, ei, 0, 0))
    return pl.pallas_call(
        functools.partial(_topk_kernel, cap=cap),
        grid=(b, e // TOPK_GROUP),
        in_specs=[blk, pl.BlockSpec(tri.shape, lambda bi, ei: (0, 0, 0))],
        out_specs=[blk, blk, pl.BlockSpec((1, TOPK_GROUP, 8, LANES), lambda bi, ei: (bi, ei, 0, 0))],
        out_shape=[jax.ShapeDtypeStruct((b, e, TOPK_ROWS, LANES), jnp.int32),
                   jax.ShapeDtypeStruct((b, e, TOPK_ROWS, LANES), F32),
                   jax.ShapeDtypeStruct((b, e, 8, LANES), F32)],
        name="topk_select",
    )(a, tri)


def _window_onehot(slot_row, start):
    i = lax.broadcasted_iota(jnp.int32, (2 * MOE_CHUNK, MOE_CHUNK), 0)
    return jnp.where(i == slot_row - start, 1.0, 0.0).astype(BF16)


def _window_tile(off, tiles):
    return jnp.minimum(lax.shift_right_logical(off, 8), tiles - 1)


def _gather_kernel(coff_ref, x_ref, slot_ref, o_ref, acc_ref, *, n_chunks, cap):
    bi = pl.program_id(0)
    ei = pl.program_id(2)
    base = (bi * N_EXPERTS + ei) * (n_chunks + 1)
    acc_ref[...] = jnp.zeros_like(acc_ref)

    row = lax.broadcasted_iota(jnp.int32, (GATHER_WIN, MOE_CHUNK), 0)

    def body(k, carry):
        off = coff_ref[base + k]
        start = pl.multiple_of(jnp.minimum(lax.shift_left(lax.shift_right_logical(off, 3), 3), cap), 8)
        g = jnp.where(row == slot_ref[0, 0, k] - start, 1.0, 0.0).astype(BF16)
        xk = x_ref[0, pl.ds(pl.multiple_of(k * MOE_CHUNK, MOE_CHUNK), MOE_CHUNK), :]
        acc_ref[pl.ds(start, GATHER_WIN), :] += _dot(g, xk)
        return carry

    lax.fori_loop(0, n_chunks, body, 0, unroll=8)
    o_ref[0, 0] = acc_ref[0:cap, :].astype(BF16)


def _gather_call(coff, xn, slot5, cap):
    b, s, d = xn.shape
    n_chunks = s // MOE_CHUNK
    ncq = d // GATHER_COLS
    grid_spec = pltpu.PrefetchScalarGridSpec(
        num_scalar_prefetch=1,
        grid=(b, ncq, N_EXPERTS),
        in_specs=[pl.BlockSpec((1, s, GATHER_COLS), lambda bi, cq, ei, co: (bi, 0, cq)),
                  pl.BlockSpec((1, 1, n_chunks, 1, MOE_CHUNK), lambda bi, cq, ei, co: (bi, ei, 0, 0, 0))],
        out_specs=pl.BlockSpec((1, 1, cap, GATHER_COLS), lambda bi, cq, ei, co: (bi, ei, 0, cq)),
        scratch_shapes=[pltpu.VMEM((cap + GATHER_WIN, GATHER_COLS), F32)],
    )
    return pl.pallas_call(
        functools.partial(_gather_kernel, n_chunks=n_chunks, cap=cap),
        grid_spec=grid_spec,
        out_shape=jax.ShapeDtypeStruct((b, N_EXPERTS, cap, d), BF16),
        compiler_params=pltpu.CompilerParams(
            dimension_semantics=("arbitrary", "arbitrary", "arbitrary"), vmem_limit_bytes=VMEM_LIMIT),
        name="moe_gather",
    )(coff, xn, slot5)


def _ffn_kernel(xc_ref, wg_ref, wu_ref, wdt_ref, o_ref, *, tiles):
    xc = xc_ref[0, 0]
    g = _dot(xc, wg_ref[0])
    u = _dot(xc, wu_ref[0])
    h = (g / (1.0 + jnp.exp(-g)) * u).astype(BF16)
    yt = _dot_nt(wdt_ref[0], h)
    for r in range(tiles):
        o_ref[0, 0, r] = yt[:, r * MOE_CHUNK:(r + 1) * MOE_CHUNK].astype(BF16)


def _ffn_call(xc, wg, wu, wdt):
    b, e, cap, d = xc.shape
    ff = wg.shape[-1]
    rt = min(cap, 1024)
    tiles = rt // MOE_CHUNK
    return pl.pallas_call(
        functools.partial(_ffn_kernel, tiles=tiles),
        grid=(e, b, cap // rt),
        in_specs=[pl.BlockSpec((1, 1, rt, d), lambda ei, bi, ri: (bi, ei, ri, 0)),
                  pl.BlockSpec((1, d, ff), lambda ei, bi, ri: (ei, 0, 0)),
                  pl.BlockSpec((1, d, ff), lambda ei, bi, ri: (ei, 0, 0)),
                  pl.BlockSpec((1, d, ff), lambda ei, bi, ri: (ei, 0, 0))],
        out_specs=pl.BlockSpec((1, 1, tiles, d, MOE_CHUNK), lambda ei, bi, ri: (bi, ei, ri, 0, 0)),
        out_shape=jax.ShapeDtypeStruct((b, e, cap // MOE_CHUNK, d, MOE_CHUNK), BF16),
        compiler_params=pltpu.CompilerParams(
            dimension_semantics=("arbitrary", "arbitrary", "arbitrary"), vmem_limit_bytes=VMEM_LIMIT),
        name="moe_ffn",
    )(xc, wg, wu, wdt)


def _scatter_kernel(coff_ref, y_ref, slot_ref, gate_ref, o_ref, *, n_chunks, tiles):
    bi = pl.program_id(0)
    kg = pl.program_id(2)
    for j in range(SCATTER_CHUNKS):
        k = kg * SCATTER_CHUNKS + j
        acc = jnp.zeros((MOE_COLS, MOE_CHUNK), F32)
        for e in range(N_EXPERTS):
            r0 = _window_tile(coff_ref[(bi * N_EXPERTS + e) * (n_chunks + 1) + k], tiles)
            r1 = jnp.minimum(r0 + 1, tiles - 1)
            g = _window_onehot(slot_ref[0, e, j], r0 * MOE_CHUNK)
            yw = jnp.concatenate([y_ref[0, e, r0], y_ref[0, e, r1]], axis=1)
            acc = acc + _dot(yw, g) * gate_ref[0, e, j]
        o_ref[0, 0, j] = acc


def _scatter_call(coff, y5, slot5, gate5):
    b, e, tiles, d, _ = y5.shape
    n_chunks = slot5.shape[2]
    ncq = d // MOE_COLS
    small = pl.BlockSpec((1, e, SCATTER_CHUNKS, 1, MOE_CHUNK), lambda bi, cq, kg, co: (bi, 0, kg, 0, 0))
    grid_spec = pltpu.PrefetchScalarGridSpec(
        num_scalar_prefetch=1,
        grid=(b, ncq, n_chunks // SCATTER_CHUNKS),
        in_specs=[pl.BlockSpec((1, e, tiles, MOE_COLS, MOE_CHUNK), lambda bi, cq, kg, co: (bi, 0, 0, cq, 0)),
                  small, small],
        out_specs=pl.BlockSpec((1, 1, SCATTER_CHUNKS, MOE_COLS, MOE_CHUNK),
                               lambda bi, cq, kg, co: (bi, cq, kg, 0, 0)),
    )
    return pl.pallas_call(
        functools.partial(_scatter_kernel, n_chunks=n_chunks, tiles=tiles),
        grid_spec=grid_spec,
        out_shape=jax.ShapeDtypeStruct((b, ncq, n_chunks, MOE_COLS, MOE_CHUNK), F32),
        compiler_params=pltpu.CompilerParams(
            dimension_semantics=("arbitrary", "arbitrary", "arbitrary"), vmem_limit_bytes=VMEM_LIMIT),
        name="moe_scatter",
    )(coff, y5, slot5, gate5)


def _final_kernel(x_ref, moe_ref, o_ref):
    o_ref[0] = x_ref[0] + _moe_tile_to_rows(moe_ref, TOK_TILE // MOE_CHUNK)


def _final_call(x1, moe):
    b, s, d = x1.shape
    ncq = d // MOE_COLS
    tok = pl.BlockSpec((1, TOK_TILE, d), lambda bi, ti: (bi, ti, 0))
    return pl.pallas_call(
        _final_kernel,
        grid=(b, s // TOK_TILE),
        in_specs=[tok, pl.BlockSpec((1, ncq, TOK_TILE // MOE_CHUNK, MOE_COLS, MOE_CHUNK),
                                    lambda bi, ti: (bi, 0, ti, 0, 0))],
        out_specs=tok,
        out_shape=jax.ShapeDtypeStruct((b, s, d), F32),
        compiler_params=pltpu.CompilerParams(
            dimension_semantics=("arbitrary", "arbitrary"), vmem_limit_bytes=VMEM_LIMIT),
        name="final_add",
    )(x1, moe)


def _block_ones(n, group):
    i = np.arange(n)
    return jnp.asarray((i[:, None] // group) == (i[None, :] // group), BF16)


def _tables(s):
    half = MLA_ROPE // 2
    inv = ROPE_THETA ** (-jnp.arange(half, dtype=F32) / half)
    ang = jnp.arange(s, dtype=jnp.int32).astype(F32)[:, None] * inv[None, :]
    cos, sin = jnp.cos(ang), jnp.sin(ang)
    ones = jnp.ones((s, MLA_NOPE), F32)
    tail = HEAD_PAD - MLA_QK
    cos_t = jnp.concatenate([ones, cos, cos, jnp.ones((s, tail), F32)], axis=1)
    sin_t = jnp.concatenate([0 * ones, -sin, sin, jnp.zeros((s, tail), F32)], axis=1)
    i = np.arange(LANES)
    tri = np.stack([i[:, None] <= i[None, :], i[None, :] < i[:, None], i[:, None] < i[None, :]])
    return {'cos': cos_t, 'sin': sin_t, 'e384': _block_ones(2 * LANES, HEAD_DIM),
            'e768': _block_ones(2 * LANES, HEAD_PAD), 'tri': jnp.asarray(tri, BF16)}


def _pad_heads(v, scale=1.0):
    g = jnp.pad(v.astype(F32) * scale, (0, HEAD_PAD - MLA_QK))
    return jnp.tile(g, MLA_HEADS)[None, :]


def _layer_weights(l, p):
    w_in = p['w_in'][l]
    kr_end = 3 * A_WIDTH + MLA_Q_RANK + MLA_KV_RANK + MLA_ROPE
    w_in_r = jnp.concatenate(
        [w_in[:, :kr_end], jnp.zeros((D_MODEL, LANES - MLA_ROPE), F32), w_in[:, kr_end:]], axis=1)
    w_uq = p['w_uq'][l].reshape(MLA_Q_RANK, MLA_HEADS, MLA_QK)
    w_uq = jnp.pad(w_uq, ((0, 0), (0, 0), (0, HEAD_PAD - MLA_QK))).reshape(MLA_Q_RANK, QB_WIDTH)
    w_ukv = p['w_ukv'][l].reshape(MLA_KV_RANK, MLA_HEADS, MLA_NOPE + MLA_V)
    w_uk = jnp.pad(w_ukv[:, :, :MLA_NOPE], ((0, 0), (0, 0), (0, HEAD_PAD - MLA_NOPE)))
    w_uk = w_uk.reshape(MLA_KV_RANK, QB_WIDTH)
    w_uv = w_ukv[:, :, MLA_NOPE:].reshape(MLA_KV_RANK, B_WIDTH)
    place = np.zeros((LANES, QB_WIDTH), np.float32)
    for h in range(MLA_HEADS):
        place[np.arange(MLA_ROPE), h * HEAD_PAD + MLA_NOPE + np.arange(MLA_ROPE)] = 1.0
    w_kk = jnp.concatenate([w_uk, jnp.asarray(place)], axis=0)
    out_norm = p['out_norm'][l]
    w_out = p['w_out'][l]
    w_r = jnp.pad(p['w_router'][l], ((0, 0), (0, LANES - N_EXPERTS)))
    wr_hi = w_r.astype(BF16)
    b_r = jnp.concatenate([p['b_router'][l].astype(F32), jnp.full((LANES - N_EXPERTS,), NEG_BIG, F32)])
    return {
        'nmix': p['norm_mix'][l][None, :], 'w_in': w_in_r.astype(BF16),
        'gqa': jnp.tile(p['q_norm_a'][l] * HEAD_DIM ** -0.5, NA_HEADS)[None, :],
        'gka': jnp.tile(p['k_norm_a'][l], NA_HEADS)[None, :],
        'cqn': p['cq_norm'][l][None, :], 'w_uq': w_uq.astype(BF16),
        'ckvn': p['ckv_norm'][l][None, :], 'w_kk': w_kk.astype(BF16), 'w_uv': w_uv.astype(BF16),
        'gqb': _pad_heads(p['q_norm_b'][l], MLA_QK ** -0.5 * LOG2_E), 'gkb': _pad_heads(p['k_norm_b'][l]),
        'ga': out_norm[None, :A_WIDTH], 'gb_col': out_norm[A_WIDTH:A_WIDTH + B_WIDTH, None],
        'gc': out_norm[None, A_WIDTH + B_WIDTH:],
        'wo_a': w_out[:A_WIDTH].astype(BF16), 'wo_b': w_out[A_WIDTH:A_WIDTH + B_WIDTH].astype(BF16),
        'wo_c': w_out[A_WIDTH + B_WIDTH:].astype(BF16),
        'conv_w': p['conv_w'][l], 'nffn': p['norm_ffn'][l][None, :],
        'wr_hi': wr_hi, 'wr_lo': (w_r - wr_hi.astype(F32)).astype(BF16), 'b_r': b_r[None, :],
        'w_gate': p['w_gate'][l].astype(BF16), 'w_up': p['w_up'][l].astype(BF16),
        'w_down_t': jnp.swapaxes(p['w_down'][l], 1, 2).astype(BF16),
    }


def _layer(x, moe, lw, tabs, rpb_l):
    b, s, d = x.shape
    cap = CAPACITY_FACTOR * s // N_EXPERTS
    n_chunks = s // MOE_CHUNK

    outs = _proj_call(x, moe, lw, tabs)
    if moe is not None:
        qa, ka, va, qb, kb, vbt, u, bc, x = outs
    else:
        qa, ka, va, qb, kb, vbt, u, bc = outs
    out_a = _na_call(qa, ka, va, _bias_call(rpb_l))
    out_bt = _mla_call(qb, kb, vbt)
    x1, xn, aff_t = _out_call(x, out_a, out_bt, u, bc, lw)

    slot, gate, off = _topk_call(aff_t, tabs['tri'], cap)
    rows_per_chunk = MOE_CHUNK // LANES
    coff = off[:, :, 0, :s // LANES:rows_per_chunk].astype(jnp.int32)
    coff = jnp.concatenate([coff, jnp.full((b, N_EXPERTS, 1), cap, jnp.int32)], axis=-1).reshape(-1)
    slot5 = slot.reshape(b, N_EXPERTS, -1)[:, :, :s].reshape(b, N_EXPERTS, n_chunks, 1, MOE_CHUNK)
    gate5 = gate.reshape(b, N_EXPERTS, -1)[:, :, :s].reshape(b, N_EXPERTS, n_chunks, 1, MOE_CHUNK)

    xc = _gather_call(coff, xn, slot5, cap)
    y5 = _ffn_call(xc, lw['w_gate'], lw['w_up'], lw['w_down_t'])
    moe_out = _scatter_call(coff, y5, slot5, gate5)
    return x1, moe_out


def kernel(x, norm_mix, w_in, q_norm_a, k_norm_a, rpb, cq_norm, w_uq, ckv_norm, w_ukv, q_norm_b,
           k_norm_b, conv_w, out_norm, w_out, norm_ffn, w_router, b_router, w_gate, w_up, w_down):
    p = dict(norm_mix=norm_mix, w_in=w_in, q_norm_a=q_norm_a, k_norm_a=k_norm_a, cq_norm=cq_norm,
             w_uq=w_uq, ckv_norm=ckv_norm, w_ukv=w_ukv, q_norm_b=q_norm_b, k_norm_b=k_norm_b,
             conv_w=conv_w, out_norm=out_norm, w_out=w_out, norm_ffn=norm_ffn, w_router=w_router,
             b_router=b_router, w_gate=w_gate, w_up=w_up, w_down=w_down)
    b, s, d = x.shape
    assert d == D_MODEL and (CAPACITY_FACTOR * s // N_EXPERTS) % MOE_CHUNK == 0
    assert s % TOK_TILE == 0 and (s // GRID_W) % NA_ROWS == 0 and s <= TOPK_ROWS * LANES
    tabs = _tables(s)
    moe = None
    for l in range(norm_mix.shape[0]):
        x, moe = _layer(x, moe, _layer_weights(l, p), tabs, rpb[l])
    return _final_call(x, moe)
```
